```python
import math
import jax, jax.numpy as jnp
from jax import lax
import numpy as np

D_MODEL = 1024
BATCH = 8
SEQ = 2048
DEPTH = 2
DEC_BATCH = 128
DEC_SEQ = 8
PAST_LEN = 16384
PAGE_SIZE = 128

GLA_HEADS = 4
GLA_DK = D_MODEL // 8
GLA_DV = D_MODEL // 4
GLA_GATE_RANK = 16
GLA_GATE_NORM = 16.0
GLA_CHUNK = 64
RWKV_HEAD = 64
RWKV_HEADS = D_MODEL // RWKV_HEAD
RWKV_WIDTH = RWKV_HEADS * RWKV_HEAD
RWKV_DECAY_LORA = 64
RWKV_A_LORA = 64
RWKV_G_LORA = 128
RWKV_GN_EPS = 64e-5
LRU_WIDTH = D_MODEL
LRU_BLOCKS = 8
LRU_BLOCK = LRU_WIDTH // LRU_BLOCKS
LRU_C = 8.0
CONV_W = 4
SSD_INNER = D_MODEL
SSD_HEADDIM = 64
SSD_HEADS = SSD_INNER // SSD_HEADDIM
SSD_GROUPS = 2
SSD_STATE = 128
SSD_CHUNK = 64
SSD_CONV_CH = SSD_INNER + 2 * SSD_GROUPS * SSD_STATE
D_FF = -(-8 * D_MODEL // (3 * 256)) * 256
NORM_EPS = 1e-6
GLA_COLS = 2 * GLA_HEADS * GLA_DK + 2 * GLA_HEADS * GLA_DV + GLA_GATE_RANK
RWKV_COLS = 3 * RWKV_WIDTH + RWKV_DECAY_LORA + RWKV_A_LORA + RWKV_G_LORA
IN0 = GLA_COLS + RWKV_COLS
MIX0 = GLA_HEADS * GLA_DV + RWKV_WIDTH
IN1 = 2 * LRU_WIDTH + SSD_INNER + SSD_CONV_CH + SSD_HEADS
MIX1 = LRU_WIDTH + SSD_INNER

kernel_name = 'hybrid_gla_rwkv7_rglru_ssd_decode_step'


def rmsnorm(x, g, eps=NORM_EPS):
    xf = x.astype(jnp.float32)
    y = xf * lax.rsqrt(jnp.mean(xf * xf, axis=-1, keepdims=True) + eps)
    return (y * g.astype(jnp.float32)).astype(x.dtype)


def split_cols(t, sizes):
    idx = np.cumsum(sizes)[:-1].tolist()
    return jnp.split(t, idx, axis=-1)


def pad_time(t, total):
    pad = total - t.shape[1]
    return jnp.pad(t, [(0, 0), (0, pad)] + [(0, 0)] * (t.ndim - 2))


def causal_dwconv(u, buf, w, b):
    t_len = u.shape[1]
    full = jnp.concatenate([buf.astype(u.dtype), u], axis=1)
    y = b
    for j in range(CONV_W):
        y = y + full[:, j:j + t_len] * w[j]
    return y, full[:, -(CONV_W - 1):]


def gla_chunked(q, k, v, log_a, s0):
    f32 = jnp.float32
    bsz, t_len, nh, _ = q.shape
    dv = v.shape[-1]
    c = min(GLA_CHUNK, t_len)
    n = -(-t_len // c)
    q, k, v, log_a = [pad_time(t.astype(f32), n * c).reshape(bsz, n, c, nh, t.shape[-1]) for t in (q, k, v, log_a)]
    b = jnp.cumsum(log_a, axis=2)
    b_last = b[:, :, -1]
    qd = q * jnp.exp(b)
    kd = k * jnp.exp(-b)
    mask = jnp.tril(jnp.ones((c, c), bool))
    scores = jnp.where(mask, jnp.einsum('bnihd,bnjhd->bnhij', qd, kd), 0.0)
    o_intra = jnp.einsum('bnhij,bnjhv->bnihv', scores, v)
    kc = k * jnp.exp(b_last[:, :, None] - b)
    d_state = jnp.einsum('bnjhd,bnjhv->bnhdv', kc, v)
    decay = jnp.exp(b_last)

    def step(s, inp):
        dec, ds = inp
        return s * dec[..., None] + ds, s

    s_fin, s_in = lax.scan(step, s0.astype(f32), (jnp.moveaxis(decay, 1, 0), jnp.moveaxis(d_state, 1, 0)))
    s_in = jnp.moveaxis(s_in, 0, 1)
    o = o_intra + jnp.einsum('bnihd,bnhdv->bnihv', qd, s_in)
    return o.reshape(bsz, n * c, nh, dv)[:, :t_len], s_fin


def rwkv7_scan(r, w, k, v, kk, a, s0):
    def step(s, inp):
        r_t, w_t, k_t, v_t, kk_t, a_t = inp
        sa = jnp.einsum('bhij,bhj->bhi', s, -kk_t)
        s = s * w_t[:, :, None, :] + sa[..., None] * (kk_t * a_t)[:, :, None, :] + v_t[..., None] * k_t[:, :, None, :]
        return s, jnp.einsum('bhij,bhj->bhi', s, r_t)

    xs = tuple(jnp.moveaxis(t, 1, 0) for t in (r, w, k, v, kk, a))
    s_fin, y = lax.scan(step, s0.astype(jnp.float32), xs)
    return jnp.moveaxis(y, 0, 1), s_fin


def ssd_chunked(x, dt, A, bm, cm, s0):
    f32 = jnp.float32
    bsz, t_len, nh, hp = x.shape
    ng, ns = bm.shape[2], bm.shape[3]
    hg = nh // ng
    c = min(SSD_CHUNK, t_len)
    n = -(-t_len // c)
    tot = n * c
    x = pad_time(x.astype(f32), tot).reshape(bsz, n, c, ng, hg, hp)
    dt = pad_time(dt.astype(f32), tot).reshape(bsz, n, c, ng, hg)
    bm = pad_time(bm.astype(f32), tot).reshape(bsz, n, c, ng, ns)
    cm = pad_time(cm.astype(f32), tot).reshape(bsz, n, c, ng, ns)
    cs = jnp.cumsum(dt * A.astype(f32).reshape(ng, hg), axis=2)
    seg = cs[:, :, :, None] - cs[:, :, None, :]
    mask = jnp.tril(jnp.ones((c, c), bool))[:, :, None, None]
    lmat = jnp.exp(jnp.where(mask, seg, -jnp.inf))
    xdt = x * dt[..., None]
    cb = jnp.einsum('bnigs,bnjgs->bngij', cm, bm)
    y_intra = jnp.einsum('bngij,bnijgh,bnjghp->bnighp', cb, lmat, xdt)
    cs_last = cs[:, :, -1]
    wts = jnp.exp(cs_last[:, :, None] - cs)
    d_state = jnp.einsum('bnjgs,bnjgh,bnjghp->bnghps', bm, wts, xdt)
    decay = jnp.exp(cs_last)

    def step(s, inp):
        dec, ds = inp
        return s * dec[..., None, None] + ds, s

    s_init = s0.astype(f32).reshape(bsz, ng, hg, hp, ns)
    s_fin, s_in = lax.scan(step, s_init, (jnp.moveaxis(decay, 1, 0), jnp.moveaxis(d_state, 1, 0)))
    s_in = jnp.moveaxis(s_in, 0, 1)
    y_inter = jnp.einsum('bnigs,bnigh,bnghps->bnighp', cm, jnp.exp(cs), s_in)
    y = (y_intra + y_inter).reshape(bsz, tot, nh, hp)[:, :t_len]
    return y, s_fin.reshape(bsz, nh, hp, ns)


def mix_ab(h, s_gla, s_rwkv, s_shift, w_in0, gla_w_a2, gla_b_a, gla_g_norm, rwkv_mu, rwkv_w0, rwkv_w2,
           rwkv_a0, rwkv_a2, rwkv_g2, rwkv_k_k, rwkv_k_a, rwkv_r_k, rwkv_ln_w, rwkv_ln_b, w_out0):
    f32 = jnp.float32
    bsz, t_len, _ = h.shape
    proj = h @ w_in0
    gla_cols, rwkv_cols = proj[..., :GLA_COLS], proj[..., GLA_COLS:]
    q, k, v, a_low, og = split_cols(gla_cols, [GLA_HEADS * GLA_DK, GLA_HEADS * GLA_DK, GLA_HEADS * GLA_DV,
                                              GLA_GATE_RANK, GLA_HEADS * GLA_DV])
    q = q.reshape(bsz, t_len, GLA_HEADS, GLA_DK) * (GLA_DK ** -0.5)
    k = k.reshape(bsz, t_len, GLA_HEADS, GLA_DK)
    v = v.reshape(bsz, t_len, GLA_HEADS, GLA_DV)
    log_a = jax.nn.log_sigmoid((a_low @ gla_w_a2 + gla_b_a).astype(f32)) / GLA_GATE_NORM
    log_a = log_a.reshape(bsz, t_len, GLA_HEADS, GLA_DK)
    o, s_gla_new = gla_chunked(q, k, v, log_a, s_gla)
    o = rmsnorm(o, gla_g_norm, 1e-5).reshape(bsz, t_len, GLA_HEADS * GLA_DV)
    o_gla = o * jax.nn.silu(og.astype(f32))
    prev = jnp.concatenate([s_shift[:, None].astype(rwkv_cols.dtype), rwkv_cols[:, :-1]], axis=1)
    mixed = rwkv_cols + (prev - rwkv_cols) * rwkv_mu
    r, kr, vr, w_low, a_lr, g_low = [t.astype(f32) for t in split_cols(
        mixed, [RWKV_WIDTH, RWKV_WIDTH, RWKV_WIDTH, RWKV_DECAY_LORA, RWKV_A_LORA, RWKV_G_LORA])]
    w = -jax.nn.softplus(-(rwkv_w0 + jnp.tanh(w_low) @ rwkv_w2)) - 0.5
    decay = jnp.exp(-jnp.exp(w))
    a = jax.nn.sigmoid(rwkv_a0 + a_lr @ rwkv_a2)
    g = jax.nn.sigmoid(g_low) @ rwkv_g2
    hd = lambda t: t.reshape(bsz, t_len, RWKV_HEADS, RWKV_HEAD)
    kk = hd(kr * rwkv_k_k)
    kk = kk / jnp.maximum(jnp.sqrt(jnp.sum(kk * kk, axis=-1, keepdims=True)), 1e-12)
    kr = kr * (1.0 + (a - 1.0) * rwkv_k_a)
    r_h, k_h, v_h = hd(r), hd(kr), hd(vr)
    y, s_rwkv_new = rwkv7_scan(r_h, hd(decay), k_h, v_h, kk, hd(a), s_rwkv)
    mu = jnp.mean(y, axis=-1, keepdims=True)
    var = jnp.mean(jnp.square(y - mu), axis=-1, keepdims=True)
    y = ((y - mu) * lax.rsqrt(var + RWKV_GN_EPS)).reshape(bsz, t_len, RWKV_WIDTH) * rwkv_ln_w + rwkv_ln_b
    bonus = jnp.sum(r_h * k_h * rwkv_r_k, axis=-1, keepdims=True) * v_h
    y_rwkv = (y + bonus.reshape(bsz, t_len, RWKV_WIDTH)) * g
    out = jnp.concatenate([o_gla, y_rwkv], axis=-1).astype(h.dtype) @ w_out0
    return out, s_gla_new, s_rwkv_new, rwkv_cols[:, -1]


def mix_cd(h, s_lru, s_lru_conv, s_ssd, s_ssd_conv, w_in1, lru_conv_w, lru_conv_b, lru_w_r, lru_b_r, lru_w_i,
           lru_b_i, lru_lambda, ssd_conv_w, ssd_conv_b, ssd_dt_bias, ssd_a_log, ssd_d, ssd_norm_w, w_out1):
    f32 = jnp.float32
    bsz, t_len, _ = h.shape
    proj = h @ w_in1
    gate_br, x_br, z, xbc, dt = split_cols(proj, [LRU_WIDTH, LRU_WIDTH, SSD_INNER, SSD_CONV_CH, SSD_HEADS])
    xc, lru_conv_new = causal_dwconv(x_br, s_lru_conv, lru_conv_w, lru_conv_b)
    xb = xc.astype(f32).reshape(bsz, t_len, LRU_BLOCKS, LRU_BLOCK)
    rg = jax.nn.sigmoid(jnp.einsum('btnd,nde->btne', xb, lru_w_r) + lru_b_r)
    ig = jax.nn.sigmoid(jnp.einsum('btnd,nde->btne', xb, lru_w_i) + lru_b_i)
    log_a = -LRU_C * rg * jax.nn.softplus(-lru_lambda.astype(f32).reshape(LRU_BLOCKS, LRU_BLOCK))
    a = jnp.exp(log_a).reshape(bsz, t_len, LRU_WIDTH)
    bterm = (jnp.sqrt(-jnp.expm1(2.0 * log_a)) * ig * xb).reshape(bsz, t_len, LRU_WIDTH)
    bterm = bterm.at[:, 0].add(a[:, 0] * s_lru.astype(f32))
    comb = lambda l, r: (l[0] * r[0], r[0] * l[1] + r[1])
    _, hseq = lax.associative_scan(comb, (a, bterm), axis=1)
    lru_out = hseq * jax.nn.gelu(gate_br.astype(f32))
    xbc_c, ssd_conv_new = causal_dwconv(xbc, s_ssd_conv, ssd_conv_w, ssd_conv_b)
    xbc_c = jax.nn.silu(xbc_c.astype(f32))
    xs, bm, cm = split_cols(xbc_c, [SSD_INNER, SSD_GROUPS * SSD_STATE, SSD_GROUPS * SSD_STATE])
    xs = xs.reshape(bsz, t_len, SSD_HEADS, SSD_HEADDIM)
    bm = bm.reshape(bsz, t_len, SSD_GROUPS, SSD_STATE)
    cm = cm.reshape(bsz, t_len, SSD_GROUPS, SSD_STATE)
    dt = jax.nn.softplus(dt.astype(f32) + ssd_dt_bias)
    A = -jnp.exp(ssd_a_log.astype(f32))
    y, s_ssd_new = ssd_chunked(xs, dt, A, bm, cm, s_ssd)
    y = (y + ssd_d[:, None] * xs).reshape(bsz, t_len, SSD_INNER) * jax.nn.silu(z.astype(f32))
    yg = y.reshape(bsz, t_len, SSD_GROUPS, SSD_INNER // SSD_GROUPS)
    yg = yg * lax.rsqrt(jnp.mean(yg * yg, axis=-1, keepdims=True) + 1e-5)
    y_ssd = yg.reshape(bsz, t_len, SSD_INNER) * ssd_norm_w
    out = jnp.concatenate([lru_out, y_ssd], axis=-1).astype(h.dtype) @ w_out1
    return out, hseq[:, -1], lru_conv_new, s_ssd_new, ssd_conv_new


def swiglu(h, w_gate, w_up, w_down):
    return (jax.nn.silu(h @ w_gate) * (h @ w_up)) @ w_down


def trunk(x, states, ab_w, cd_w, ffn_w):
    s_gla, s_rwkv, s_shift, s_lru, s_lru_conv, s_ssd, s_ssd_conv = states
    g_mix, g_ffn, w_gate, w_up, w_down, g_final = ffn_w
    for layer in range(DEPTH):
        h = rmsnorm(x, g_mix[layer])
        if layer % 2 == 0:
            m, s_gla, s_rwkv, s_shift = mix_ab(h, s_gla, s_rwkv, s_shift, *ab_w)
        else:
            m, s_lru, s_lru_conv, s_ssd, s_ssd_conv = mix_cd(h, s_lru, s_lru_conv, s_ssd, s_ssd_conv, *cd_w)
        x = x + m.astype(x.dtype)
        x = x + swiglu(rmsnorm(x, g_ffn[layer]), w_gate[layer], w_up[layer], w_down[layer]).astype(x.dtype)
    return rmsnorm(x, g_final), (s_gla, s_rwkv, s_shift, s_lru, s_lru_conv, s_ssd, s_ssd_conv)


def setup_inputs(seed: int = 0) -> dict:
    key = jax.random.key(seed)
    ks = iter(jax.random.split(key, 64))
    f32 = jnp.float32
    nrm = lambda shape, scale: scale * jax.random.normal(next(ks), shape, f32)
    uni = lambda shape, lo, hi: jax.random.uniform(next(ks), shape, f32, lo, hi)
    dt0 = jnp.exp(uni((SSD_HEADS,), math.log(1e-3), math.log(1e-1)))
    return {
        'x_prompt': nrm((BATCH, SEQ, D_MODEL), 1.0),
        'x_sample': nrm((DEC_BATCH, DEC_SEQ, D_MODEL), 1.0),
        'state_gla': nrm((DEC_BATCH, GLA_HEADS, GLA_DK, GLA_DV), 0.5),
        'state_rwkv': nrm((DEC_BATCH, RWKV_HEADS, RWKV_HEAD, RWKV_HEAD), 0.3),
        'state_rwkv_shift': nrm((DEC_BATCH, RWKV_COLS), 1.0),
        'state_lru': nrm((DEC_BATCH, LRU_WIDTH), 0.5),
        'state_lru_conv': nrm((DEC_BATCH, CONV_W - 1, LRU_WIDTH), 1.0),
        'state_ssd': nrm((DEC_BATCH, SSD_HEADS, SSD_HEADDIM, SSD_STATE), 0.3),
        'state_ssd_conv': nrm((DEC_BATCH, CONV_W - 1, SSD_CONV_CH), 1.0),
        'w_in0': nrm((D_MODEL, IN0), D_MODEL ** -0.5),
        'gla_w_a2': nrm((GLA_GATE_RANK, GLA_HEADS * GLA_DK), GLA_GATE_RANK ** -0.5),
        'gla_b_a': 1.0 + nrm((GLA_HEADS * GLA_DK,), 0.5),
        'gla_g_norm': 1.0 + nrm((GLA_DV,), 0.1),
        'rwkv_mu': uni((RWKV_COLS,), 0.0, 1.0),
        'rwkv_w0': nrm((RWKV_WIDTH,), 0.5) - 0.5,
        'rwkv_w2': nrm((RWKV_DECAY_LORA, RWKV_WIDTH), 0.1 * RWKV_DECAY_LORA ** -0.5),
        'rwkv_a0': nrm((RWKV_WIDTH,), 0.5),
        'rwkv_a2': nrm((RWKV_A_LORA, RWKV_WIDTH), 0.5 * RWKV_A_LORA ** -0.5),
        'rwkv_g2': nrm((RWKV_G_LORA, RWKV_WIDTH), RWKV_G_LORA ** -0.5),
        'rwkv_k_k': 1.0 + nrm((RWKV_WIDTH,), 0.1),
        'rwkv_k_a': 1.0 + nrm((RWKV_WIDTH,), 0.1),
        'rwkv_r_k': nrm((RWKV_HEADS, RWKV_HEAD), 0.1),
        'rwkv_ln_w': 1.0 + nrm((RWKV_WIDTH,), 0.1),
        'rwkv_ln_b': nrm((RWKV_WIDTH,), 0.01),
        'w_out0': nrm((MIX0, D_MODEL), MIX0 ** -0.5),
        'w_in1': nrm((D_MODEL, IN1), D_MODEL ** -0.5),
        'lru_conv_w': nrm((CONV_W, LRU_WIDTH), CONV_W ** -0.5),
        'lru_conv_b': nrm((LRU_WIDTH,), 0.01),
        'lru_w_r': nrm((LRU_BLOCKS, LRU_BLOCK, LRU_BLOCK), LRU_BLOCK ** -0.5),
        'lru_b_r': nrm((LRU_BLOCKS, LRU_BLOCK), 0.01),
        'lru_w_i': nrm((LRU_BLOCKS, LRU_BLOCK, LRU_BLOCK), LRU_BLOCK ** -0.5),
        'lru_b_i': nrm((LRU_BLOCKS, LRU_BLOCK), 0.01),
        'lru_lambda': uni((LRU_WIDTH,), 4.3, 9.0),
        'ssd_conv_w': nrm((CONV_W, SSD_CONV_CH), CONV_W ** -0.5),
        'ssd_conv_b': nrm((SSD_CONV_CH,), 0.01),
        'ssd_dt_bias': dt0 + jnp.log(-jnp.expm1(-dt0)),
        'ssd_a_log': jnp.log(uni((SSD_HEADS,), 1.0, 16.0)),
        'ssd_d': 1.0 + nrm((SSD_HEADS,), 0.1),
        'ssd_norm_w': 1.0 + nrm((SSD_INNER,), 0.1),
        'w_out1': nrm((MIX1, D_MODEL), MIX1 ** -0.5),
        'g_mix': 1.0 + nrm((DEPTH, D_MODEL), 0.1),
        'g_ffn': 1.0 + nrm((DEPTH, D_MODEL), 0.1),
        'w_ffn_gate': nrm((DEPTH, D_MODEL, D_FF), D_MODEL ** -0.5),
        'w_ffn_up': nrm((DEPTH, D_MODEL, D_FF), D_MODEL ** -0.5),
        'w_ffn_down': nrm((DEPTH, D_FF, D_MODEL), D_FF ** -0.5),
        'g_final': 1.0 + nrm((D_MODEL,), 0.1),
    }


def reference(x_prompt, x_sample, state_gla, state_rwkv, state_rwkv_shift, state_lru, state_lru_conv, state_ssd,
              state_ssd_conv, w_in0, gla_w_a2, gla_b_a, gla_g_norm, rwkv_mu, rwkv_w0, rwkv_w2, rwkv_a0, rwkv_a2,
              rwkv_g2, rwkv_k_k, rwkv_k_a, rwkv_r_k, rwkv_ln_w, rwkv_ln_b, w_out0, w_in1, lru_conv_w, lru_conv_b,
              lru_w_r, lru_b_r, lru_w_i, lru_b_i, lru_lambda, ssd_conv_w, ssd_conv_b, ssd_dt_bias, ssd_a_log, ssd_d,
              ssd_norm_w, w_out1, g_mix, g_ffn, w_ffn_gate, w_ffn_up, w_ffn_down, g_final):
    f32 = jnp.float32
    ab_w = (w_in0, gla_w_a2, gla_b_a, gla_g_norm, rwkv_mu, rwkv_w0, rwkv_w2, rwkv_a0, rwkv_a2, rwkv_g2,
            rwkv_k_k, rwkv_k_a, rwkv_r_k, rwkv_ln_w, rwkv_ln_b, w_out0)
    cd_w = (w_in1, lru_conv_w, lru_conv_b, lru_w_r, lru_b_r, lru_w_i, lru_b_i, lru_lambda, ssd_conv_w,
            ssd_conv_b, ssd_dt_bias, ssd_a_log, ssd_d, ssd_norm_w, w_out1)
    ffn_w = (g_mix, g_ffn, w_ffn_gate, w_ffn_up, w_ffn_down, g_final)
    bp = x_prompt.shape[0]
    prompt_init = (
        jnp.zeros((bp, GLA_HEADS, GLA_DK, GLA_DV), f32),
        jnp.zeros((bp, RWKV_HEADS, RWKV_HEAD, RWKV_HEAD), f32),
        jnp.zeros((bp, RWKV_COLS), x_prompt.dtype),
        jnp.zeros((bp, LRU_WIDTH), f32),
        jnp.zeros((bp, CONV_W - 1, LRU_WIDTH), x_prompt.dtype),
        jnp.zeros((bp, SSD_HEADS, SSD_HEADDIM, SSD_STATE), f32),
        jnp.zeros((bp, CONV_W - 1, SSD_CONV_CH), x_prompt.dtype),
    )
    sample_init = (state_gla, state_rwkv, state_rwkv_shift, state_lru, state_lru_conv, state_ssd, state_ssd_conv)
    y_prompt, p_states = trunk(x_prompt, prompt_init, ab_w, cd_w, ffn_w)
    y_sample, s_states = trunk(x_sample, sample_init, ab_w, cd_w, ffn_w)
    p_gla, p_rwkv, p_shift, p_lru, p_lru_conv, p_ssd, p_ssd_conv = p_states
    s_gla, s_rwkv, s_shift, s_lru, s_lru_conv, s_ssd, s_ssd_conv = s_states
    return (y_prompt, y_sample, p_gla, p_rwkv, p_shift, p_lru, p_lru_conv, p_ssd, p_ssd_conv,
            s_gla, s_rwkv, s_shift, s_lru, s_lru_conv, s_ssd, s_ssd_conv)
```

```python
import functools

import jax
import jax.numpy as jnp
from jax import lax
from jax.experimental import pallas as pl
from jax.experimental.pallas import tpu as pltpu

F32 = jnp.float32
BF16 = jnp.bfloat16

D_MODEL = 1024
NORM_EPS = 1e-6
GLA_HEADS = 4
GLA_DK = 128
GLA_DV = 256
GLA_GATE_RANK = 16
GLA_GATE_NORM = 16.0
GLA_CHUNK = 64
RWKV_HEAD = 64
RWKV_HEADS = 16
RWKV_WIDTH = 1024
RWKV_LORA = 256
RWKV_COLS = 3 * RWKV_WIDTH + RWKV_LORA
RWKV_GN_EPS = 64e-5
LRU_WIDTH = 1024
LRU_BLOCKS = 8
LRU_BLOCK = 128
LRU_C = 8.0
CONV_W = 4
SSD_INNER = 1024
SSD_HEADDIM = 64
SSD_HEADS = 16
SSD_GROUPS = 2
SSD_STATE = 128
SSD_CHUNK = 64
SSD_CONV_CH = SSD_INNER + 2 * SSD_GROUPS * SSD_STATE
D_FF = 2816

LANES = 128
SUBLANES = 8
VMEM_LIMIT_BYTES = 56 * 1024 * 1024

P0_COLS = 6528
P0_TN = 2176
P1_COLS = 4864
P1_TN = 2432


def _cparams(semantics):
    return pltpu.CompilerParams(dimension_semantics=semantics, vmem_limit_bytes=VMEM_LIMIT_BYTES)


def _dot(a, b):
    return jnp.dot(a, b, preferred_element_type=F32)


def _dot_nt(a, b):
    return lax.dot_general(a, b, (((1,), (1,)), ((), ())), preferred_element_type=F32)


def _dot_tn(a, b):
    return lax.dot_general(a, b, (((0,), (0,)), ((), ())), preferred_element_type=F32)


def _softplus(x):
    return jnp.maximum(x, 0.0) + jnp.log1p(jnp.exp(-jnp.abs(x)))


def _silu(x):
    return x * jax.nn.sigmoid(x)


def _gelu_tanh(x):
    return 0.5 * x * (1.0 + jnp.tanh(0.7978845608028654 * (x + 0.044715 * (x * x * x))))


def _rms(x, g, eps):
    return x * lax.rsqrt(jnp.mean(x * x, axis=-1, keepdims=True) + eps) * g


def _row_iota(shape):
    return lax.broadcasted_iota(jnp.int32, shape, 0)


def _cumsum_rows(x, seg):
    rows = _row_iota(x.shape) & (seg - 1)
    d = 1
    while d < seg:
        x = x + jnp.where(rows >= d, pltpu.roll(x, d, axis=0), 0.0)
        d *= 2
    return x


def _shifted(ext, k, rows):
    return pltpu.roll(ext, k, axis=0)[SUBLANES:SUBLANES + rows]


def _in_proj_kernel(x_ref, g_ref, w_ref, o_ref, h_scr):
    @pl.when(pl.program_id(1) == 0)
    def _():
        h_scr[...] = _rms(x_ref[...], g_ref[...], NORM_EPS).astype(BF16)

    o_ref[...] = _dot(h_scr[...], w_ref[...])


def _in_proj(x2d, g, w, tn):
    n, d = x2d.shape
    cols = w.shape[1]
    tm = min(n, 1024)
    return pl.pallas_call(
        _in_proj_kernel,
        grid=(n // tm, cols // tn),
        in_specs=[
            pl.BlockSpec((tm, d), lambda i, j: (i, 0)),
            pl.BlockSpec((1, d), lambda i, j: (0, 0)),
            pl.BlockSpec((d, tn), lambda i, j: (0, j)),
        ],
        out_specs=pl.BlockSpec((tm, tn), lambda i, j: (i, j)),
        out_shape=jax.ShapeDtypeStruct((n, cols), F32),
        scratch_shapes=[pltpu.VMEM((tm, d), BF16)],
        compiler_params=_cparams(("parallel", "arbitrary")),
        name="in_proj",
    )(x2d, g.reshape(1, d), w)


def _out_ffn_kernel(*refs, has_gate, final_norm):
    refs = list(refs)
    x_ref, oa_ref, ob_ref = refs[:3]
    pos = 3
    gate_ref = None
    if has_gate:
        gate_ref = refs[pos]
        pos += 1
    woa_ref, wob_ref, gffn_ref, wg_ref, wu_ref, wd_ref = refs[pos:pos + 6]
    pos += 6
    gfin_ref = None
    if final_norm:
        gfin_ref = refs[pos]
        pos += 1
    out_ref, x1_scr, h_scr, acc_scr = refs[pos:pos + 4]

    k = pl.program_id(1)

    @pl.when(k == 0)
    def _():
        ob = ob_ref[...]
        if has_gate:
            ob = (ob * gate_ref[...]).astype(BF16)
        x1 = x_ref[...] + (_dot(oa_ref[...], woa_ref[...]) + _dot(ob, wob_ref[...]))
        x1_scr[...] = x1
        h_scr[...] = _rms(x1, gffn_ref[...], NORM_EPS).astype(BF16)
        acc_scr[...] = jnp.zeros_like(acc_scr)

    h = h_scr[...]
    act = (_silu(_dot(h, wg_ref[...])) * _dot(h, wu_ref[...])).astype(BF16)
    acc_scr[...] += _dot(act, wd_ref[...])

    @pl.when(k == pl.num_programs(1) - 1)
    def _():
        y = x1_scr[...] + acc_scr[...]
        if final_norm:
            y = _rms(y, gfin_ref[...], NORM_EPS)
        out_ref[...] = y


def _out_ffn(x2d, oa, ob, gate, wo, g_ffn, wg, wu, wd, g_final):
    n, d = x2d.shape
    tm = min(n, 512)
    tf = D_FF // 2
    has_gate = gate is not None
    final_norm = g_final is not None
    row = lambda i, k: (i, 0)
    const = lambda i, k: (0, 0)
    args = [x2d, oa, ob]
    in_specs = [pl.BlockSpec((tm, d), row)] * 3
    if has_gate:
        args.append(gate)
        in_specs.append(pl.BlockSpec((tm, d), row))
    half = wo.shape[0] // 2
    args += [wo[:half], wo[half:], g_ffn.reshape(1, d), wg, wu, wd]
    in_specs += [
        pl.BlockSpec((half, d), const),
        pl.BlockSpec((half, d), const),
        pl.BlockSpec((1, d), const),
        pl.BlockSpec((d, tf), lambda i, k: (0, k)),
        pl.BlockSpec((d, tf), lambda i, k: (0, k)),
        pl.BlockSpec((tf, d), lambda i, k: (k, 0)),
    ]
    if final_norm:
        args.append(g_final.reshape(1, d))
        in_specs.append(pl.BlockSpec((1, d), const))
    return pl.pallas_call(
        functools.partial(_out_ffn_kernel, has_gate=has_gate, final_norm=final_norm),
        grid=(n // tm, D_FF // tf),
        in_specs=in_specs,
        out_specs=pl.BlockSpec((tm, d), row),
        out_shape=jax.ShapeDtypeStruct((n, d), F32),
        scratch_shapes=[pltpu.VMEM((tm, d), F32), pltpu.VMEM((tm, d), BF16), pltpu.VMEM((tm, d), F32)],
        compiler_params=_cparams(("parallel", "arbitrary")),
        name="out_ffn",
    )(*args)


def _gla_kernel(q_ref, k_ref, v_ref, og_ref, al_ref, wa2_ref, ba_ref, gn_ref, s0_ref,
                o_ref, sout_ref, st_scr, *, c, mm):
    n = pl.program_id(1)

    @pl.when(n == 0)
    def _():
        for h in range(GLA_HEADS):
            st_scr[h] = s0_ref[0, h].T

    la_all = _dot(al_ref[...].astype(BF16), wa2_ref[...]) + ba_ref[...]
    la_all = -_softplus(-la_all) * (1.0 / GLA_GATE_NORM)
    b_all = _cumsum_rows(la_all, c)
    mask = _row_iota((c, c)) >= lax.broadcasted_iota(jnp.int32, (c, c), 1)
    gn = gn_ref[...]
    for h in range(GLA_HEADS):
        ks = slice(h * GLA_DK, (h + 1) * GLA_DK)
        vs = slice(h * GLA_DV, (h + 1) * GLA_DV)
        b = b_all[:, ks]
        b_last = b[c - 1:c, :]
        q = q_ref[:, ks] * (GLA_DK ** -0.5)
        k = k_ref[:, ks]
        v = v_ref[:, vs].astype(mm)
        qd = (q * jnp.exp(b)).astype(mm)
        kd = (k * jnp.exp(-b)).astype(mm)
        kc = (k * jnp.exp(b_last - b)).astype(mm)
        scores = jnp.where(mask, _dot_nt(qd, kd), 0.0)
        st = st_scr[h]
        o = _dot(scores.astype(mm), v) + _dot_nt(qd, st.astype(mm))
        st_scr[h] = st * jnp.exp(b_last) + _dot_tn(v, kc)
        o = _rms(o, gn, 1e-5)
        o_ref[:, vs] = (o * _silu(og_ref[:, vs])).astype(BF16)

    @pl.when(n == pl.num_programs(1) - 1)
    def _():
        for h in range(GLA_HEADS):
            sout_ref[0, h] = st_scr[h].T


def _gla(p0, s0, wa2, ba, gn, bsz, t_len):
    c = min(GLA_CHUNK, t_len)
    nt = t_len // c
    mm = BF16 if c % 16 == 0 else F32
    n = bsz * t_len
    rows = lambda w: (lambda b, j: (b * nt + j, w))
    const = lambda b, j: (0, 0)
    st_spec = pl.BlockSpec((1, GLA_HEADS, GLA_DK, GLA_DV), lambda b, j: (b, 0, 0, 0))
    return pl.pallas_call(
        functools.partial(_gla_kernel, c=c, mm=mm),
        grid=(bsz, nt),
        in_specs=[
            pl.BlockSpec((c, 512), rows(0)),
            pl.BlockSpec((c, 512), rows(1)),
            pl.BlockSpec((c, 1024), rows(1)),
            pl.BlockSpec((c, 1024), rows(2)),
            pl.BlockSpec((c, LANES), rows(50)),
            pl.BlockSpec((LANES, 512), const),
            pl.BlockSpec((1, 512), const),
            pl.BlockSpec((1, GLA_DV), const),
            st_spec,
        ],
        out_specs=[pl.BlockSpec((c, 1024), rows(0)), st_spec],
        out_shape=[jax.ShapeDtypeStruct((n, 1024), BF16),
                   jax.ShapeDtypeStruct((bsz, GLA_HEADS, GLA_DK, GLA_DV), F32)],
        scratch_shapes=[pltpu.VMEM((GLA_HEADS, GLA_DV, GLA_DK), F32)],
        compiler_params=_cparams(("parallel", "arbitrary")),
        name="gla",
    )(p0, p0, p0, p0, p0, wa2, ba, gn, s0)


def _rwkv_prep_kernel(r_ref, k_ref, v_ref, l_ref, ir_ref, ik_ref, iv_ref, il_ref,
                      mu_ref, w0_ref, a0_ref, wa_ref, g2_ref,
                      ro_ref, wo_ref, ko_ref, vo_ref, ao_ref, go_ref, carry,
                      *, tm, t_len, multi_seq):
    j = pl.program_id(1)
    cols = ((0, 1024), (1024, 2048), (2048, 3072), (3072, RWKV_COLS))

    if not multi_seq:
        @pl.when(j == 0)
        def _():
            for (lo, hi), iref in zip(cols, (ir_ref, ik_ref, iv_ref, il_ref)):
                carry[:, lo:hi] = iref[...]

    def mixed(x_ref, i_ref, lo, hi):
        x = x_ref[...]
        rolled = pltpu.roll(x, 1, axis=0)
        rows = _row_iota(x.shape)
        if multi_seq:
            prev = jnp.where((rows & (t_len - 1)) == 0, i_ref[...], rolled)
        else:
            prev = jnp.where(rows == 0, carry[SUBLANES - 1:SUBLANES, lo:hi], rolled)
            carry[:, lo:hi] = x[tm - SUBLANES:tm, :]
        return x + (prev - x) * mu_ref[:, lo:hi]

    r = mixed(r_ref, ir_ref, *cols[0])
    k = mixed(k_ref, ik_ref, *cols[1])
    v = mixed(v_ref, iv_ref, *cols[2])
    lo_rank = mixed(l_ref, il_ref, *cols[3])

    wa_in = lo_rank[:, :LANES]
    lane = lax.broadcasted_iota(jnp.int32, wa_in.shape, 1)
    wa_in = jnp.where(lane < 64, jnp.tanh(wa_in), wa_in).astype(BF16)
    wa = _dot(wa_in, wa_ref[...])
    g = _dot(jax.nn.sigmoid(lo_rank[:, LANES:]).astype(BF16), g2_ref[...])
    w = -_softplus(-(w0_ref[...] + wa[:, :RWKV_WIDTH])) - 0.5
    ro_ref[...] = r
    wo_ref[...] = jnp.exp(-jnp.exp(w))
    ko_ref[...] = k
    vo_ref[...] = v
    ao_ref[...] = jax.nn.sigmoid(a0_ref[...] + wa[:, RWKV_WIDTH:])
    go_ref[...] = g


def _rwkv_prep(p0, shift8, mu, w0, a0, wa, g2, bsz, t_len):
    n = bsz * t_len
    multi_seq = t_len < 256
    tm = min(n, 512) if multi_seq else 256
    nt = 1 if multi_seq else t_len // tm
    nb = n // tm if multi_seq else bsz
    ti = tm if multi_seq else SUBLANES
    rows = lambda w: (lambda i, j: (i * nt + j, w))
    init = lambda w: (lambda i, j: (i, w))
    const = lambda i, j: (0, 0)
    out = jax.ShapeDtypeStruct((n, RWKV_WIDTH), F32)
    return pl.pallas_call(
        functools.partial(_rwkv_prep_kernel, tm=tm, t_len=t_len, multi_seq=multi_seq),
        grid=(nb, nt),
        in_specs=[
            pl.BlockSpec((tm, 1024), rows(3)),
            pl.BlockSpec((tm, 1024), rows(4)),
            pl.BlockSpec((tm, 1024), rows(5)),
            pl.BlockSpec((tm, RWKV_LORA), rows(24)),
            pl.BlockSpec((ti, 1024), init(0)),
            pl.BlockSpec((ti, 1024), init(1)),
            pl.BlockSpec((ti, 1024), init(2)),
            pl.BlockSpec((ti, RWKV_LORA), init(12)),
            pl.BlockSpec((1, RWKV_COLS), const),
            pl.BlockSpec((1, RWKV_WIDTH), const),
            pl.BlockSpec((1, RWKV_WIDTH), const),
            pl.BlockSpec((LANES, 2 * RWKV_WIDTH), const),
            pl.BlockSpec((LANES, RWKV_WIDTH), const),
        ],
        out_specs=[pl.BlockSpec((tm, RWKV_WIDTH), rows(0))] * 6,
        out_shape=[out] * 6,
        scratch_shapes=[pltpu.VMEM((SUBLANES, RWKV_COLS), F32)],
        compiler_params=_cparams(("parallel", "arbitrary")),
        name="rwkv_prep",
    )(p0, p0, p0, p0, shift8, shift8, shift8, shift8, mu, w0, a0, wa, g2)


def _rwkv_scan_kernel(r_ref, w_ref, k_ref, v_ref, a_ref, kk_ref, ka_ref, rk_ref, lnw_ref, lnb_ref,
                      s0_ref, y_ref, s_ref, *, tt):
    @pl.when(pl.program_id(1) == 0)
    def _():
        s_ref[...] = s0_ref[...]

    k_k = kk_ref[...]
    k_a = ka_ref[...]
    r_k = rk_ref[...]
    ln_w = lnw_ref[...]
    ln_b = lnb_ref[...]

    def step(s, carry):
        k = k_ref[s]
        a = a_ref[s]
        r = r_ref[s]
        w = w_ref[s]
        v = v_ref[s]
        kk = k * k_k
        kk = kk / jnp.maximum(jnp.sqrt(jnp.sum(kk * kk, axis=0, keepdims=True)), 1e-12)
        kb = kk * a
        kp = k * (1.0 + (a - 1.0) * k_a)
        bonus = jnp.sum(r * kp * r_k, axis=0, keepdims=True) * v
        sa = jnp.zeros_like(v)
        for j in range(RWKV_HEAD):
            sa = sa + s_ref[j] * kk[j:j + 1, :]
        sa = -sa
        y = jnp.zeros_like(v)
        for j in range(RWKV_HEAD):
            sn = s_ref[j] * w[j:j + 1, :] + sa * kb[j:j + 1, :] + v * kp[j:j + 1, :]
            s_ref[j] = sn
            y = y + sn * r[j:j + 1, :]
        mu = jnp.mean(y, axis=0, keepdims=True)
        yc = y - mu
        var = jnp.mean(yc * yc, axis=0, keepdims=True)
        y_ref[s] = yc * lax.rsqrt(var + RWKV_GN_EPS) * ln_w + ln_b + bonus
        return carry

    lax.fori_loop(0, tt, step, 0)


def _rwkv_scan(r, w, k, v, a, tiles, s0, t_len):
    nc = r.shape[-1]
    tt = min(t_len, 32)
    seq = pl.BlockSpec((tt, RWKV_HEAD, LANES), lambda g, t: (t, 0, g))
    tile = pl.BlockSpec((RWKV_HEAD, LANES), lambda g, t: (0, 0))
    st = pl.BlockSpec((RWKV_HEAD, RWKV_HEAD, LANES), lambda g, t: (0, 0, g))
    return pl.pallas_call(
        functools.partial(_rwkv_scan_kernel, tt=tt),
        grid=(nc // LANES, t_len // tt),
        in_specs=[seq] * 5 + [tile] * 5 + [st],
        out_specs=[seq, st],
        out_shape=[jax.ShapeDtypeStruct((t_len, RWKV_HEAD, nc), F32),
                   jax.ShapeDtypeStruct((RWKV_HEAD, RWKV_HEAD, nc), F32)],
        compiler_params=_cparams(("parallel", "arbitrary")),
        name="rwkv_scan",
    )(r, w, k, v, a, *tiles, s0)


def _to_chains(x2d, bsz, t_len):
    x = x2d.reshape(bsz, t_len, RWKV_HEADS, RWKV_HEAD)
    return x.transpose(1, 3, 0, 2).reshape(t_len, RWKV_HEAD, bsz * RWKV_HEADS)


def _from_chains(y, bsz, t_len):
    y = y.reshape(t_len, RWKV_HEAD, bsz, RWKV_HEADS)
    return y.transpose(2, 0, 3, 1).reshape(bsz * t_len, RWKV_WIDTH)


def _chain_tile(p):
    t = p.reshape(RWKV_HEADS, RWKV_HEAD).T
    return jnp.tile(t, (1, LANES // RWKV_HEADS))


def _lru_kernel(gate_ref, x_ref, cinit_ref, cw_ref, cb_ref, wri_ref, br_ref, bi_ref, lam_ref, h0_ref,
                o_ref, hfin_ref, halo, hprev, *, tt):
    j = pl.program_id(1)

    @pl.when(j == 0)
    def _():
        halo[...] = cinit_ref[...]
        hprev[...] = jnp.broadcast_to(h0_ref[0], hprev.shape)

    u = x_ref[...]
    ext = jnp.concatenate([halo[...], u], axis=0)
    xc = cb_ref[...] + _shifted(ext, 3, tt) * cw_ref[0:1, :]
    xc = xc + _shifted(ext, 2, tt) * cw_ref[1:2, :]
    xc = xc + _shifted(ext, 1, tt) * cw_ref[2:3, :]
    xc = xc + u * cw_ref[3:4, :]
    halo[...] = u[tt - SUBLANES:tt, :]

    rows = _row_iota((tt, LRU_BLOCK))
    for n in range(LRU_BLOCKS):
        cs = slice(n * LRU_BLOCK, (n + 1) * LRU_BLOCK)
        xb = xc[:, cs]
        ri = _dot(xb.astype(BF16), wri_ref[n])
        rg = jax.nn.sigmoid(ri[:, :LRU_BLOCK] + br_ref[:, cs])
        ig = jax.nn.sigmoid(ri[:, LRU_BLOCK:] + bi_ref[:, cs])
        log_a = -LRU_C * rg * _softplus(-lam_ref[:, cs])
        a = jnp.exp(log_a)
        b = jnp.sqrt(-jnp.tanh(log_a) * (a * a + 1.0)) * ig * xb
        d = 1
        while d < tt:
            keep = rows >= d
            a_sh = jnp.where(keep, pltpu.roll(a, d, axis=0), 1.0)
            b_sh = jnp.where(keep, pltpu.roll(b, d, axis=0), 0.0)
            b = b + a * b_sh
            a = a * a_sh
            d *= 2
        h = a * hprev[0:1, cs] + b
        o_ref[:, cs] = (h * _gelu_tanh(gate_ref[:, cs])).astype(BF16)
        h_last = h[tt - 1:tt, :]
        hprev[:, cs] = jnp.broadcast_to(h_last, (SUBLANES, LRU_BLOCK))
        hfin_ref[0, :, cs] = h_last


def _lru(p1, cinit, cw, cb, wri, br, bi, lam, h0, bsz, t_len):
    tt = min(t_len, 64)
    nt = t_len // tt
    n = bsz * t_len
    rows = lambda w: (lambda b, j: (b * nt + j, w))
    const2 = lambda b, j: (0, 0)
    return pl.pallas_call(
        functools.partial(_lru_kernel, tt=tt),
        grid=(bsz, nt),
        in_specs=[
            pl.BlockSpec((tt, 1024), rows(0)),
            pl.BlockSpec((tt, 1024), rows(1)),
            pl.BlockSpec((SUBLANES, LRU_WIDTH), lambda b, j: (b, 0)),
            pl.BlockSpec((SUBLANES, LRU_WIDTH), const2),
            pl.BlockSpec((1, LRU_WIDTH), const2),
            pl.BlockSpec((LRU_BLOCKS, LRU_BLOCK, 2 * LRU_BLOCK), lambda b, j: (0, 0, 0)),
            pl.BlockSpec((1, LRU_WIDTH), const2),
            pl.BlockSpec((1, LRU_WIDTH), const2),
            pl.BlockSpec((1, LRU_WIDTH), const2),
            pl.BlockSpec((1, 1, LRU_WIDTH), lambda b, j: (b, 0, 0)),
        ],
        out_specs=[pl.BlockSpec((tt, LRU_WIDTH), rows(0)),
                   pl.BlockSpec((1, 1, LRU_WIDTH), lambda b, j: (b, 0, 0))],
        out_shape=[jax.ShapeDtypeStruct((n, LRU_WIDTH), BF16),
                   jax.ShapeDtypeStruct((bsz, 1, LRU_WIDTH), F32)],
        scratch_shapes=[pltpu.VMEM((SUBLANES, LRU_WIDTH), F32), pltpu.VMEM((SUBLANES, LRU_WIDTH), F32)],
        compiler_params=_cparams(("parallel", "arbitrary")),
        name="lru",
    )(p1, p1, cinit, cw, cb, wri, br, bi, lam, h0)


def _ssd_kernel(z_ref, xbc_ref, dt_ref, cinit_ref, cw_ref, cb_ref, dtb_ref, alog_ref, dd_ref, nw_ref, s0_ref,
                o_ref, s_ref, halo, y_scr, *, c, mm):
    j = pl.program_id(1)

    @pl.when(j == 0)
    def _():
        halo[...] = cinit_ref[...]
        s_ref[...] = s0_ref[...]

    u = xbc_ref[...]
    ext = jnp.concatenate([halo[...], u], axis=0)
    xc = cb_ref[...] + _shifted(ext, 3, c) * cw_ref[0:1, :]
    xc = xc + _shifted(ext, 2, c) * cw_ref[1:2, :]
    xc = xc + _shifted(ext, 1, c) * cw_ref[2:3, :]
    xc = _silu(xc + u * cw_ref[3:4, :])
    halo[...] = u[c - SUBLANES:c, :]

    xs = xc[:, :SSD_INNER]
    gs = SSD_GROUPS * SSD_STATE
    bm = xc[:, SSD_INNER:SSD_INNER + gs].astype(mm)
    cm = xc[:, SSD_INNER + gs:].astype(mm)
    dt = _softplus(dt_ref[...] + dtb_ref[...])
    cs = _cumsum_rows(dt * (-jnp.exp(alog_ref[...])), c)
    cs_t = jnp.concatenate([cs, jnp.zeros((LANES - c, LANES), F32)], axis=0).T
    mask = _row_iota((c, c)) >= lax.broadcasted_iota(jnp.int32, (c, c), 1)
    hg = SSD_HEADS // SSD_GROUPS
    cb = [_dot_nt(cm[:, g * SSD_STATE:(g + 1) * SSD_STATE], bm[:, g * SSD_STATE:(g + 1) * SSD_STATE])
          for g in range(SSD_GROUPS)]
    for h in range(SSD_HEADS):
        g = h // hg
        ps = slice(h * SSD_HEADDIM, (h + 1) * SSD_HEADDIM)
        ss = slice(g * SSD_STATE, (g + 1) * SSD_STATE)
        col = cs[:, h:h + 1]
        last = cs[c - 1:c, h:h + 1]
        lmat = jnp.exp(jnp.where(mask, col - cs_t[h:h + 1, :c], -jnp.inf))
        xh = xs[:, ps]
        xdt = xh * dt[:, h:h + 1]
        st = s_ref[0, h]
        y = _dot((cb[g] * lmat).astype(mm), xdt.astype(mm))
        y = y + jnp.exp(col) * _dot_nt(cm[:, ss], st.astype(mm))
        s_ref[0, h] = st * jnp.exp(last) + _dot_tn((xdt * jnp.exp(last - col)).astype(mm), bm[:, ss])
        y_scr[:, ps] = y + dd_ref[:, ps] * xh

    y = y_scr[...] * _silu(z_ref[...])
    gw = SSD_INNER // SSD_GROUPS
    for g in range(SSD_GROUPS):
        gsl = slice(g * gw, (g + 1) * gw)
        o_ref[:, gsl] = _rms(y[:, gsl], nw_ref[:, gsl], 1e-5).astype(BF16)


def _ssd(p1, cinit, cw, cb, dtb, alog, dd, nw, s0, bsz, t_len):
    c = min(SSD_CHUNK, t_len)
    nt = t_len // c
    mm = BF16 if c % 16 == 0 else F32
    n = bsz * t_len
    rows = lambda w: (lambda b, j: (b * nt + j, w))
    const2 = lambda b, j: (0, 0)
    st_spec = pl.BlockSpec((1, SSD_HEADS, SSD_HEADDIM, SSD_STATE), lambda b, j: (b, 0, 0, 0))
    return pl.pallas_call(
        functools.partial(_ssd_kernel, c=c, mm=mm),
        grid=(bsz, nt),
        in_specs=[
            pl.BlockSpec((c, 1024), rows(2)),
            pl.BlockSpec((c, SSD_CONV_CH), rows(2)),
            pl.BlockSpec((c, LANES), rows(36)),
            pl.BlockSpec((SUBLANES, SSD_CONV_CH), lambda b, j: (b, 0)),
            pl.BlockSpec((SUBLANES, SSD_CONV_CH), const2),
            pl.BlockSpec((1, SSD_CONV_CH), const2),
            pl.BlockSpec((1, LANES), const2),
            pl.BlockSpec((1, LANES), const2),
            pl.BlockSpec((1, SSD_INNER), const2),
            pl.BlockSpec((1, SSD_INNER), const2),
            st_spec,
        ],
        out_specs=[pl.BlockSpec((c, SSD_INNER), rows(0)), st_spec],
        out_shape=[jax.ShapeDtypeStruct((n, SSD_INNER), BF16),
                   jax.ShapeDtypeStruct((bsz, SSD_HEADS, SSD_HEADDIM, SSD_STATE), F32)],
        scratch_shapes=[pltpu.VMEM((SUBLANES, SSD_CONV_CH), F32), pltpu.VMEM((c, SSD_INNER), F32)],
        compiler_params=_cparams(("parallel", "arbitrary")),
        name="ssd",
    )(p1, p1, p1, cinit, cw, cb, dtb, alog, dd, nw, s0)


def _pad_rows_to_8(x):
    bsz, r, ch = x.shape
    return jnp.pad(x, ((0, 0), (SUBLANES - r, 0), (0, 0))).reshape(bsz * SUBLANES, ch)


def _pad_lanes(x, width):
    return jnp.pad(x, ((0, 0), (0, width - x.shape[1])))


def _trunk(x, states, w):
    s_gla, s_rwkv, s_shift, s_lru, s_lru_conv, s_ssd, s_ssd_conv = states
    bsz, t_len, d = x.shape
    n = bsz * t_len
    x2 = x.reshape(n, d)

    p0 = _in_proj(x2, w["g_mix0"], w["w_in0"], P0_TN)
    o_gla, s_gla_new = _gla(p0, s_gla, w["gla_wa2"], w["gla_ba"], w["gla_gn"], bsz, t_len)
    shift8 = jnp.broadcast_to(s_shift[:, None, :], (bsz, SUBLANES, RWKV_COLS)).reshape(bsz * SUBLANES, RWKV_COLS)
    r, wdec, k, v, a, g = _rwkv_prep(p0, shift8, w["rwkv_mu"], w["rwkv_w0"], w["rwkv_a0"], w["rwkv_wa"],
                                     w["rwkv_g2"], bsz, t_len)
    chains = [_to_chains(t, bsz, t_len) for t in (r, wdec, k, v, a)]
    s0c = s_rwkv.transpose(3, 2, 0, 1).reshape(RWKV_HEAD, RWKV_HEAD, bsz * RWKV_HEADS)
    y_c, s_c = _rwkv_scan(*chains, w["rwkv_tiles"], s0c, t_len)
    y_rwkv = _from_chains(y_c, bsz, t_len)
    s_rwkv_new = s_c.reshape(RWKV_HEAD, RWKV_HEAD, bsz, RWKV_HEADS).transpose(2, 3, 1, 0)
    shift_new = p0[:, 3072:3072 + RWKV_COLS].reshape(bsz, t_len, RWKV_COLS)[:, -1]
    x2 = _out_ffn(x2, o_gla, y_rwkv, g, w["w_out0"], w["g_ffn0"], w["wg0"], w["wu0"], w["wd0"], None)

    p1 = _in_proj(x2, w["g_mix1"], w["w_in1"], P1_TN)
    lru_out, h_lru = _lru(p1, _pad_rows_to_8(s_lru_conv), w["lru_cw"], w["lru_cb"], w["lru_wri"], w["lru_br"],
                          w["lru_bi"], w["lru_lam"], s_lru.reshape(bsz, 1, LRU_WIDTH), bsz, t_len)
    y_ssd, s_ssd_new = _ssd(p1, _pad_rows_to_8(s_ssd_conv), w["ssd_cw"], w["ssd_cb"], w["ssd_dtb"], w["ssd_alog"],
                            w["ssd_dd"], w["ssd_nw"], s_ssd, bsz, t_len)
    keep = CONV_W - 1
    lru_conv_new = p1[:, 1024:2048].reshape(bsz, t_len, LRU_WIDTH)[:, -keep:]
    ssd_conv_new = p1[:, 3072:3072 + SSD_CONV_CH].reshape(bsz, t_len, SSD_CONV_CH)[:, -keep:]
    y = _out_ffn(x2, lru_out, y_ssd, None, w["w_out1"], w["g_ffn1"], w["wg1"], w["wu1"], w["wd1"], w["g_final"])
    return y.reshape(bsz, t_len, d), (s_gla_new, s_rwkv_new, shift_new, h_lru.reshape(bsz, LRU_WIDTH),
                                      lru_conv_new, s_ssd_new, ssd_conv_new)


def kernel(x_prompt, x_sample, state_gla, state_rwkv, state_rwkv_shift, state_lru, state_lru_conv, state_ssd, state_ssd_conv, w_in0, gla_w_a2, gla_b_a, gla_g_norm, rwkv_mu, rwkv_w0, rwkv_w2, rwkv_a0, rwkv_a2, rwkv_g2, rwkv_k_k, rwkv_k_a, rwkv_r_k, rwkv_ln_w, rwkv_ln_b, w_out0, w_in1, lru_conv_w, lru_conv_b, lru_w_r, lru_b_r, lru_w_i, lru_b_i, lru_lambda, ssd_conv_w, ssd_conv_b, ssd_dt_bias, ssd_a_log, ssd_d, ssd_norm_w, w_out1, g_mix, g_ffn, w_ffn_gate, w_ffn_up, w_ffn_down, g_final):
    row = lambda p: p.reshape(1, -1).astype(F32)
    w0 = jnp.concatenate([w_in0[:, :2048], w_in0[:, 2064:3088], w_in0[:, 3088:],
                          _pad_lanes(w_in0[:, 2048:2064], LANES)], axis=1)
    zeros64 = jnp.zeros((64, RWKV_WIDTH), F32)
    rwkv_wa = jnp.concatenate([jnp.concatenate([rwkv_w2, zeros64], axis=1),
                               jnp.concatenate([zeros64, rwkv_a2], axis=1)], axis=0)
    w = {
        "w_in0": w0.astype(BF16),
        "w_in1": _pad_lanes(w_in1, P1_COLS).astype(BF16),
        "g_mix0": g_mix[0], "g_mix1": g_mix[1], "g_ffn0": g_ffn[0], "g_ffn1": g_ffn[1], "g_final": g_final,
        "gla_wa2": jnp.pad(gla_w_a2, ((0, LANES - GLA_GATE_RANK), (0, 0))).astype(BF16),
        "gla_ba": row(gla_b_a), "gla_gn": row(gla_g_norm),
        "rwkv_mu": row(rwkv_mu), "rwkv_w0": row(rwkv_w0), "rwkv_a0": row(rwkv_a0),
        "rwkv_wa": rwkv_wa.astype(BF16), "rwkv_g2": rwkv_g2.astype(BF16),
        "rwkv_tiles": [_chain_tile(p) for p in (rwkv_k_k, rwkv_k_a, rwkv_r_k, rwkv_ln_w, rwkv_ln_b)],
        "w_out0": w_out0.astype(BF16), "w_out1": w_out1.astype(BF16),
        "wg0": w_ffn_gate[0].astype(BF16), "wu0": w_ffn_up[0].astype(BF16), "wd0": w_ffn_down[0].astype(BF16),
        "wg1": w_ffn_gate[1].astype(BF16), "wu1": w_ffn_up[1].astype(BF16), "wd1": w_ffn_down[1].astype(BF16),
        "lru_cw": jnp.pad(lru_conv_w, ((0, SUBLANES - CONV_W), (0, 0))), "lru_cb": row(lru_conv_b),
        "lru_wri": jnp.concatenate([lru_w_r, lru_w_i], axis=2).astype(BF16),
        "lru_br": row(lru_b_r), "lru_bi": row(lru_b_i), "lru_lam": row(lru_lambda),
        "ssd_cw": jnp.pad(ssd_conv_w, ((0, SUBLANES - CONV_W), (0, 0))), "ssd_cb": row(ssd_conv_b),
        "ssd_dtb": _pad_lanes(row(ssd_dt_bias), LANES), "ssd_alog": _pad_lanes(row(ssd_a_log), LANES),
        "ssd_dd": row(jnp.repeat(ssd_d, SSD_HEADDIM)), "ssd_nw": row(ssd_norm_w),
    }
    bp = x_prompt.shape[0]
    prompt_init = (
        jnp.zeros((bp, GLA_HEADS, GLA_DK, GLA_DV), F32),
        jnp.zeros((bp, RWKV_HEADS, RWKV_HEAD, RWKV_HEAD), F32),
        jnp.zeros((bp, RWKV_COLS), F32),
        jnp.zeros((bp, LRU_WIDTH), F32),
        jnp.zeros((bp, CONV_W - 1, LRU_WIDTH), F32),
        jnp.zeros((bp, SSD_HEADS, SSD_HEADDIM, SSD_STATE), F32),
        jnp.zeros((bp, CONV_W - 1, SSD_CONV_CH), F32),
    )
    sample_init = (state_gla, state_rwkv, state_rwkv_shift, state_lru, state_lru_conv, state_ssd, state_ssd_conv)
    y_prompt, p_states = _trunk(x_prompt, prompt_init, w)
    y_sample, s_states = _trunk(x_sample, sample_init, w)
    return (y_prompt, y_sample, *p_states, *s_states)
```

```python
import functools

import jax
import jax.numpy as jnp
from jax import lax
from jax.experimental import pallas as pl
from jax.experimental.pallas import tpu as pltpu

F32 = jnp.float32
BF16 = jnp.bfloat16

D_MODEL = 1024
NORM_EPS = 1e-6
GLA_HEADS = 4
GLA_DK = 128
GLA_DV = 256
GLA_GATE_RANK = 16
GLA_GATE_NORM = 16.0
GLA_CHUNK = 64
RWKV_HEAD = 64
RWKV_HEADS = 16
RWKV_WIDTH = 1024
RWKV_LORA = 256
RWKV_COLS = 3 * RWKV_WIDTH + RWKV_LORA
RWKV_GN_EPS = 64e-5
LRU_WIDTH = 1024
LRU_BLOCKS = 8
LRU_BLOCK = 128
LRU_C = 8.0
CONV_W = 4
SSD_INNER = 1024
SSD_HEADDIM = 64
SSD_HEADS = 16
SSD_GROUPS = 2
SSD_STATE = 128
SSD_CHUNK = 64
SSD_CONV_CH = SSD_INNER + 2 * SSD_GROUPS * SSD_STATE
D_FF = 2816

LANES = 128
SUBLANES = 8
VMEM_LIMIT_BYTES = 56 * 1024 * 1024

P0_COLS = 6528
P0_TN = 2176
P1_COLS = 4864
P1_TN = 2432


def _cparams(semantics):
    return pltpu.CompilerParams(dimension_semantics=semantics, vmem_limit_bytes=VMEM_LIMIT_BYTES)


def _dot(a, b):
    return jnp.dot(a, b, preferred_element_type=F32)


def _dot_nt(a, b):
    return lax.dot_general(a, b, (((1,), (1,)), ((), ())), preferred_element_type=F32)


def _dot_tn(a, b):
    return lax.dot_general(a, b, (((0,), (0,)), ((), ())), preferred_element_type=F32)


def _softplus(x):
    return jnp.maximum(x, 0.0) + jnp.log1p(jnp.exp(-jnp.abs(x)))


def _silu(x):
    return x * jax.nn.sigmoid(x)


def _gelu_tanh(x):
    return 0.5 * x * (1.0 + jnp.tanh(0.7978845608028654 * (x + 0.044715 * (x * x * x))))


def _rms(x, g, eps):
    return x * lax.rsqrt(jnp.mean(x * x, axis=-1, keepdims=True) + eps) * g


def _row_iota(shape):
    return lax.broadcasted_iota(jnp.int32, shape, 0)


def _cumsum_rows(x, seg):
    rows = _row_iota(x.shape) & (seg - 1)
    d = 1
    while d < seg:
        x = x + jnp.where(rows >= d, pltpu.roll(x, d, axis=0), 0.0)
        d *= 2
    return x


def _shifted(ext, k, rows):
    return pltpu.roll(ext, k, axis=0)[SUBLANES:SUBLANES + rows]


def _in_proj_kernel(x_ref, g_ref, w_ref, o_ref, h_scr):
    @pl.when(pl.program_id(1) == 0)
    def _():
        h_scr[...] = _rms(x_ref[...], g_ref[...], NORM_EPS).astype(BF16)

    o_ref[...] = _dot(h_scr[...], w_ref[...])


def _in_proj(x2d, g, w, tn):
    n, d = x2d.shape
    cols = w.shape[1]
    tm = min(n, 1024)
    return pl.pallas_call(
        _in_proj_kernel,
        grid=(n // tm, cols // tn),
        in_specs=[
            pl.BlockSpec((tm, d), lambda i, j: (i, 0)),
            pl.BlockSpec((1, d), lambda i, j: (0, 0)),
            pl.BlockSpec((d, tn), lambda i, j: (0, j)),
        ],
        out_specs=pl.BlockSpec((tm, tn), lambda i, j: (i, j)),
        out_shape=jax.ShapeDtypeStruct((n, cols), F32),
        scratch_shapes=[pltpu.VMEM((tm, d), BF16)],
        compiler_params=_cparams(("parallel", "arbitrary")),
        name="in_proj",
    )(x2d, g.reshape(1, d), w)


def _out_ffn_kernel(*refs, has_gate, final_norm):
    refs = list(refs)
    x_ref, oa_ref, ob_ref = refs[:3]
    pos = 3
    gate_ref = None
    if has_gate:
        gate_ref = refs[pos]
        pos += 1
    woa_ref, wob_ref, gffn_ref, wg_ref, wu_ref, wd_ref = refs[pos:pos + 6]
    pos += 6
    gfin_ref = None
    if final_norm:
        gfin_ref = refs[pos]
        pos += 1
    out_ref, x1_scr, h_scr, acc_scr = refs[pos:pos + 4]

    k = pl.program_id(1)

    @pl.when(k == 0)
    def _():
        ob = ob_ref[...]
        if has_gate:
            ob = (ob * gate_ref[...]).astype(BF16)
        x1 = x_ref[...] + (_dot(oa_ref[...], woa_ref[...]) + _dot(ob, wob_ref[...]))
        x1_scr[...] = x1
        h_scr[...] = _rms(x1, gffn_ref[...], NORM_EPS).astype(BF16)
        acc_scr[...] = jnp.zeros_like(acc_scr)

    h = h_scr[...]
    act = (_silu(_dot(h, wg_ref[...])) * _dot(h, wu_ref[...])).astype(BF16)
    acc_scr[...] += _dot(act, wd_ref[...])

    @pl.when(k == pl.num_programs(1) - 1)
    def _():
        y = x1_scr[...] + acc_scr[...]
        if final_norm:
            y = _rms(y, gfin_ref[...], NORM_EPS)
        out_ref[...] = y


def _out_ffn(x2d, oa, ob, gate, wo, g_ffn, wg, wu, wd, g_final):
    n, d = x2d.shape
    tm = min(n, 512)
    tf = D_FF // 2
    has_gate = gate is not None
    final_norm = g_final is not None
    row = lambda i, k: (i, 0)
    const = lambda i, k: (0, 0)
    args = [x2d, oa, ob]
    in_specs = [pl.BlockSpec((tm, d), row)] * 3
    if has_gate:
        args.append(gate)
        in_specs.append(pl.BlockSpec((tm, d), row))
    half = wo.shape[0] // 2
    args += [wo[:half], wo[half:], g_ffn.reshape(1, d), wg, wu, wd]
    in_specs += [
        pl.BlockSpec((half, d), const),
        pl.BlockSpec((half, d), const),
        pl.BlockSpec((1, d), const),
        pl.BlockSpec((d, tf), lambda i, k: (0, k)),
        pl.BlockSpec((d, tf), lambda i, k: (0, k)),
        pl.BlockSpec((tf, d), lambda i, k: (k, 0)),
    ]
    if final_norm:
        args.append(g_final.reshape(1, d))
        in_specs.append(pl.BlockSpec((1, d), const))
    return pl.pallas_call(
        functools.partial(_out_ffn_kernel, has_gate=has_gate, final_norm=final_norm),
        grid=(n // tm, D_FF // tf),
        in_specs=in_specs,
        out_specs=pl.BlockSpec((tm, d), row),
        out_shape=jax.ShapeDtypeStruct((n, d), F32),
        scratch_shapes=[pltpu.VMEM((tm, d), F32), pltpu.VMEM((tm, d), BF16), pltpu.VMEM((tm, d), F32)],
        compiler_params=_cparams(("parallel", "arbitrary")),
        name="out_ffn",
    )(*args)


def _gla_kernel(q_ref, k_ref, v_ref, og_ref, al_ref, wa2_ref, ba_ref, gn_ref, s0_ref,
                o_ref, sout_ref, st_scr, *, c, mm, bb):
    n = pl.program_id(1)

    @pl.when(n == 0)
    def _():
        for i in range(bb):
            for h in range(GLA_HEADS):
                st_scr[i * GLA_HEADS + h] = s0_ref[i, h].T

    mask = _row_iota((c, c)) >= lax.broadcasted_iota(jnp.int32, (c, c), 1)
    gn = gn_ref[...]
    for i in range(bb):
        la_all = _dot(al_ref[i].astype(BF16), wa2_ref[...]) + ba_ref[...]
        la_all = -_softplus(-la_all) * (1.0 / GLA_GATE_NORM)
        b_all = _cumsum_rows(la_all, c)
        for h in range(GLA_HEADS):
            ks = slice(h * GLA_DK, (h + 1) * GLA_DK)
            vs = slice(h * GLA_DV, (h + 1) * GLA_DV)
            b = b_all[:, ks]
            b_last = b[c - 1:c, :]
            q = q_ref[i, :, ks] * (GLA_DK ** -0.5)
            k = k_ref[i, :, ks]
            v = v_ref[i, :, vs].astype(mm)
            qd = (q * jnp.exp(b)).astype(mm)
            kd = (k * jnp.exp(-b)).astype(mm)
            kc = (k * jnp.exp(b_last - b)).astype(mm)
            scores = jnp.where(mask, _dot_nt(qd, kd), 0.0)
            st = st_scr[i * GLA_HEADS + h]
            o = _dot(scores.astype(mm), v) + _dot_nt(qd, st.astype(mm))
            st_scr[i * GLA_HEADS + h] = st * jnp.exp(b_last) + _dot_tn(v, kc)
            o = _rms(o, gn, 1e-5)
            o_ref[i, :, vs] = (o * _silu(og_ref[i, :, vs])).astype(BF16)

    @pl.when(n == pl.num_programs(1) - 1)
    def _():
        for i in range(bb):
            for h in range(GLA_HEADS):
                sout_ref[i, h] = st_scr[i * GLA_HEADS + h].T


def _batch_block(bsz, t_len):
    return 2 if t_len >= 64 else min(bsz, 8)


def _gla(p0, s0, wa2, ba, gn, bsz, t_len):
    c = min(GLA_CHUNK, t_len)
    nt = t_len // c
    bb = _batch_block(bsz, t_len)
    mm = BF16 if c % 16 == 0 else F32
    p3 = p0.reshape(bsz, t_len, P0_COLS)
    rows = lambda w: (lambda b, j: (b, j, w))
    const = lambda b, j: (0, 0)
    st_spec = pl.BlockSpec((bb, GLA_HEADS, GLA_DK, GLA_DV), lambda b, j: (b, 0, 0, 0))
    o, s_new = pl.pallas_call(
        functools.partial(_gla_kernel, c=c, mm=mm, bb=bb),
        grid=(bsz // bb, nt),
        in_specs=[
            pl.BlockSpec((bb, c, 512), rows(0)),
            pl.BlockSpec((bb, c, 512), rows(1)),
            pl.BlockSpec((bb, c, 1024), rows(1)),
            pl.BlockSpec((bb, c, 1024), rows(2)),
            pl.BlockSpec((bb, c, LANES), rows(50)),
            pl.BlockSpec((LANES, 512), const),
            pl.BlockSpec((1, 512), const),
            pl.BlockSpec((1, GLA_DV), const),
            st_spec,
        ],
        out_specs=[pl.BlockSpec((bb, c, 1024), rows(0)), st_spec],
        out_shape=[jax.ShapeDtypeStruct((bsz, t_len, 1024), BF16),
                   jax.ShapeDtypeStruct((bsz, GLA_HEADS, GLA_DK, GLA_DV), F32)],
        scratch_shapes=[pltpu.VMEM((bb * GLA_HEADS, GLA_DV, GLA_DK), F32)],
        compiler_params=_cparams(("parallel", "arbitrary")),
        name="gla",
    )(p3, p3, p3, p3, p3, wa2, ba, gn, s0)
    return o.reshape(bsz * t_len, 1024), s_new


def _rwkv_prep_kernel(r_ref, k_ref, v_ref, l_ref, ir_ref, ik_ref, iv_ref, il_ref,
                      mu_ref, w0_ref, a0_ref, wa_ref, g2_ref,
                      ro_ref, wo_ref, ko_ref, vo_ref, ao_ref, go_ref, carry,
                      *, tm, t_len, multi_seq):
    j = pl.program_id(1)
    cols = ((0, 1024), (1024, 2048), (2048, 3072), (3072, RWKV_COLS))

    if not multi_seq:
        @pl.when(j == 0)
        def _():
            for (lo, hi), iref in zip(cols, (ir_ref, ik_ref, iv_ref, il_ref)):
                carry[:, lo:hi] = iref[...]

    def mixed(x_ref, i_ref, lo, hi):
        x = x_ref[...]
        rolled = pltpu.roll(x, 1, axis=0)
        rows = _row_iota(x.shape)
        if multi_seq:
            prev = jnp.where((rows & (t_len - 1)) == 0, i_ref[...], rolled)
        else:
            prev = jnp.where(rows == 0, carry[SUBLANES - 1:SUBLANES, lo:hi], rolled)
            carry[:, lo:hi] = x[tm - SUBLANES:tm, :]
        return x + (prev - x) * mu_ref[:, lo:hi]

    r = mixed(r_ref, ir_ref, *cols[0])
    k = mixed(k_ref, ik_ref, *cols[1])
    v = mixed(v_ref, iv_ref, *cols[2])
    lo_rank = mixed(l_ref, il_ref, *cols[3])

    wa_in = lo_rank[:, :LANES]
    lane = lax.broadcasted_iota(jnp.int32, wa_in.shape, 1)
    wa_in = jnp.where(lane < 64, jnp.tanh(wa_in), wa_in).astype(BF16)
    wa = _dot(wa_in, wa_ref[...])
    g = _dot(jax.nn.sigmoid(lo_rank[:, LANES:]).astype(BF16), g2_ref[...])
    w = -_softplus(-(w0_ref[...] + wa[:, :RWKV_WIDTH])) - 0.5
    ro_ref[...] = r
    wo_ref[...] = jnp.exp(-jnp.exp(w))
    ko_ref[...] = k
    vo_ref[...] = v
    ao_ref[...] = jax.nn.sigmoid(a0_ref[...] + wa[:, RWKV_WIDTH:])
    go_ref[...] = g


def _rwkv_prep(p0, shift8, mu, w0, a0, wa, g2, bsz, t_len):
    n = bsz * t_len
    multi_seq = t_len < 256
    tm = min(n, 512) if multi_seq else 256
    nt = 1 if multi_seq else t_len // tm
    nb = n // tm if multi_seq else bsz
    ti = tm if multi_seq else SUBLANES
    rows = lambda w: (lambda i, j: (i * nt + j, w))
    init = lambda w: (lambda i, j: (i, w))
    const = lambda i, j: (0, 0)
    out = jax.ShapeDtypeStruct((n, RWKV_WIDTH), F32)
    return pl.pallas_call(
        functools.partial(_rwkv_prep_kernel, tm=tm, t_len=t_len, multi_seq=multi_seq),
        grid=(nb, nt),
        in_specs=[
            pl.BlockSpec((tm, 1024), rows(3)),
            pl.BlockSpec((tm, 1024), rows(4)),
            pl.BlockSpec((tm, 1024), rows(5)),
            pl.BlockSpec((tm, RWKV_LORA), rows(24)),
            pl.BlockSpec((ti, 1024), init(0)),
            pl.BlockSpec((ti, 1024), init(1)),
            pl.BlockSpec((ti, 1024), init(2)),
            pl.BlockSpec((ti, RWKV_LORA), init(12)),
            pl.BlockSpec((1, RWKV_COLS), const),
            pl.BlockSpec((1, RWKV_WIDTH), const),
            pl.BlockSpec((1, RWKV_WIDTH), const),
            pl.BlockSpec((LANES, 2 * RWKV_WIDTH), const),
            pl.BlockSpec((LANES, RWKV_WIDTH), const),
        ],
        out_specs=[pl.BlockSpec((tm, RWKV_WIDTH), rows(0))] * 6,
        out_shape=[out] * 6,
        scratch_shapes=[pltpu.VMEM((SUBLANES, RWKV_COLS), F32)],
        compiler_params=_cparams(("parallel", "arbitrary")),
        name="rwkv_prep",
    )(p0, p0, p0, p0, shift8, shift8, shift8, shift8, mu, w0, a0, wa, g2)


def _rwkv_scan_kernel(r_ref, w_ref, k_ref, v_ref, a_ref, kk_ref, ka_ref, rk_ref, lnw_ref, lnb_ref,
                      s0_ref, y_ref, s_ref, kk_scr, kb_scr, kp_scr, *, tt):
    @pl.when(pl.program_id(1) == 0)
    def _():
        s_ref[...] = s0_ref[...]

    def prep(s, carry):
        k = k_ref[s]
        a = a_ref[s]
        kk = k * kk_ref[...]
        kk = kk / jnp.maximum(jnp.sqrt(jnp.sum(kk * kk, axis=0, keepdims=True)), 1e-12)
        kk_scr[s] = kk
        kb_scr[s] = kk * a
        kp_scr[s] = k * (1.0 + (a - 1.0) * ka_ref[...])
        return carry

    lax.fori_loop(0, tt, prep, 0)

    def row(ref, s, j):
        return jnp.broadcast_to(ref[s, pl.ds(j, 1), :], (RWKV_HEAD, LANES))

    s_kk = jnp.zeros((RWKV_HEAD, LANES), F32)
    for j in range(RWKV_HEAD):
        s_kk = s_kk + s_ref[j] * row(kk_scr, 0, j)

    def step(s, s_kk):
        sa = -s_kk
        v = v_ref[s]
        nxt = jnp.minimum(s + 1, tt - 1)
        y = jnp.zeros_like(v)
        s_kk_next = jnp.zeros_like(v)
        for j in range(RWKV_HEAD):
            sn = s_ref[j] * row(w_ref, s, j) + sa * row(kb_scr, s, j) + v * row(kp_scr, s, j)
            s_ref[j] = sn
            y = y + sn * row(r_ref, s, j)
            s_kk_next = s_kk_next + sn * row(kk_scr, nxt, j)
        bonus = jnp.sum(r_ref[s] * kp_scr[s] * rk_ref[...], axis=0, keepdims=True) * v
        mu = jnp.mean(y, axis=0, keepdims=True)
        yc = y - mu
        var = jnp.mean(yc * yc, axis=0, keepdims=True)
        y_ref[s] = yc * lax.rsqrt(var + RWKV_GN_EPS) * lnw_ref[...] + lnb_ref[...] + bonus
        return s_kk_next

    lax.fori_loop(0, tt, step, s_kk)


def _rwkv_scan(r, w, k, v, a, tiles, s0, t_len):
    nc = r.shape[-1]
    tt = min(t_len, 32)
    seq = pl.BlockSpec((tt, RWKV_HEAD, LANES), lambda g, t: (t, 0, g))
    tile = pl.BlockSpec((RWKV_HEAD, LANES), lambda g, t: (0, 0))
    st = pl.BlockSpec((RWKV_HEAD, RWKV_HEAD, LANES), lambda g, t: (0, 0, g))
    return pl.pallas_call(
        functools.partial(_rwkv_scan_kernel, tt=tt),
        grid=(nc // LANES, t_len // tt),
        in_specs=[seq] * 5 + [tile] * 5 + [st],
        out_specs=[seq, st],
        out_shape=[jax.ShapeDtypeStruct((t_len, RWKV_HEAD, nc), F32),
                   jax.ShapeDtypeStruct((RWKV_HEAD, RWKV_HEAD, nc), F32)],
        scratch_shapes=[pltpu.VMEM((tt, RWKV_HEAD, LANES), F32)] * 3,
        compiler_params=_cparams(("parallel", "arbitrary")),
        name="rwkv_scan",
    )(r, w, k, v, a, *tiles, s0)


def _to_chains(x2d, bsz, t_len):
    x = x2d.reshape(bsz, t_len, RWKV_HEADS, RWKV_HEAD)
    return x.transpose(1, 3, 0, 2).reshape(t_len, RWKV_HEAD, bsz * RWKV_HEADS)


def _from_chains(y, bsz, t_len):
    y = y.reshape(t_len, RWKV_HEAD, bsz, RWKV_HEADS)
    return y.transpose(2, 0, 3, 1).reshape(bsz * t_len, RWKV_WIDTH)


def _chain_tile(p):
    t = p.reshape(RWKV_HEADS, RWKV_HEAD).T
    return jnp.tile(t, (1, LANES // RWKV_HEADS))


def _lru_kernel(gate_ref, x_ref, cinit_ref, cw_ref, cb_ref, wri_ref, br_ref, bi_ref, lam_ref, h0_ref,
                o_ref, hfin_ref, halo, hprev, *, tt, bb):
    j = pl.program_id(1)

    @pl.when(j == 0)
    def _():
        halo[...] = cinit_ref[...]
        for i in range(bb):
            hprev[i] = jnp.broadcast_to(h0_ref[i], (SUBLANES, LRU_WIDTH))

    rows = _row_iota((tt, LRU_BLOCK))
    for i in range(bb):
        u = x_ref[i]
        ext = jnp.concatenate([halo[i], u], axis=0)
        xc = cb_ref[...] + _shifted(ext, 3, tt) * cw_ref[0:1, :]
        xc = xc + _shifted(ext, 2, tt) * cw_ref[1:2, :]
        xc = xc + _shifted(ext, 1, tt) * cw_ref[2:3, :]
        xc = xc + u * cw_ref[3:4, :]
        halo[i] = u[tt - SUBLANES:tt, :]

        for n in range(LRU_BLOCKS):
            cs = slice(n * LRU_BLOCK, (n + 1) * LRU_BLOCK)
            xb = xc[:, cs]
            ri = _dot(xb.astype(BF16), wri_ref[n])
            rg = jax.nn.sigmoid(ri[:, :LRU_BLOCK] + br_ref[:, cs])
            ig = jax.nn.sigmoid(ri[:, LRU_BLOCK:] + bi_ref[:, cs])
            log_a = -LRU_C * rg * _softplus(-lam_ref[:, cs])
            a = jnp.exp(log_a)
            b = jnp.sqrt(-jnp.tanh(log_a) * (a * a + 1.0)) * ig * xb
            d = 1
            while d < tt:
                keep = rows >= d
                a_sh = jnp.where(keep, pltpu.roll(a, d, axis=0), 1.0)
                b_sh = jnp.where(keep, pltpu.roll(b, d, axis=0), 0.0)
                b = b + a * b_sh
                a = a * a_sh
                d *= 2
            h = a * hprev[i, 0:1, cs] + b
            o_ref[i, :, cs] = (h * _gelu_tanh(gate_ref[i, :, cs])).astype(BF16)
            h_last = h[tt - 1:tt, :]
            hprev[i, :, cs] = jnp.broadcast_to(h_last, (SUBLANES, LRU_BLOCK))
            hfin_ref[i, :, cs] = h_last


def _lru(p1, cinit, cw, cb, wri, br, bi, lam, h0, bsz, t_len):
    tt = min(t_len, 64)
    nt = t_len // tt
    bb = _batch_block(bsz, t_len)
    p3 = p1.reshape(bsz, t_len, P1_COLS)
    rows = lambda w: (lambda b, j: (b, j, w))
    per_seq = lambda b, j: (b, 0, 0)
    const2 = lambda b, j: (0, 0)
    o, h_fin = pl.pallas_call(
        functools.partial(_lru_kernel, tt=tt, bb=bb),
        grid=(bsz // bb, nt),
        in_specs=[
            pl.BlockSpec((bb, tt, 1024), rows(0)),
            pl.BlockSpec((bb, tt, 1024), rows(1)),
            pl.BlockSpec((bb, SUBLANES, LRU_WIDTH), per_seq),
            pl.BlockSpec((SUBLANES, LRU_WIDTH), const2),
            pl.BlockSpec((1, LRU_WIDTH), const2),
            pl.BlockSpec((LRU_BLOCKS, LRU_BLOCK, 2 * LRU_BLOCK), lambda b, j: (0, 0, 0)),
            pl.BlockSpec((1, LRU_WIDTH), const2),
            pl.BlockSpec((1, LRU_WIDTH), const2),
            pl.BlockSpec((1, LRU_WIDTH), const2),
            pl.BlockSpec((bb, 1, LRU_WIDTH), per_seq),
        ],
        out_specs=[pl.BlockSpec((bb, tt, LRU_WIDTH), rows(0)),
                   pl.BlockSpec((bb, 1, LRU_WIDTH), per_seq)],
        out_shape=[jax.ShapeDtypeStruct((bsz, t_len, LRU_WIDTH), BF16),
                   jax.ShapeDtypeStruct((bsz, 1, LRU_WIDTH), F32)],
        scratch_shapes=[pltpu.VMEM((bb, SUBLANES, LRU_WIDTH), F32), pltpu.VMEM((bb, SUBLANES, LRU_WIDTH), F32)],
        compiler_params=_cparams(("parallel", "arbitrary")),
        name="lru",
    )(p3, p3, cinit, cw, cb, wri, br, bi, lam, h0)
    return o.reshape(bsz * t_len, LRU_WIDTH), h_fin


def _expand_heads(x, e_ref, terms):
    parts = []
    rest = x
    for _ in range(terms):
        part = rest.astype(BF16).astype(F32)
        parts.append(part)
        rest = rest - part
    rows = x.shape[0]
    out = _dot(jnp.concatenate(parts, axis=0), e_ref[...])
    acc = out[:rows]
    for t in range(1, terms):
        acc = acc + out[t * rows:(t + 1) * rows]
    return acc


def _ssd_kernel(z_ref, xbc_ref, dt_ref, cinit_ref, cw_ref, cb_ref, dtb_ref, alog_ref, dd_ref, nw_ref,
                ej_ref, ep_ref, s0_ref, o_ref, s_ref, halo, st_t, *, c, mm, bb):
    j = pl.program_id(1)
    hg = SSD_HEADS // SSD_GROUPS
    gw = SSD_INNER // SSD_GROUPS
    pair = 2 * SSD_HEADDIM

    @pl.when(j == 0)
    def _():
        halo[...] = cinit_ref[...]
        for i in range(bb):
            for q in range(SSD_HEADS // 2):
                two = jnp.concatenate([s0_ref[i, 2 * q], s0_ref[i, 2 * q + 1]], axis=0)
                st_t[i, :, q * pair:(q + 1) * pair] = two.T

    wj = SSD_HEADS * c
    row_t = _row_iota((c, wj))
    lane_t = lax.broadcasted_iota(jnp.int32, (c, wj), 1) & (c - 1)
    causal = lane_t <= row_t
    diag = lane_t == row_t
    bd_rows = hg * c
    bd_mask = ((_row_iota((bd_rows, gw)) >> (c.bit_length() - 1))
               == (lax.broadcasted_iota(jnp.int32, (bd_rows, gw), 1) >> (SSD_HEADDIM.bit_length() - 1)))
    neg_a = -jnp.exp(alog_ref[...])
    gs = SSD_GROUPS * SSD_STATE
    for i in range(bb):
        u = xbc_ref[i]
        ext = jnp.concatenate([halo[i], u], axis=0)
        xc = cb_ref[...] + _shifted(ext, 3, c) * cw_ref[0:1, :]
        xc = xc + _shifted(ext, 2, c) * cw_ref[1:2, :]
        xc = xc + _shifted(ext, 1, c) * cw_ref[2:3, :]
        xc = _silu(xc + u * cw_ref[3:4, :])
        halo[i] = u[c - SUBLANES:c, :]

        xs = xc[:, :SSD_INNER]
        bm = xc[:, SSD_INNER:SSD_INNER + gs].astype(mm)
        cm = xc[:, SSD_INNER + gs:].astype(mm)
        dt = _softplus(dt_ref[i] + dtb_ref[...])
        cs = _cumsum_rows(dt * neg_a, c)
        cs_p = _expand_heads(cs, ep_ref, 3)
        dt_p = _expand_heads(dt, ep_ref, 2)
        cs_j = cs_p if c == SSD_HEADDIM else _expand_heads(cs, ej_ref, 3)
        cs_row = jnp.sum(jnp.where(diag, cs_j, 0.0), axis=0, keepdims=True)
        lmat = jnp.exp(jnp.where(causal, cs_j - cs_row, -jnp.inf))
        last_p = cs_p[c - 1:c, :]
        e_cs = jnp.exp(cs_p)
        e_last = jnp.exp(last_p)
        xdt = xs * dt_p
        xw = (xdt * jnp.exp(last_p - cs_p)).astype(mm)
        xdt = xdt.astype(mm)
        y_groups = []
        for g in range(SSD_GROUPS):
            gl = slice(g * gw, (g + 1) * gw)
            ss = slice(g * SSD_STATE, (g + 1) * SSD_STATE)
            cb_x = _dot_nt(cm[:, ss], jnp.concatenate([bm[:, ss]] * hg, axis=0))
            m_x = (cb_x * lmat[:, g * bd_rows:(g + 1) * bd_rows]).astype(mm)
            blocks = jnp.where(bd_mask, jnp.concatenate([xdt[:, gl]] * hg, axis=0), jnp.zeros((), mm))
            st = st_t[i, :, gl]
            y_groups.append(_dot(m_x, blocks) + e_cs[:, gl] * _dot(cm[:, ss], st.astype(mm)))
            st_t[i, :, gl] = st * e_last[:, gl] + _dot_tn(bm[:, ss], xw[:, gl])
        y = (jnp.concatenate(y_groups, axis=1) + dd_ref[...] * xs) * _silu(z_ref[i])
        for g in range(SSD_GROUPS):
            gl = slice(g * gw, (g + 1) * gw)
            o_ref[i, :, gl] = _rms(y[:, gl], nw_ref[:, gl], 1e-5).astype(BF16)

    @pl.when(j == pl.num_programs(1) - 1)
    def _():
        for i in range(bb):
            for q in range(SSD_HEADS // 2):
                two = st_t[i, :, q * pair:(q + 1) * pair].T
                s_ref[i, 2 * q] = two[:SSD_HEADDIM]
                s_ref[i, 2 * q + 1] = two[SSD_HEADDIM:]


def _ssd(p1, cinit, cw, cb, dtb, alog, dd, nw, s0, bsz, t_len):
    c = min(SSD_CHUNK, t_len)
    nt = t_len // c
    bb = _batch_block(bsz, t_len)
    mm = BF16 if c % 16 == 0 else F32
    p3 = p1.reshape(bsz, t_len, P1_COLS)
    rows = lambda w: (lambda b, j: (b, j, w))
    const2 = lambda b, j: (0, 0)
    st_spec = pl.BlockSpec((bb, SSD_HEADS, SSD_HEADDIM, SSD_STATE), lambda b, j: (b, 0, 0, 0))
    head = jnp.arange(LANES, dtype=jnp.int32)[:, None]
    expand = lambda width: (jnp.arange(SSD_HEADS * width, dtype=jnp.int32)[None, :] // width == head).astype(F32)
    o, s_new = pl.pallas_call(
        functools.partial(_ssd_kernel, c=c, mm=mm, bb=bb),
        grid=(bsz // bb, nt),
        in_specs=[
            pl.BlockSpec((bb, c, 1024), rows(2)),
            pl.BlockSpec((bb, c, SSD_CONV_CH), rows(2)),
            pl.BlockSpec((bb, c, LANES), rows(36)),
            pl.BlockSpec((bb, SUBLANES, SSD_CONV_CH), lambda b, j: (b, 0, 0)),
            pl.BlockSpec((SUBLANES, SSD_CONV_CH), const2),
            pl.BlockSpec((1, SSD_CONV_CH), const2),
            pl.BlockSpec((1, LANES), const2),
            pl.BlockSpec((1, LANES), const2),
            pl.BlockSpec((1, SSD_INNER), const2),
            pl.BlockSpec((1, SSD_INNER), const2),
            pl.BlockSpec((LANES, SSD_HEADS * c), const2),
            pl.BlockSpec((LANES, SSD_INNER), const2),
            st_spec,
        ],
        out_specs=[pl.BlockSpec((bb, c, SSD_INNER), rows(0)), st_spec],
        out_shape=[jax.ShapeDtypeStruct((bsz, t_len, SSD_INNER), BF16),
                   jax.ShapeDtypeStruct((bsz, SSD_HEADS, SSD_HEADDIM, SSD_STATE), F32)],
        scratch_shapes=[pltpu.VMEM((bb, SUBLANES, SSD_CONV_CH), F32),
                        pltpu.VMEM((bb, SSD_STATE, SSD_INNER), F32)],
        compiler_params=_cparams(("parallel", "arbitrary")),
        name="ssd",
    )(p3, p3, p3, cinit, cw, cb, dtb, alog, dd, nw, expand(c), expand(SSD_HEADDIM), s0)
    return o.reshape(bsz * t_len, SSD_INNER), s_new


def _pad_rows_to_8(x):
    return jnp.pad(x, ((0, 0), (SUBLANES - x.shape[1], 0), (0, 0)))


def _pad_lanes(x, width):
    return jnp.pad(x, ((0, 0), (0, width - x.shape[1])))


def _trunk(x, states, w):
    s_gla, s_rwkv, s_shift, s_lru, s_lru_conv, s_ssd, s_ssd_conv = states
    bsz, t_len, d = x.shape
    n = bsz * t_len
    x2 = x.reshape(n, d)

    p0 = _in_proj(x2, w["g_mix0"], w["w_in0"], P0_TN)
    o_gla, s_gla_new = _gla(p0, s_gla, w["gla_wa2"], w["gla_ba"], w["gla_gn"], bsz, t_len)
    shift8 = jnp.broadcast_to(s_shift[:, None, :], (bsz, SUBLANES, RWKV_COLS)).reshape(bsz * SUBLANES, RWKV_COLS)
    r, wdec, k, v, a, g = _rwkv_prep(p0, shift8, w["rwkv_mu"], w["rwkv_w0"], w["rwkv_a0"], w["rwkv_wa"],
                                     w["rwkv_g2"], bsz, t_len)
    chains = [_to_chains(t, bsz, t_len) for t in (r, wdec, k, v, a)]
    s0c = s_rwkv.transpose(3, 2, 0, 1).reshape(RWKV_HEAD, RWKV_HEAD, bsz * RWKV_HEADS)
    y_c, s_c = _rwkv_scan(*chains, w["rwkv_tiles"], s0c, t_len)
    y_rwkv = _from_chains(y_c, bsz, t_len)
    s_rwkv_new = s_c.reshape(RWKV_HEAD, RWKV_HEAD, bsz, RWKV_HEADS).transpose(2, 3, 1, 0)
    shift_new = p0.reshape(bsz, t_len, P0_COLS)[:, -1, 3072:3072 + RWKV_COLS]
    x2 = _out_ffn(x2, o_gla, y_rwkv, g, w["w_out0"], w["g_ffn0"], w["wg0"], w["wu0"], w["wd0"], None)

    p1 = _in_proj(x2, w["g_mix1"], w["w_in1"], P1_TN)
    lru_out, h_lru = _lru(p1, _pad_rows_to_8(s_lru_conv), w["lru_cw"], w["lru_cb"], w["lru_wri"], w["lru_br"],
                          w["lru_bi"], w["lru_lam"], s_lru.reshape(bsz, 1, LRU_WIDTH), bsz, t_len)
    y_ssd, s_ssd_new = _ssd(p1, _pad_rows_to_8(s_ssd_conv), w["ssd_cw"], w["ssd_cb"], w["ssd_dtb"], w["ssd_alog"],
                            w["ssd_dd"], w["ssd_nw"], s_ssd, bsz, t_len)
    keep = CONV_W - 1
    p1_tail = p1.reshape(bsz, t_len, P1_COLS)[:, -keep:]
    lru_conv_new = p1_tail[:, :, 1024:2048]
    ssd_conv_new = p1_tail[:, :, 3072:3072 + SSD_CONV_CH]
    y = _out_ffn(x2, lru_out, y_ssd, None, w["w_out1"], w["g_ffn1"], w["wg1"], w["wu1"], w["wd1"], w["g_final"])
    return y.reshape(bsz, t_len, d), (s_gla_new, s_rwkv_new, shift_new, h_lru.reshape(bsz, LRU_WIDTH),
                                      lru_conv_new, s_ssd_new, ssd_conv_new)


def kernel(x_prompt, x_sample, state_gla, state_rwkv, state_rwkv_shift, state_lru, state_lru_conv, state_ssd, state_ssd_conv, w_in0, gla_w_a2, gla_b_a, gla_g_norm, rwkv_mu, rwkv_w0, rwkv_w2, rwkv_a0, rwkv_a2, rwkv_g2, rwkv_k_k, rwkv_k_a, rwkv_r_k, rwkv_ln_w, rwkv_ln_b, w_out0, w_in1, lru_conv_w, lru_conv_b, lru_w_r, lru_b_r, lru_w_i, lru_b_i, lru_lambda, ssd_conv_w, ssd_conv_b, ssd_dt_bias, ssd_a_log, ssd_d, ssd_norm_w, w_out1, g_mix, g_ffn, w_ffn_gate, w_ffn_up, w_ffn_down, g_final):
    row = lambda p: p.reshape(1, -1).astype(F32)
    w0 = jnp.concatenate([w_in0[:, :2048], w_in0[:, 2064:3088], w_in0[:, 3088:],
                          _pad_lanes(w_in0[:, 2048:2064], LANES)], axis=1)
    zeros64 = jnp.zeros((64, RWKV_WIDTH), F32)
    rwkv_wa = jnp.concatenate([jnp.concatenate([rwkv_w2, zeros64], axis=1),
                               jnp.concatenate([zeros64, rwkv_a2], axis=1)], axis=0)
    w = {
        "w_in0": w0.astype(BF16),
        "w_in1": _pad_lanes(w_in1, P1_COLS).astype(BF16),
        "g_mix0": g_mix[0], "g_mix1": g_mix[1], "g_ffn0": g_ffn[0], "g_ffn1": g_ffn[1], "g_final": g_final,
        "gla_wa2": jnp.pad(gla_w_a2, ((0, LANES - GLA_GATE_RANK), (0, 0))).astype(BF16),
        "gla_ba": row(gla_b_a), "gla_gn": row(gla_g_norm),
        "rwkv_mu": row(rwkv_mu), "rwkv_w0": row(rwkv_w0), "rwkv_a0": row(rwkv_a0),
        "rwkv_wa": rwkv_wa.astype(BF16), "rwkv_g2": rwkv_g2.astype(BF16),
        "rwkv_tiles": [_chain_tile(p) for p in (rwkv_k_k, rwkv_k_a, rwkv_r_k, rwkv_ln_w, rwkv_ln_b)],
        "w_out0": w_out0.astype(BF16), "w_out1": w_out1.astype(BF16),
        "wg0": w_ffn_gate[0].astype(BF16), "wu0": w_ffn_up[0].astype(BF16), "wd0": w_ffn_down[0].astype(BF16),
        "wg1": w_ffn_gate[1].astype(BF16), "wu1": w_ffn_up[1].astype(BF16), "wd1": w_ffn_down[1].astype(BF16),
        "lru_cw": jnp.pad(lru_conv_w, ((0, SUBLANES - CONV_W), (0, 0))), "lru_cb": row(lru_conv_b),
        "lru_wri": jnp.concatenate([lru_w_r, lru_w_i], axis=2).astype(BF16),
        "lru_br": row(lru_b_r), "lru_bi": row(lru_b_i), "lru_lam": row(lru_lambda),
        "ssd_cw": jnp.pad(ssd_conv_w, ((0, SUBLANES - CONV_W), (0, 0))), "ssd_cb": row(ssd_conv_b),
        "ssd_dtb": _pad_lanes(row(ssd_dt_bias), LANES), "ssd_alog": _pad_lanes(row(ssd_a_log), LANES),
        "ssd_dd": row(jnp.repeat(ssd_d, SSD_HEADDIM)), "ssd_nw": row(ssd_norm_w),
    }
    bp = x_prompt.shape[0]
    prompt_init = (
        jnp.zeros((bp, GLA_HEADS, GLA_DK, GLA_DV), F32),
        jnp.zeros((bp, RWKV_HEADS, RWKV_HEAD, RWKV_HEAD), F32),
        jnp.zeros((bp, RWKV_COLS), F32),
        jnp.zeros((bp, LRU_WIDTH), F32),
        jnp.zeros((bp, CONV_W - 1, LRU_WIDTH), F32),
        jnp.zeros((bp, SSD_HEADS, SSD_HEADDIM, SSD_STATE), F32),
        jnp.zeros((bp, CONV_W - 1, SSD_CONV_CH), F32),
    )
    sample_init = (state_gla, state_rwkv, state_rwkv_shift, state_lru, state_lru_conv, state_ssd, state_ssd_conv)
    y_prompt, p_states = _trunk(x_prompt, prompt_init, w)
    y_sample, s_states = _trunk(x_sample, sample_init, w)
    return (y_prompt, y_sample, *p_states, *s_states)
```

```python
import functools

import jax
import jax.numpy as jnp
from jax import lax
from jax.experimental import pallas as pl
from jax.experimental.pallas import tpu as pltpu

F32 = jnp.float32
BF16 = jnp.bfloat16

D_MODEL = 1024
NORM_EPS = 1e-6
GLA_HEADS = 4
GLA_DK = 128
GLA_DV = 256
GLA_GATE_RANK = 16
GLA_GATE_NORM = 16.0
GLA_CHUNK = 64
RWKV_HEAD = 64
RWKV_HEADS = 16
RWKV_WIDTH = 1024
RWKV_LORA = 256
RWKV_COLS = 3 * RWKV_WIDTH + RWKV_LORA
RWKV_GN_EPS = 64e-5
LRU_WIDTH = 1024
LRU_BLOCKS = 8
LRU_BLOCK = 128
LRU_C = 8.0
CONV_W = 4
SSD_INNER = 1024
SSD_HEADDIM = 64
SSD_HEADS = 16
SSD_GROUPS = 2
SSD_STATE = 128
SSD_CHUNK = 64
SSD_CONV_CH = SSD_INNER + 2 * SSD_GROUPS * SSD_STATE
D_FF = 2816

LANES = 128
SUBLANES = 8
VMEM_LIMIT_BYTES = 56 * 1024 * 1024

P0_COLS = 6528
P0_TN = 2176
P1_COLS = 4864
P1_TN = 2432


def _cparams(semantics):
    return pltpu.CompilerParams(dimension_semantics=semantics, vmem_limit_bytes=VMEM_LIMIT_BYTES)


def _dot(a, b):
    return jnp.dot(a, b, preferred_element_type=F32)


def _dot_nt(a, b):
    return lax.dot_general(a, b, (((1,), (1,)), ((), ())), preferred_element_type=F32)


def _dot_tn(a, b):
    return lax.dot_general(a, b, (((0,), (0,)), ((), ())), preferred_element_type=F32)


def _softplus(x):
    return jnp.maximum(x, 0.0) + jnp.log1p(jnp.exp(-jnp.abs(x)))


def _silu(x):
    return x * jax.nn.sigmoid(x)


def _gelu_tanh(x):
    return 0.5 * x * (1.0 + jnp.tanh(0.7978845608028654 * (x + 0.044715 * (x * x * x))))


def _rms(x, g, eps):
    return x * lax.rsqrt(jnp.mean(x * x, axis=-1, keepdims=True) + eps) * g


def _row_iota(shape):
    return lax.broadcasted_iota(jnp.int32, shape, 0)


def _cumsum_rows(x, seg):
    rows = _row_iota(x.shape) & (seg - 1)
    d = 1
    while d < seg:
        x = x + jnp.where(rows >= d, pltpu.roll(x, d, axis=0), 0.0)
        d *= 2
    return x


def _shifted(ext, k, rows):
    return pltpu.roll(ext, k, axis=0)[SUBLANES:SUBLANES + rows]


def _in_proj_kernel(x_ref, g_ref, w_ref, o_ref, h_scr):
    @pl.when(pl.program_id(1) == 0)
    def _():
        h_scr[...] = _rms(x_ref[...], g_ref[...], NORM_EPS).astype(BF16)

    o_ref[...] = _dot(h_scr[...], w_ref[...])


def _in_proj(x2d, g, w, tn):
    n, d = x2d.shape
    cols = w.shape[1]
    tm = min(n, 1024)
    return pl.pallas_call(
        _in_proj_kernel,
        grid=(n // tm, cols // tn),
        in_specs=[
            pl.BlockSpec((tm, d), lambda i, j: (i, 0)),
            pl.BlockSpec((1, d), lambda i, j: (0, 0)),
            pl.BlockSpec((d, tn), lambda i, j: (0, j)),
        ],
        out_specs=pl.BlockSpec((tm, tn), lambda i, j: (i, j)),
        out_shape=jax.ShapeDtypeStruct((n, cols), F32),
        scratch_shapes=[pltpu.VMEM((tm, d), BF16)],
        compiler_params=_cparams(("parallel", "arbitrary")),
        name="in_proj",
    )(x2d, g.reshape(1, d), w)


def _out_ffn_kernel(*refs, has_gate, final_norm):
    refs = list(refs)
    x_ref, oa_ref, ob_ref = refs[:3]
    pos = 3
    gate_ref = None
    if has_gate:
        gate_ref = refs[pos]
        pos += 1
    woa_ref, wob_ref, gffn_ref, wg_ref, wu_ref, wd_ref = refs[pos:pos + 6]
    pos += 6
    gfin_ref = None
    if final_norm:
        gfin_ref = refs[pos]
        pos += 1
    out_ref, x1_scr, h_scr, acc_scr = refs[pos:pos + 4]

    k = pl.program_id(1)

    @pl.when(k == 0)
    def _():
        ob = ob_ref[...]
        if has_gate:
            ob = (ob * gate_ref[...]).astype(BF16)
        x1 = x_ref[...] + (_dot(oa_ref[...], woa_ref[...]) + _dot(ob, wob_ref[...]))
        x1_scr[...] = x1
        h_scr[...] = _rms(x1, gffn_ref[...], NORM_EPS).astype(BF16)
        acc_scr[...] = jnp.zeros_like(acc_scr)

    h = h_scr[...]
    act = (_silu(_dot(h, wg_ref[...])) * _dot(h, wu_ref[...])).astype(BF16)
    acc_scr[...] += _dot(act, wd_ref[...])

    @pl.when(k == pl.num_programs(1) - 1)
    def _():
        y = x1_scr[...] + acc_scr[...]
        if final_norm:
            y = _rms(y, gfin_ref[...], NORM_EPS)
        out_ref[...] = y


def _out_ffn(x2d, oa, ob, gate, wo, g_ffn, wg, wu, wd, g_final):
    n, d = x2d.shape
    tm = min(n, 512)
    tf = D_FF // 2
    has_gate = gate is not None
    final_norm = g_final is not None
    row = lambda i, k: (i, 0)
    const = lambda i, k: (0, 0)
    args = [x2d, oa, ob]
    in_specs = [pl.BlockSpec((tm, d), row)] * 3
    if has_gate:
        args.append(gate)
        in_specs.append(pl.BlockSpec((tm, d), row))
    half = wo.shape[0] // 2
    args += [wo[:half], wo[half:], g_ffn.reshape(1, d), wg, wu, wd]
    in_specs += [
        pl.BlockSpec((half, d), const),
        pl.BlockSpec((half, d), const),
        pl.BlockSpec((1, d), const),
        pl.BlockSpec((d, tf), lambda i, k: (0, k)),
        pl.BlockSpec((d, tf), lambda i, k: (0, k)),
        pl.BlockSpec((tf, d), lambda i, k: (k, 0)),
    ]
    if final_norm:
        args.append(g_final.reshape(1, d))
        in_specs.append(pl.BlockSpec((1, d), const))
    return pl.pallas_call(
        functools.partial(_out_ffn_kernel, has_gate=has_gate, final_norm=final_norm),
        grid=(n // tm, D_FF // tf),
        in_specs=in_specs,
        out_specs=pl.BlockSpec((tm, d), row),
        out_shape=jax.ShapeDtypeStruct((n, d), F32),
        scratch_shapes=[pltpu.VMEM((tm, d), F32), pltpu.VMEM((tm, d), BF16), pltpu.VMEM((tm, d), F32)],
        compiler_params=_cparams(("parallel", "arbitrary")),
        name="out_ffn",
    )(*args)


def _gla_kernel(q_ref, k_ref, v_ref, og_ref, al_ref, wa2_ref, ba_ref, gn_ref, s0_ref,
                o_ref, sout_ref, st_scr, *, c, mm, bb):
    n = pl.program_id(1)

    @pl.when(n == 0)
    def _():
        for i in range(bb):
            for h in range(GLA_HEADS):
                st_scr[i * GLA_HEADS + h] = s0_ref[i, h].T

    mask = _row_iota((c, c)) >= lax.broadcasted_iota(jnp.int32, (c, c), 1)
    gn = gn_ref[...]
    for i in range(bb):
        la_all = _dot(al_ref[i].astype(BF16), wa2_ref[...]) + ba_ref[...]
        la_all = -_softplus(-la_all) * (1.0 / GLA_GATE_NORM)
        b_all = _cumsum_rows(la_all, c)
        for h in range(GLA_HEADS):
            ks = slice(h * GLA_DK, (h + 1) * GLA_DK)
            vs = slice(h * GLA_DV, (h + 1) * GLA_DV)
            b = b_all[:, ks]
            b_last = b[c - 1:c, :]
            q = q_ref[i, :, ks] * (GLA_DK ** -0.5)
            k = k_ref[i, :, ks]
            v = v_ref[i, :, vs].astype(mm)
            qd = (q * jnp.exp(b)).astype(mm)
            kd = (k * jnp.exp(-b)).astype(mm)
            kc = (k * jnp.exp(b_last - b)).astype(mm)
            scores = jnp.where(mask, _dot_nt(qd, kd), 0.0)
            st = st_scr[i * GLA_HEADS + h]
            o = _dot(scores.astype(mm), v) + _dot_nt(qd, st.astype(mm))
            st_scr[i * GLA_HEADS + h] = st * jnp.exp(b_last) + _dot_tn(v, kc)
            o = _rms(o, gn, 1e-5)
            o_ref[i, :, vs] = (o * _silu(og_ref[i, :, vs])).astype(BF16)

    @pl.when(n == pl.num_programs(1) - 1)
    def _():
        for i in range(bb):
            for h in range(GLA_HEADS):
                sout_ref[i, h] = st_scr[i * GLA_HEADS + h].T


def _batch_block(bsz, t_len):
    return 2 if t_len >= 64 else min(bsz, 8)


def _gla(p0, s0, wa2, ba, gn, bsz, t_len):
    c = min(GLA_CHUNK, t_len)
    nt = t_len // c
    bb = _batch_block(bsz, t_len)
    mm = BF16 if c % 16 == 0 else F32
    p3 = p0.reshape(bsz, t_len, P0_COLS)
    rows = lambda w: (lambda b, j: (b, j, w))
    const = lambda b, j: (0, 0)
    st_spec = pl.BlockSpec((bb, GLA_HEADS, GLA_DK, GLA_DV), lambda b, j: (b, 0, 0, 0))
    o, s_new = pl.pallas_call(
        functools.partial(_gla_kernel, c=c, mm=mm, bb=bb),
        grid=(bsz // bb, nt),
        in_specs=[
            pl.BlockSpec((bb, c, 512), rows(0)),
            pl.BlockSpec((bb, c, 512), rows(1)),
            pl.BlockSpec((bb, c, 1024), rows(1)),
            pl.BlockSpec((bb, c, 1024), rows(2)),
            pl.BlockSpec((bb, c, LANES), rows(50)),
            pl.BlockSpec((LANES, 512), const),
            pl.BlockSpec((1, 512), const),
            pl.BlockSpec((1, GLA_DV), const),
            st_spec,
        ],
        out_specs=[pl.BlockSpec((bb, c, 1024), rows(0)), st_spec],
        out_shape=[jax.ShapeDtypeStruct((bsz, t_len, 1024), BF16),
                   jax.ShapeDtypeStruct((bsz, GLA_HEADS, GLA_DK, GLA_DV), F32)],
        scratch_shapes=[pltpu.VMEM((bb * GLA_HEADS, GLA_DV, GLA_DK), F32)],
        compiler_params=_cparams(("parallel", "arbitrary")),
        name="gla",
    )(p3, p3, p3, p3, p3, wa2, ba, gn, s0)
    return o.reshape(bsz * t_len, 1024), s_new


def _rwkv_prep_kernel(r_ref, k_ref, v_ref, l_ref, ir_ref, ik_ref, iv_ref, il_ref,
                      mu_ref, w0_ref, a0_ref, wa_ref, g2_ref,
                      ro_ref, wo_ref, ko_ref, vo_ref, ao_ref, go_ref, carry,
                      *, tm, t_len, multi_seq):
    j = pl.program_id(1)
    cols = ((0, 1024), (1024, 2048), (2048, 3072), (3072, RWKV_COLS))

    if not multi_seq:
        @pl.when(j == 0)
        def _():
            for (lo, hi), iref in zip(cols, (ir_ref, ik_ref, iv_ref, il_ref)):
                carry[:, lo:hi] = iref[...]

    def mixed(x_ref, i_ref, lo, hi):
        x = x_ref[...]
        rolled = pltpu.roll(x, 1, axis=0)
        rows = _row_iota(x.shape)
        if multi_seq:
            prev = jnp.where((rows & (t_len - 1)) == 0, i_ref[...], rolled)
        else:
            prev = jnp.where(rows == 0, carry[SUBLANES - 1:SUBLANES, lo:hi], rolled)
            carry[:, lo:hi] = x[tm - SUBLANES:tm, :]
        return x + (prev - x) * mu_ref[:, lo:hi]

    r = mixed(r_ref, ir_ref, *cols[0])
    k = mixed(k_ref, ik_ref, *cols[1])
    v = mixed(v_ref, iv_ref, *cols[2])
    lo_rank = mixed(l_ref, il_ref, *cols[3])

    wa_in = lo_rank[:, :LANES]
    lane = lax.broadcasted_iota(jnp.int32, wa_in.shape, 1)
    wa_in = jnp.where(lane < 64, jnp.tanh(wa_in), wa_in).astype(BF16)
    wa = _dot(wa_in, wa_ref[...])
    g = _dot(jax.nn.sigmoid(lo_rank[:, LANES:]).astype(BF16), g2_ref[...])
    w = -_softplus(-(w0_ref[...] + wa[:, :RWKV_WIDTH])) - 0.5
    ro_ref[...] = r
    wo_ref[...] = jnp.exp(-jnp.exp(w))
    ko_ref[...] = k
    vo_ref[...] = v
    ao_ref[...] = jax.nn.sigmoid(a0_ref[...] + wa[:, RWKV_WIDTH:])
    go_ref[...] = g


def _rwkv_prep(p0, shift8, mu, w0, a0, wa, g2, bsz, t_len):
    n = bsz * t_len
    multi_seq = t_len < 256
    tm = min(n, 512) if multi_seq else 256
    nt = 1 if multi_seq else t_len // tm
    nb = n // tm if multi_seq else bsz
    ti = tm if multi_seq else SUBLANES
    rows = lambda w: (lambda i, j: (i * nt + j, w))
    init = lambda w: (lambda i, j: (i, w))
    const = lambda i, j: (0, 0)
    out = jax.ShapeDtypeStruct((n, RWKV_WIDTH), F32)
    return pl.pallas_call(
        functools.partial(_rwkv_prep_kernel, tm=tm, t_len=t_len, multi_seq=multi_seq),
        grid=(nb, nt),
        in_specs=[
            pl.BlockSpec((tm, 1024), rows(3)),
            pl.BlockSpec((tm, 1024), rows(4)),
            pl.BlockSpec((tm, 1024), rows(5)),
            pl.BlockSpec((tm, RWKV_LORA), rows(24)),
            pl.BlockSpec((ti, 1024), init(0)),
            pl.BlockSpec((ti, 1024), init(1)),
            pl.BlockSpec((ti, 1024), init(2)),
            pl.BlockSpec((ti, RWKV_LORA), init(12)),
            pl.BlockSpec((1, RWKV_COLS), const),
            pl.BlockSpec((1, RWKV_WIDTH), const),
            pl.BlockSpec((1, RWKV_WIDTH), const),
            pl.BlockSpec((LANES, 2 * RWKV_WIDTH), const),
            pl.BlockSpec((LANES, RWKV_WIDTH), const),
        ],
        out_specs=[pl.BlockSpec((tm, RWKV_WIDTH), rows(0))] * 6,
        out_shape=[out] * 6,
        scratch_shapes=[pltpu.VMEM((SUBLANES, RWKV_COLS), F32)],
        compiler_params=_cparams(("parallel", "arbitrary")),
        name="rwkv_prep",
    )(p0, p0, p0, p0, shift8, shift8, shift8, shift8, mu, w0, a0, wa, g2)


def _rwkv_scan_kernel(r_ref, w_ref, k_ref, a_ref, v_ref, kk_ref, ka_ref, rk_ref, lnw_ref, lnb_ref,
                      s0_ref, y_ref, s_ref, kk_scr, kb_scr, kp_scr, bonus_scr, *, tt):
    @pl.when(pl.program_id(1) == 0)
    def _():
        s_ref[...] = s0_ref[...]

    def tile_row(ref, j):
        return ref[pl.ds(j, 1), :]

    norm2 = jnp.zeros((tt, LANES), F32)
    for j in range(RWKV_HEAD):
        kj = k_ref[0, j] * tile_row(kk_ref, j)
        norm2 = norm2 + kj * kj
    norm = jnp.maximum(jnp.sqrt(norm2), 1e-12)
    bonus = jnp.zeros((tt, LANES), F32)
    for j in range(RWKV_HEAD):
        k = k_ref[0, j]
        a = a_ref[0, j]
        kk = k * tile_row(kk_ref, j) / norm
        kp = k * (1.0 + (a - 1.0) * tile_row(ka_ref, j))
        kk_scr[j] = kk
        kb_scr[j] = kk * a
        kp_scr[j] = kp
        bonus = bonus + r_ref[0, j] * kp * tile_row(rk_ref, j)
    bonus_scr[...] = bonus

    def row(ref, j, s):
        return jnp.broadcast_to(ref[j, pl.ds(s, 1), :], (RWKV_HEAD, LANES))

    def row_in(ref, j, s):
        return jnp.broadcast_to(ref[0, j, pl.ds(s, 1), :], (RWKV_HEAD, LANES))

    s_kk = jnp.zeros((RWKV_HEAD, LANES), F32)
    for j in range(RWKV_HEAD):
        s_kk = s_kk + s_ref[j] * row(kk_scr, j, 0)

    def step(s, s_kk):
        sa = -s_kk
        v = v_ref[pl.ds(s, RWKV_HEAD, stride=tt), :]
        nxt = jnp.minimum(s + 1, tt - 1)
        y = jnp.zeros_like(v)
        s_kk_next = jnp.zeros_like(v)
        for j in range(RWKV_HEAD):
            sn = s_ref[j] * row_in(w_ref, j, s) + sa * row(kb_scr, j, s) + v * row(kp_scr, j, s)
            s_ref[j] = sn
            y = y + sn * row_in(r_ref, j, s)
            s_kk_next = s_kk_next + sn * row(kk_scr, j, nxt)
        mu = jnp.mean(y, axis=0, keepdims=True)
        yc = y - mu
        var = jnp.mean(yc * yc, axis=0, keepdims=True)
        out = yc * lax.rsqrt(var + RWKV_GN_EPS) * lnw_ref[...] + lnb_ref[...] + bonus_scr[pl.ds(s, 1), :] * v
        y_ref[pl.ds(s, RWKV_HEAD, stride=tt), :] = out
        return s_kk_next

    lax.fori_loop(0, tt, step, s_kk)


def _rwkv_scan(r, w, k, a, v, tiles, s0, tt):
    nb, _, _, nc = r.shape
    seq = pl.BlockSpec((1, RWKV_HEAD, tt, LANES), lambda g, t: (t, 0, 0, g))
    flat = pl.BlockSpec((RWKV_HEAD * tt, LANES), lambda g, t: (t, g))
    tile = pl.BlockSpec((RWKV_HEAD, LANES), lambda g, t: (0, 0))
    st = pl.BlockSpec((RWKV_HEAD, RWKV_HEAD, LANES), lambda g, t: (0, 0, g))
    y, s_new = pl.pallas_call(
        functools.partial(_rwkv_scan_kernel, tt=tt),
        grid=(nc // LANES, nb),
        in_specs=[seq] * 4 + [flat] + [tile] * 5 + [st],
        out_specs=[flat, st],
        out_shape=[jax.ShapeDtypeStruct((nb * RWKV_HEAD * tt, nc), F32),
                   jax.ShapeDtypeStruct((RWKV_HEAD, RWKV_HEAD, nc), F32)],
        scratch_shapes=[pltpu.VMEM((RWKV_HEAD, tt, LANES), F32)] * 3 + [pltpu.VMEM((tt, LANES), F32)],
        compiler_params=_cparams(("parallel", "arbitrary")),
        name="rwkv_scan",
    )(r, w, k, a, v.reshape(nb * RWKV_HEAD * tt, nc), *tiles, s0)
    return y.reshape(nb, RWKV_HEAD, tt, nc), s_new


RELAYOUT_PITCH = LANES + SUBLANES
RELAYOUT_STEPS = LANES
SCAN_STEPS = 32


def _to_rows_kernel(x_ref, o_ref, z_scr):
    nseq = x_ref.shape[0]
    pairs = RWKV_HEADS // 2
    for b in range(nseq):
        for hp in range(pairs):
            q = b * pairs + hp
            z_scr[q * RELAYOUT_PITCH:q * RELAYOUT_PITCH + LANES, :] = x_ref[b, :, hp * LANES:(hp + 1) * LANES].T
    half = nseq * pairs
    for j in range(RWKV_HEAD):
        m = jnp.concatenate([z_scr[pl.ds(j, half, stride=RELAYOUT_PITCH), :],
                             z_scr[pl.ds(RWKV_HEAD + j, half, stride=RELAYOUT_PITCH), :]], axis=0)
        mt = m.T
        for tq in range(RELAYOUT_STEPS // SCAN_STEPS):
            o_ref[tq, j] = mt[tq * SCAN_STEPS:(tq + 1) * SCAN_STEPS]


def _to_rows(x3):
    bsz, t_len, _ = x3.shape
    per = RELAYOUT_STEPS // SCAN_STEPS
    return pl.pallas_call(
        _to_rows_kernel,
        grid=(t_len // RELAYOUT_STEPS,),
        in_specs=[pl.BlockSpec((bsz, RELAYOUT_STEPS, RWKV_WIDTH), lambda t: (0, t, 0))],
        out_specs=pl.BlockSpec((per, RWKV_HEAD, SCAN_STEPS, LANES), lambda t: (t, 0, 0, 0)),
        out_shape=jax.ShapeDtypeStruct((t_len // SCAN_STEPS, RWKV_HEAD, SCAN_STEPS, LANES), F32),
        scratch_shapes=[pltpu.VMEM((bsz * (RWKV_HEADS // 2) * RELAYOUT_PITCH, LANES), F32)],
        compiler_params=_cparams(("arbitrary",)),
        name="to_rows",
    )(x3)


def _from_rows_kernel(y_ref, o_ref, z_scr):
    nseq = o_ref.shape[0]
    pairs = RWKV_HEADS // 2
    half = nseq * pairs
    for i in range(RWKV_HEAD):
        m = jnp.concatenate([y_ref[tq, i] for tq in range(RELAYOUT_STEPS // SCAN_STEPS)], axis=0)
        mt = m.T
        z_scr[pl.ds(i, half, stride=RELAYOUT_PITCH), :] = mt[:half]
        z_scr[pl.ds(RWKV_HEAD + i, half, stride=RELAYOUT_PITCH), :] = mt[half:]
    for b in range(nseq):
        for hp in range(pairs):
            q = b * pairs + hp
            o_ref[b, :, hp * LANES:(hp + 1) * LANES] = z_scr[q * RELAYOUT_PITCH:q * RELAYOUT_PITCH + LANES, :].T


def _from_rows(y4, bsz):
    t_len = y4.shape[0] * SCAN_STEPS
    per = RELAYOUT_STEPS // SCAN_STEPS
    return pl.pallas_call(
        _from_rows_kernel,
        grid=(t_len // RELAYOUT_STEPS,),
        in_specs=[pl.BlockSpec((per, RWKV_HEAD, SCAN_STEPS, LANES), lambda t: (t, 0, 0, 0))],
        out_specs=pl.BlockSpec((bsz, RELAYOUT_STEPS, RWKV_WIDTH), lambda t: (0, t, 0)),
        out_shape=jax.ShapeDtypeStruct((bsz, t_len, RWKV_WIDTH), F32),
        scratch_shapes=[pltpu.VMEM((bsz * (RWKV_HEADS // 2) * RELAYOUT_PITCH, LANES), F32)],
        compiler_params=_cparams(("arbitrary",)),
        name="from_rows",
    )(y4)


def _rwkv_recurrence(seqs, s_rwkv, w, bsz, t_len):
    pairs = RWKV_HEADS // 2
    if bsz * RWKV_HEADS == LANES and t_len % RELAYOUT_STEPS == 0:
        rows = [_to_rows(t.reshape(bsz, t_len, RWKV_WIDTH)) for t in seqs]
        s0 = s_rwkv.reshape(bsz, pairs, 2, RWKV_HEAD, RWKV_HEAD).transpose(4, 3, 2, 0, 1)
        y4, s_c = _rwkv_scan(*rows, w["rwkv_tiles_paired"], s0.reshape(RWKV_HEAD, RWKV_HEAD, LANES), SCAN_STEPS)
        y = _from_rows(y4, bsz).reshape(bsz * t_len, RWKV_WIDTH)
        s_new = s_c.reshape(RWKV_HEAD, RWKV_HEAD, 2, bsz, pairs).transpose(3, 4, 2, 1, 0)
        return y, s_new.reshape(bsz, RWKV_HEADS, RWKV_HEAD, RWKV_HEAD)
    tt = min(t_len, SCAN_STEPS)
    nb = t_len // tt
    nc = bsz * RWKV_HEADS
    rows = [t.reshape(bsz, nb, tt, RWKV_HEADS, RWKV_HEAD).transpose(1, 4, 2, 0, 3).reshape(nb, RWKV_HEAD, tt, nc)
            for t in seqs]
    s0 = s_rwkv.transpose(3, 2, 0, 1).reshape(RWKV_HEAD, RWKV_HEAD, nc)
    y4, s_c = _rwkv_scan(*rows, w["rwkv_tiles"], s0, tt)
    y = y4.reshape(nb, RWKV_HEAD, tt, bsz, RWKV_HEADS).transpose(3, 0, 2, 4, 1).reshape(bsz * t_len, RWKV_WIDTH)
    return y, s_c.reshape(RWKV_HEAD, RWKV_HEAD, bsz, RWKV_HEADS).transpose(2, 3, 1, 0)


def _chain_tile(p):
    t = p.reshape(RWKV_HEADS, RWKV_HEAD).T
    return jnp.tile(t, (1, LANES // RWKV_HEADS))


def _chain_tile_paired(p):
    pairs = RWKV_HEADS // 2
    t = p.reshape(pairs, 2, RWKV_HEAD).transpose(2, 1, 0)[:, :, None, :]
    return jnp.broadcast_to(t, (RWKV_HEAD, 2, LANES // RWKV_HEADS, pairs)).reshape(RWKV_HEAD, LANES)


def _lru_kernel(gate_ref, x_ref, cinit_ref, cw_ref, cb_ref, wri_ref, br_ref, bi_ref, lam_ref, h0_ref,
                o_ref, hfin_ref, halo, hprev, *, tt, bb):
    j = pl.program_id(1)

    @pl.when(j == 0)
    def _():
        halo[...] = cinit_ref[...]
        for i in range(bb):
            hprev[i] = jnp.broadcast_to(h0_ref[i], (SUBLANES, LRU_WIDTH))

    rows = _row_iota((tt, LRU_BLOCK))
    for i in range(bb):
        u = x_ref[i]
        ext = jnp.concatenate([halo[i], u], axis=0)
        xc = cb_ref[...] + _shifted(ext, 3, tt) * cw_ref[0:1, :]
        xc = xc + _shifted(ext, 2, tt) * cw_ref[1:2, :]
        xc = xc + _shifted(ext, 1, tt) * cw_ref[2:3, :]
        xc = xc + u * cw_ref[3:4, :]
        halo[i] = u[tt - SUBLANES:tt, :]

        for n in range(LRU_BLOCKS):
            cs = slice(n * LRU_BLOCK, (n + 1) * LRU_BLOCK)
            xb = xc[:, cs]
            ri = _dot(xb.astype(BF16), wri_ref[n])
            rg = jax.nn.sigmoid(ri[:, :LRU_BLOCK] + br_ref[:, cs])
            ig = jax.nn.sigmoid(ri[:, LRU_BLOCK:] + bi_ref[:, cs])
            log_a = -LRU_C * rg * _softplus(-lam_ref[:, cs])
            a = jnp.exp(log_a)
            b = jnp.sqrt(-jnp.tanh(log_a) * (a * a + 1.0)) * ig * xb
            d = 1
            while d < tt:
                keep = rows >= d
                a_sh = jnp.where(keep, pltpu.roll(a, d, axis=0), 1.0)
                b_sh = jnp.where(keep, pltpu.roll(b, d, axis=0), 0.0)
                b = b + a * b_sh
                a = a * a_sh
                d *= 2
            h = a * hprev[i, 0:1, cs] + b
            o_ref[i, :, cs] = (h * _gelu_tanh(gate_ref[i, :, cs])).astype(BF16)
            h_last = h[tt - 1:tt, :]
            hprev[i, :, cs] = jnp.broadcast_to(h_last, (SUBLANES, LRU_BLOCK))
            hfin_ref[i, :, cs] = h_last


def _lru(p1, cinit, cw, cb, wri, br, bi, lam, h0, bsz, t_len):
    tt = min(t_len, 64)
    nt = t_len // tt
    bb = _batch_block(bsz, t_len)
    p3 = p1.reshape(bsz, t_len, P1_COLS)
    rows = lambda w: (lambda b, j: (b, j, w))
    per_seq = lambda b, j: (b, 0, 0)
    const2 = lambda b, j: (0, 0)
    o, h_fin = pl.pallas_call(
        functools.partial(_lru_kernel, tt=tt, bb=bb),
        grid=(bsz // bb, nt),
        in_specs=[
            pl.BlockSpec((bb, tt, 1024), rows(0)),
            pl.BlockSpec((bb, tt, 1024), rows(1)),
            pl.BlockSpec((bb, SUBLANES, LRU_WIDTH), per_seq),
            pl.BlockSpec((SUBLANES, LRU_WIDTH), const2),
            pl.BlockSpec((1, LRU_WIDTH), const2),
            pl.BlockSpec((LRU_BLOCKS, LRU_BLOCK, 2 * LRU_BLOCK), lambda b, j: (0, 0, 0)),
            pl.BlockSpec((1, LRU_WIDTH), const2),
            pl.BlockSpec((1, LRU_WIDTH), const2),
            pl.BlockSpec((1, LRU_WIDTH), const2),
            pl.BlockSpec((bb, 1, LRU_WIDTH), per_seq),
        ],
        out_specs=[pl.BlockSpec((bb, tt, LRU_WIDTH), rows(0)),
                   pl.BlockSpec((bb, 1, LRU_WIDTH), per_seq)],
        out_shape=[jax.ShapeDtypeStruct((bsz, t_len, LRU_WIDTH), BF16),
                   jax.ShapeDtypeStruct((bsz, 1, LRU_WIDTH), F32)],
        scratch_shapes=[pltpu.VMEM((bb, SUBLANES, LRU_WIDTH), F32), pltpu.VMEM((bb, SUBLANES, LRU_WIDTH), F32)],
        compiler_params=_cparams(("parallel", "arbitrary")),
        name="lru",
    )(p3, p3, cinit, cw, cb, wri, br, bi, lam, h0)
    return o.reshape(bsz * t_len, LRU_WIDTH), h_fin


def _expand_heads(x, e_ref, terms):
    parts = []
    rest = x
    for _ in range(terms):
        part = rest.astype(BF16).astype(F32)
        parts.append(part)
        rest = rest - part
    rows = x.shape[0]
    out = _dot(jnp.concatenate(parts, axis=0), e_ref[...])
    acc = out[:rows]
    for t in range(1, terms):
        acc = acc + out[t * rows:(t + 1) * rows]
    return acc


def _ssd_kernel(z_ref, xbc_ref, dt_ref, cinit_ref, cw_ref, cb_ref, dtb_ref, alog_ref, dd_ref, nw_ref,
                ej_ref, ep_ref, s0_ref, o_ref, s_ref, halo, st_t, *, c, mm, bb):
    j = pl.program_id(1)
    hg = SSD_HEADS // SSD_GROUPS
    gw = SSD_INNER // SSD_GROUPS
    pair = 2 * SSD_HEADDIM

    @pl.when(j == 0)
    def _():
        halo[...] = cinit_ref[...]
        for i in range(bb):
            for q in range(SSD_HEADS // 2):
                two = jnp.concatenate([s0_ref[i, 2 * q], s0_ref[i, 2 * q + 1]], axis=0)
                st_t[i, :, q * pair:(q + 1) * pair] = two.T

    wj = SSD_HEADS * c
    row_t = _row_iota((c, wj))
    lane_t = lax.broadcasted_iota(jnp.int32, (c, wj), 1) & (c - 1)
    causal = lane_t <= row_t
    diag = lane_t == row_t
    bd_rows = hg * c
    bd_mask = ((_row_iota((bd_rows, gw)) >> (c.bit_length() - 1))
               == (lax.broadcasted_iota(jnp.int32, (bd_rows, gw), 1) >> (SSD_HEADDIM.bit_length() - 1)))
    neg_a = -jnp.exp(alog_ref[...])
    gs = SSD_GROUPS * SSD_STATE
    for i in range(bb):
        u = xbc_ref[i]
        ext = jnp.concatenate([halo[i], u], axis=0)
        xc = cb_ref[...] + _shifted(ext, 3, c) * cw_ref[0:1, :]
        xc = xc + _shifted(ext, 2, c) * cw_ref[1:2, :]
        xc = xc + _shifted(ext, 1, c) * cw_ref[2:3, :]
        xc = _silu(xc + u * cw_ref[3:4, :])
        halo[i] = u[c - SUBLANES:c, :]

        xs = xc[:, :SSD_INNER]
        bm = xc[:, SSD_INNER:SSD_INNER + gs].astype(mm)
        cm = xc[:, SSD_INNER + gs:].astype(mm)
        dt = _softplus(dt_ref[i] + dtb_ref[...])
        cs = _cumsum_rows(dt * neg_a, c)
        cs_p = _expand_heads(cs, ep_ref, 3)
        dt_p = _expand_heads(dt, ep_ref, 2)
        cs_j = cs_p if c == SSD_HEADDIM else _expand_heads(cs, ej_ref, 3)
        cs_row = jnp.sum(jnp.where(diag, cs_j, 0.0), axis=0, keepdims=True)
        lmat = jnp.exp(jnp.where(causal, cs_j - cs_row, -jnp.inf))
        last_p = cs_p[c - 1:c, :]
        e_cs = jnp.exp(cs_p)
        e_last = jnp.exp(last_p)
        xdt = xs * dt_p
        xw = (xdt * jnp.exp(last_p - cs_p)).astype(mm)
        xdt = xdt.astype(mm)
        y_groups = []
        for g in range(SSD_GROUPS):
            gl = slice(g * gw, (g + 1) * gw)
            ss = slice(g * SSD_STATE, (g + 1) * SSD_STATE)
            cb_x = _dot_nt(cm[:, ss], jnp.concatenate([bm[:, ss]] * hg, axis=0))
            m_x = (cb_x * lmat[:, g * bd_rows:(g + 1) * bd_rows]).astype(mm)
            blocks = jnp.where(bd_mask, jnp.concatenate([xdt[:, gl]] * hg, axis=0), jnp.zeros((), mm))
            st = st_t[i, :, gl]
            y_groups.append(_dot(m_x, blocks) + e_cs[:, gl] * _dot(cm[:, ss], st.astype(mm)))
            st_t[i, :, gl] = st * e_last[:, gl] + _dot_tn(bm[:, ss], xw[:, gl])
        y = (jnp.concatenate(y_groups, axis=1) + dd_ref[...] * xs) * _silu(z_ref[i])
        for g in range(SSD_GROUPS):
            gl = slice(g * gw, (g + 1) * gw)
            o_ref[i, :, gl] = _rms(y[:, gl], nw_ref[:, gl], 1e-5).astype(BF16)

    @pl.when(j == pl.num_programs(1) - 1)
    def _():
        for i in range(bb):
            for q in range(SSD_HEADS // 2):
                two = st_t[i, :, q * pair:(q + 1) * pair].T
                s_ref[i, 2 * q] = two[:SSD_HEADDIM]
                s_ref[i, 2 * q + 1] = two[SSD_HEADDIM:]


def _ssd(p1, cinit, cw, cb, dtb, alog, dd, nw, s0, bsz, t_len):
    c = min(SSD_CHUNK, t_len)
    nt = t_len // c
    bb = _batch_block(bsz, t_len)
    mm = BF16 if c % 16 == 0 else F32
    p3 = p1.reshape(bsz, t_len, P1_COLS)
    rows = lambda w: (lambda b, j: (b, j, w))
    const2 = lambda b, j: (0, 0)
    st_spec = pl.BlockSpec((bb, SSD_HEADS, SSD_HEADDIM, SSD_STATE), lambda b, j: (b, 0, 0, 0))
    head = jnp.arange(LANES, dtype=jnp.int32)[:, None]
    expand = lambda width: (jnp.arange(SSD_HEADS * width, dtype=jnp.int32)[None, :] // width == head).astype(F32)
    o, s_new = pl.pallas_call(
        functools.partial(_ssd_kernel, c=c, mm=mm, bb=bb),
        grid=(bsz // bb, nt),
        in_specs=[
            pl.BlockSpec((bb, c, 1024), rows(2)),
            pl.BlockSpec((bb, c, SSD_CONV_CH), rows(2)),
            pl.BlockSpec((bb, c, LANES), rows(36)),
            pl.BlockSpec((bb, SUBLANES, SSD_CONV_CH), lambda b, j: (b, 0, 0)),
            pl.BlockSpec((SUBLANES, SSD_CONV_CH), const2),
            pl.BlockSpec((1, SSD_CONV_CH), const2),
            pl.BlockSpec((1, LANES), const2),
            pl.BlockSpec((1, LANES), const2),
            pl.BlockSpec((1, SSD_INNER), const2),
            pl.BlockSpec((1, SSD_INNER), const2),
            pl.BlockSpec((LANES, SSD_HEADS * c), const2),
            pl.BlockSpec((LANES, SSD_INNER), const2),
            st_spec,
        ],
        out_specs=[pl.BlockSpec((bb, c, SSD_INNER), rows(0)), st_spec],
        out_shape=[jax.ShapeDtypeStruct((bsz, t_len, SSD_INNER), BF16),
                   jax.ShapeDtypeStruct((bsz, SSD_HEADS, SSD_HEADDIM, SSD_STATE), F32)],
        scratch_shapes=[pltpu.VMEM((bb, SUBLANES, SSD_CONV_CH), F32),
                        pltpu.VMEM((bb, SSD_STATE, SSD_INNER), F32)],
        compiler_params=_cparams(("parallel", "arbitrary")),
        name="ssd",
    )(p3, p3, p3, cinit, cw, cb, dtb, alog, dd, nw, expand(c), expand(SSD_HEADDIM), s0)
    return o.reshape(bsz * t_len, SSD_INNER), s_new


def _pad_rows_to_8(x):
    return jnp.pad(x, ((0, 0), (SUBLANES - x.shape[1], 0), (0, 0)))


def _pad_lanes(x, width):
    return jnp.pad(x, ((0, 0), (0, width - x.shape[1])))


def _trunk(x, states, w):
    s_gla, s_rwkv, s_shift, s_lru, s_lru_conv, s_ssd, s_ssd_conv = states
    bsz, t_len, d = x.shape
    n = bsz * t_len
    x2 = x.reshape(n, d)

    p0 = _in_proj(x2, w["g_mix0"], w["w_in0"], P0_TN)
    o_gla, s_gla_new = _gla(p0, s_gla, w["gla_wa2"], w["gla_ba"], w["gla_gn"], bsz, t_len)
    shift8 = jnp.broadcast_to(s_shift[:, None, :], (bsz, SUBLANES, RWKV_COLS)).reshape(bsz * SUBLANES, RWKV_COLS)
    r, wdec, k, v, a, g = _rwkv_prep(p0, shift8, w["rwkv_mu"], w["rwkv_w0"], w["rwkv_a0"], w["rwkv_wa"],
                                     w["rwkv_g2"], bsz, t_len)
    y_rwkv, s_rwkv_new = _rwkv_recurrence((r, wdec, k, a, v), s_rwkv, w, bsz, t_len)
    shift_new = p0.reshape(bsz, t_len, P0_COLS)[:, -1, 3072:3072 + RWKV_COLS]
    x2 = _out_ffn(x2, o_gla, y_rwkv, g, w["w_out0"], w["g_ffn0"], w["wg0"], w["wu0"], w["wd0"], None)

    p1 = _in_proj(x2, w["g_mix1"], w["w_in1"], P1_TN)
    lru_out, h_lru = _lru(p1, _pad_rows_to_8(s_lru_conv), w["lru_cw"], w["lru_cb"], w["lru_wri"], w["lru_br"],
                          w["lru_bi"], w["lru_lam"], s_lru.reshape(bsz, 1, LRU_WIDTH), bsz, t_len)
    y_ssd, s_ssd_new = _ssd(p1, _pad_rows_to_8(s_ssd_conv), w["ssd_cw"], w["ssd_cb"], w["ssd_dtb"], w["ssd_alog"],
                            w["ssd_dd"], w["ssd_nw"], s_ssd, bsz, t_len)
    keep = CONV_W - 1
    p1_tail = p1.reshape(bsz, t_len, P1_COLS)[:, -keep:]
    lru_conv_new = p1_tail[:, :, 1024:2048]
    ssd_conv_new = p1_tail[:, :, 3072:3072 + SSD_CONV_CH]
    y = _out_ffn(x2, lru_out, y_ssd, None, w["w_out1"], w["g_ffn1"], w["wg1"], w["wu1"], w["wd1"], w["g_final"])
    return y.reshape(bsz, t_len, d), (s_gla_new, s_rwkv_new, shift_new, h_lru.reshape(bsz, LRU_WIDTH),
                                      lru_conv_new, s_ssd_new, ssd_conv_new)


def kernel(x_prompt, x_sample, state_gla, state_rwkv, state_rwkv_shift, state_lru, state_lru_conv, state_ssd, state_ssd_conv, w_in0, gla_w_a2, gla_b_a, gla_g_norm, rwkv_mu, rwkv_w0, rwkv_w2, rwkv_a0, rwkv_a2, rwkv_g2, rwkv_k_k, rwkv_k_a, rwkv_r_k, rwkv_ln_w, rwkv_ln_b, w_out0, w_in1, lru_conv_w, lru_conv_b, lru_w_r, lru_b_r, lru_w_i, lru_b_i, lru_lambda, ssd_conv_w, ssd_conv_b, ssd_dt_bias, ssd_a_log, ssd_d, ssd_norm_w, w_out1, g_mix, g_ffn, w_ffn_gate, w_ffn_up, w_ffn_down, g_final):
    row = lambda p: p.reshape(1, -1).astype(F32)
    w0 = jnp.concatenate([w_in0[:, :2048], w_in0[:, 2064:3088], w_in0[:, 3088:],
                          _pad_lanes(w_in0[:, 2048:2064], LANES)], axis=1)
    zeros64 = jnp.zeros((64, RWKV_WIDTH), F32)
    rwkv_wa = jnp.concatenate([jnp.concatenate([rwkv_w2, zeros64], axis=1),
                               jnp.concatenate([zeros64, rwkv_a2], axis=1)], axis=0)
    w = {
        "w_in0": w0.astype(BF16),
        "w_in1": _pad_lanes(w_in1, P1_COLS).astype(BF16),
        "g_mix0": g_mix[0], "g_mix1": g_mix[1], "g_ffn0": g_ffn[0], "g_ffn1": g_ffn[1], "g_final": g_final,
        "gla_wa2": jnp.pad(gla_w_a2, ((0, LANES - GLA_GATE_RANK), (0, 0))).astype(BF16),
        "gla_ba": row(gla_b_a), "gla_gn": row(gla_g_norm),
        "rwkv_mu": row(rwkv_mu), "rwkv_w0": row(rwkv_w0), "rwkv_a0": row(rwkv_a0),
        "rwkv_wa": rwkv_wa.astype(BF16), "rwkv_g2": rwkv_g2.astype(BF16),
        "rwkv_tiles": [_chain_tile(p) for p in (rwkv_k_k, rwkv_k_a, rwkv_r_k, rwkv_ln_w, rwkv_ln_b)],
        "rwkv_tiles_paired": [_chain_tile_paired(p)
                              for p in (rwkv_k_k, rwkv_k_a, rwkv_r_k, rwkv_ln_w, rwkv_ln_b)],
        "w_out0": w_out0.astype(BF16), "w_out1": w_out1.astype(BF16),
        "wg0": w_ffn_gate[0].astype(BF16), "wu0": w_ffn_up[0].astype(BF16), "wd0": w_ffn_down[0].astype(BF16),
        "wg1": w_ffn_gate[1].astype(BF16), "wu1": w_ffn_up[1].astype(BF16), "wd1": w_ffn_down[1].astype(BF16),
        "lru_cw": jnp.pad(lru_conv_w, ((0, SUBLANES - CONV_W), (0, 0))), "lru_cb": row(lru_conv_b),
        "lru_wri": jnp.concatenate([lru_w_r, lru_w_i], axis=2).astype(BF16),
        "lru_br": row(lru_b_r), "lru_bi": row(lru_b_i), "lru_lam": row(lru_lambda),
        "ssd_cw": jnp.pad(ssd_conv_w, ((0, SUBLANES - CONV_W), (0, 0))), "ssd_cb": row(ssd_conv_b),
        "ssd_dtb": _pad_lanes(row(ssd_dt_bias), LANES), "ssd_alog": _pad_lanes(row(ssd_a_log), LANES),
        "ssd_dd": row(jnp.repeat(ssd_d, SSD_HEADDIM)), "ssd_nw": row(ssd_norm_w),
    }
    bp = x_prompt.shape[0]
    prompt_init = (
        jnp.zeros((bp, GLA_HEADS, GLA_DK, GLA_DV), F32),
        jnp.zeros((bp, RWKV_HEADS, RWKV_HEAD, RWKV_HEAD), F32),
        jnp.zeros((bp, RWKV_COLS), F32),
        jnp.zeros((bp, LRU_WIDTH), F32),
        jnp.zeros((bp, CONV_W - 1, LRU_WIDTH), F32),
        jnp.zeros((bp, SSD_HEADS, SSD_HEADDIM, SSD_STATE), F32),
        jnp.zeros((bp, CONV_W - 1, SSD_CONV_CH), F32),
    )
    sample_init = (state_gla, state_rwkv, state_rwkv_shift, state_lru, state_lru_conv, state_ssd, state_ssd_conv)
    y_prompt, p_states = _trunk(x_prompt, prompt_init, w)
    y_sample, s_states = _trunk(x_sample, sample_init, w)
    return (y_prompt, y_sample, *p_states, *s_states)
```

```python
import functools

import jax
import jax.numpy as jnp
from jax import lax
from jax.experimental import pallas as pl
from jax.experimental.pallas import tpu as pltpu

F32 = jnp.float32
BF16 = jnp.bfloat16

D_MODEL = 1024
NORM_EPS = 1e-6
GLA_HEADS = 4
GLA_DK = 128
GLA_DV = 256
GLA_GATE_RANK = 16
GLA_GATE_NORM = 16.0
GLA_CHUNK = 64
RWKV_HEAD = 64
RWKV_HEADS = 16
RWKV_WIDTH = 1024
RWKV_LORA = 256
RWKV_COLS = 3 * RWKV_WIDTH + RWKV_LORA
RWKV_GN_EPS = 64e-5
LRU_WIDTH = 1024
LRU_BLOCKS = 8
LRU_BLOCK = 128
LRU_C = 8.0
CONV_W = 4
SSD_INNER = 1024
SSD_HEADDIM = 64
SSD_HEADS = 16
SSD_GROUPS = 2
SSD_STATE = 128
SSD_CHUNK = 64
SSD_CONV_CH = SSD_INNER + 2 * SSD_GROUPS * SSD_STATE
D_FF = 2816

LANES = 128
SUBLANES = 8
VMEM_LIMIT_BYTES = 56 * 1024 * 1024

P0_COLS = 6528
P0_TN = 2176
P1_COLS = 4864
P1_TN = 2432


def _cparams(semantics):
    return pltpu.CompilerParams(dimension_semantics=semantics, vmem_limit_bytes=VMEM_LIMIT_BYTES)


def _dot(a, b):
    return jnp.dot(a, b, preferred_element_type=F32)


def _dot_nt(a, b):
    return lax.dot_general(a, b, (((1,), (1,)), ((), ())), preferred_element_type=F32)


def _dot_tn(a, b):
    return lax.dot_general(a, b, (((0,), (0,)), ((), ())), preferred_element_type=F32)


def _softplus(x):
    return jnp.maximum(x, 0.0) + jnp.log1p(jnp.exp(-jnp.abs(x)))


def _silu(x):
    return x * jax.nn.sigmoid(x)


def _gelu_tanh(x):
    return 0.5 * x * (1.0 + jnp.tanh(0.7978845608028654 * (x + 0.044715 * (x * x * x))))


def _rms(x, g, eps):
    return x * lax.rsqrt(jnp.mean(x * x, axis=-1, keepdims=True) + eps) * g


def _row_iota(shape):
    return lax.broadcasted_iota(jnp.int32, shape, 0)


def _cumsum_rows(x, seg):
    rows = _row_iota(x.shape) & (seg - 1)
    d = 1
    while d < seg:
        x = x + jnp.where(rows >= d, pltpu.roll(x, d, axis=0), 0.0)
        d *= 2
    return x


def _shifted(ext, k, rows):
    return pltpu.roll(ext, k, axis=0)[SUBLANES:SUBLANES + rows]


def _in_proj_kernel(x_ref, g_ref, w_ref, o_ref, h_scr):
    @pl.when(pl.program_id(1) == 0)
    def _():
        h_scr[...] = _rms(x_ref[...], g_ref[...], NORM_EPS).astype(BF16)

    o_ref[...] = _dot(h_scr[...], w_ref[...])


def _in_proj(x2d, g, w, tn):
    n, d = x2d.shape
    cols = w.shape[1]
    tm = min(n, 1024)
    return pl.pallas_call(
        _in_proj_kernel,
        grid=(n // tm, cols // tn),
        in_specs=[
            pl.BlockSpec((tm, d), lambda i, j: (i, 0)),
            pl.BlockSpec((1, d), lambda i, j: (0, 0)),
            pl.BlockSpec((d, tn), lambda i, j: (0, j)),
        ],
        out_specs=pl.BlockSpec((tm, tn), lambda i, j: (i, j)),
        out_shape=jax.ShapeDtypeStruct((n, cols), F32),
        scratch_shapes=[pltpu.VMEM((tm, d), BF16)],
        compiler_params=_cparams(("parallel", "arbitrary")),
        name="in_proj",
    )(x2d, g.reshape(1, d), w)


def _out_ffn_kernel(*refs, has_gate, final_norm):
    refs = list(refs)
    x_ref, oa_ref, ob_ref = refs[:3]
    pos = 3
    gate_ref = None
    if has_gate:
        gate_ref = refs[pos]
        pos += 1
    woa_ref, wob_ref, gffn_ref, wg_ref, wu_ref, wd_ref = refs[pos:pos + 6]
    pos += 6
    gfin_ref = None
    if final_norm:
        gfin_ref = refs[pos]
        pos += 1
    out_ref, x1_scr, h_scr, acc_scr = refs[pos:pos + 4]

    k = pl.program_id(1)

    @pl.when(k == 0)
    def _():
        ob = ob_ref[...]
        if has_gate:
            ob = (ob * gate_ref[...]).astype(BF16)
        x1 = x_ref[...] + (_dot(oa_ref[...], woa_ref[...]) + _dot(ob, wob_ref[...]))
        x1_scr[...] = x1
        h_scr[...] = _rms(x1, gffn_ref[...], NORM_EPS).astype(BF16)
        acc_scr[...] = jnp.zeros_like(acc_scr)

    h = h_scr[...]
    act = (_silu(_dot(h, wg_ref[...])) * _dot(h, wu_ref[...])).astype(BF16)
    acc_scr[...] += _dot(act, wd_ref[...])

    @pl.when(k == pl.num_programs(1) - 1)
    def _():
        y = x1_scr[...] + acc_scr[...]
        if final_norm:
            y = _rms(y, gfin_ref[...], NORM_EPS)
        out_ref[...] = y


def _out_ffn(x2d, oa, ob, gate, wo, g_ffn, wg, wu, wd, g_final):
    n, d = x2d.shape
    tm = min(n, 512)
    tf = D_FF // 2
    has_gate = gate is not None
    final_norm = g_final is not None
    row = lambda i, k: (i, 0)
    const = lambda i, k: (0, 0)
    args = [x2d, oa, ob]
    in_specs = [pl.BlockSpec((tm, d), row)] * 3
    if has_gate:
        args.append(gate)
        in_specs.append(pl.BlockSpec((tm, d), row))
    half = wo.shape[0] // 2
    args += [wo[:half], wo[half:], g_ffn.reshape(1, d), wg, wu, wd]
    in_specs += [
        pl.BlockSpec((half, d), const),
        pl.BlockSpec((half, d), const),
        pl.BlockSpec((1, d), const),
        pl.BlockSpec((d, tf), lambda i, k: (0, k)),
        pl.BlockSpec((d, tf), lambda i, k: (0, k)),
        pl.BlockSpec((tf, d), lambda i, k: (k, 0)),
    ]
    if final_norm:
        args.append(g_final.reshape(1, d))
        in_specs.append(pl.BlockSpec((1, d), const))
    return pl.pallas_call(
        functools.partial(_out_ffn_kernel, has_gate=has_gate, final_norm=final_norm),
        grid=(n // tm, D_FF // tf),
        in_specs=in_specs,
        out_specs=pl.BlockSpec((tm, d), row),
        out_shape=jax.ShapeDtypeStruct((n, d), F32),
        scratch_shapes=[pltpu.VMEM((tm, d), F32), pltpu.VMEM((tm, d), BF16), pltpu.VMEM((tm, d), F32)],
        compiler_params=_cparams(("parallel", "arbitrary")),
        name="out_ffn",
    )(*args)


def _gla_kernel(q_ref, k_ref, v_ref, og_ref, al_ref, wa2_ref, ba_ref, gn_ref, s0_ref,
                o_ref, sout_ref, st_scr, *, c, mm, bb):
    n = pl.program_id(1)

    @pl.when(n == 0)
    def _():
        for i in range(bb):
            for h in range(GLA_HEADS):
                st_scr[i * GLA_HEADS + h] = s0_ref[i, h].T

    mask = _row_iota((c, c)) >= lax.broadcasted_iota(jnp.int32, (c, c), 1)
    gn = gn_ref[...]
    for i in range(bb):
        la_all = _dot(al_ref[i].astype(BF16), wa2_ref[...]) + ba_ref[...]
        la_all = -_softplus(-la_all) * (1.0 / GLA_GATE_NORM)
        b_all = _cumsum_rows(la_all, c)
        for h in range(GLA_HEADS):
            ks = slice(h * GLA_DK, (h + 1) * GLA_DK)
            vs = slice(h * GLA_DV, (h + 1) * GLA_DV)
            b = b_all[:, ks]
            b_last = b[c - 1:c, :]
            q = q_ref[i, :, ks] * (GLA_DK ** -0.5)
            k = k_ref[i, :, ks]
            v = v_ref[i, :, vs].astype(mm)
            qd = (q * jnp.exp(b)).astype(mm)
            kd = (k * jnp.exp(-b)).astype(mm)
            kc = (k * jnp.exp(b_last - b)).astype(mm)
            scores = jnp.where(mask, _dot_nt(qd, kd), 0.0)
            st = st_scr[i * GLA_HEADS + h]
            o = _dot(scores.astype(mm), v) + _dot_nt(qd, st.astype(mm))
            st_scr[i * GLA_HEADS + h] = st * jnp.exp(b_last) + _dot_tn(v, kc)
            o = _rms(o, gn, 1e-5)
            o_ref[i, :, vs] = (o * _silu(og_ref[i, :, vs])).astype(BF16)

    @pl.when(n == pl.num_programs(1) - 1)
    def _():
        for i in range(bb):
            for h in range(GLA_HEADS):
                sout_ref[i, h] = st_scr[i * GLA_HEADS + h].T


def _batch_block(bsz, t_len):
    return 2 if t_len >= 64 else min(bsz, 8)


def _gla(p0, s0, wa2, ba, gn, bsz, t_len):
    c = min(GLA_CHUNK, t_len)
    nt = t_len // c
    bb = _batch_block(bsz, t_len)
    mm = BF16 if c % 16 == 0 else F32
    p3 = p0.reshape(bsz, t_len, P0_COLS)
    rows = lambda w: (lambda b, j: (b, j, w))
    const = lambda b, j: (0, 0)
    st_spec = pl.BlockSpec((bb, GLA_HEADS, GLA_DK, GLA_DV), lambda b, j: (b, 0, 0, 0))
    o, s_new = pl.pallas_call(
        functools.partial(_gla_kernel, c=c, mm=mm, bb=bb),
        grid=(bsz // bb, nt),
        in_specs=[
            pl.BlockSpec((bb, c, 512), rows(0)),
            pl.BlockSpec((bb, c, 512), rows(1)),
            pl.BlockSpec((bb, c, 1024), rows(1)),
            pl.BlockSpec((bb, c, 1024), rows(2)),
            pl.BlockSpec((bb, c, LANES), rows(50)),
            pl.BlockSpec((LANES, 512), const),
            pl.BlockSpec((1, 512), const),
            pl.BlockSpec((1, GLA_DV), const),
            st_spec,
        ],
        out_specs=[pl.BlockSpec((bb, c, 1024), rows(0)), st_spec],
        out_shape=[jax.ShapeDtypeStruct((bsz, t_len, 1024), BF16),
                   jax.ShapeDtypeStruct((bsz, GLA_HEADS, GLA_DK, GLA_DV), F32)],
        scratch_shapes=[pltpu.VMEM((bb * GLA_HEADS, GLA_DV, GLA_DK), F32)],
        compiler_params=_cparams(("parallel", "arbitrary")),
        name="gla",
    )(p3, p3, p3, p3, p3, wa2, ba, gn, s0)
    return o.reshape(bsz * t_len, 1024), s_new


def _rwkv_prep_kernel(r_ref, k_ref, v_ref, l_ref, ir_ref, ik_ref, iv_ref, il_ref,
                      mu_ref, w0_ref, a0_ref, wa_ref, g2_ref,
                      ro_ref, wo_ref, ko_ref, vo_ref, ao_ref, go_ref, carry,
                      *, tm, t_len, multi_seq):
    j = pl.program_id(1)
    cols = ((0, 1024), (1024, 2048), (2048, 3072), (3072, RWKV_COLS))

    if not multi_seq:
        @pl.when(j == 0)
        def _():
            for (lo, hi), iref in zip(cols, (ir_ref, ik_ref, iv_ref, il_ref)):
                carry[:, lo:hi] = iref[...]

    def mixed(x_ref, i_ref, lo, hi):
        x = x_ref[...]
        rolled = pltpu.roll(x, 1, axis=0)
        rows = _row_iota(x.shape)
        if multi_seq:
            prev = jnp.where((rows & (t_len - 1)) == 0, i_ref[...], rolled)
        else:
            prev = jnp.where(rows == 0, carry[SUBLANES - 1:SUBLANES, lo:hi], rolled)
            carry[:, lo:hi] = x[tm - SUBLANES:tm, :]
        return x + (prev - x) * mu_ref[:, lo:hi]

    r = mixed(r_ref, ir_ref, *cols[0])
    k = mixed(k_ref, ik_ref, *cols[1])
    v = mixed(v_ref, iv_ref, *cols[2])
    lo_rank = mixed(l_ref, il_ref, *cols[3])

    wa_in = lo_rank[:, :LANES]
    lane = lax.broadcasted_iota(jnp.int32, wa_in.shape, 1)
    wa_in = jnp.where(lane < 64, jnp.tanh(wa_in), wa_in).astype(BF16)
    wa = _dot(wa_in, wa_ref[...])
    g = _dot(jax.nn.sigmoid(lo_rank[:, LANES:]).astype(BF16), g2_ref[...])
    w = -_softplus(-(w0_ref[...] + wa[:, :RWKV_WIDTH])) - 0.5
    ro_ref[...] = r
    wo_ref[...] = -jnp.exp(w)
    ko_ref[...] = k
    vo_ref[...] = v
    ao_ref[...] = jax.nn.sigmoid(a0_ref[...] + wa[:, RWKV_WIDTH:])
    go_ref[...] = g


def _rwkv_prep(p0, shift8, mu, w0, a0, wa, g2, bsz, t_len):
    n = bsz * t_len
    multi_seq = t_len < 256
    tm = min(n, 512) if multi_seq else 256
    nt = 1 if multi_seq else t_len // tm
    nb = n // tm if multi_seq else bsz
    ti = tm if multi_seq else SUBLANES
    rows = lambda w: (lambda i, j: (i * nt + j, w))
    init = lambda w: (lambda i, j: (i, w))
    const = lambda i, j: (0, 0)
    out = jax.ShapeDtypeStruct((n, RWKV_WIDTH), F32)
    return pl.pallas_call(
        functools.partial(_rwkv_prep_kernel, tm=tm, t_len=t_len, multi_seq=multi_seq),
        grid=(nb, nt),
        in_specs=[
            pl.BlockSpec((tm, 1024), rows(3)),
            pl.BlockSpec((tm, 1024), rows(4)),
            pl.BlockSpec((tm, 1024), rows(5)),
            pl.BlockSpec((tm, RWKV_LORA), rows(24)),
            pl.BlockSpec((ti, 1024), init(0)),
            pl.BlockSpec((ti, 1024), init(1)),
            pl.BlockSpec((ti, 1024), init(2)),
            pl.BlockSpec((ti, RWKV_LORA), init(12)),
            pl.BlockSpec((1, RWKV_COLS), const),
            pl.BlockSpec((1, RWKV_WIDTH), const),
            pl.BlockSpec((1, RWKV_WIDTH), const),
            pl.BlockSpec((LANES, 2 * RWKV_WIDTH), const),
            pl.BlockSpec((LANES, RWKV_WIDTH), const),
        ],
        out_specs=[pl.BlockSpec((tm, RWKV_WIDTH), rows(0))] * 6,
        out_shape=[out] * 6,
        scratch_shapes=[pltpu.VMEM((SUBLANES, RWKV_COLS), F32)],
        compiler_params=_cparams(("parallel", "arbitrary")),
        name="rwkv_prep",
    )(p0, p0, p0, p0, shift8, shift8, shift8, shift8, mu, w0, a0, wa, g2)


def _rwkv_scan_kernel(r_ref, lw_ref, k_ref, a_ref, v_ref, kk_ref, ka_ref, rk_ref, lnw_ref, lnb_ref,
                      s0_ref, y_ref, s_ref, kb_scr, kp_scr, rs_scr, kn_scr, bonus_scr, grow_scr, *, tt):
    @pl.when(pl.program_id(1) == 0)
    def _():
        s_ref[...] = s0_ref[...]

    def tile_row(ref, j):
        return ref[pl.ds(j, 1), :]

    def bcast(x):
        return jnp.broadcast_to(x, (RWKV_HEAD, LANES))

    norm2 = jnp.zeros((tt, LANES), F32)
    for j in range(RWKV_HEAD):
        kj = k_ref[0, j] * tile_row(kk_ref, j)
        norm2 = norm2 + kj * kj
    norm = jnp.maximum(jnp.sqrt(norm2), 1e-12)
    bonus = jnp.zeros((tt, LANES), F32)
    s_kk = jnp.zeros((RWKV_HEAD, LANES), F32)
    for j in range(RWKV_HEAD):
        k = k_ref[0, j]
        a = a_ref[0, j]
        r = r_ref[0, j]
        kk = k * tile_row(kk_ref, j) / norm
        kp = k * (1.0 + (a - 1.0) * tile_row(ka_ref, j))
        log_g = _cumsum_rows(lw_ref[0, j], tt)
        grow = jnp.exp(log_g)
        shrink = jnp.exp(-log_g)
        kb_scr[j] = kk * a * shrink
        kp_scr[j] = kp * shrink
        rs_scr[j] = r * grow
        kn_scr[j] = pltpu.roll(kk, tt - 1, axis=0) * grow
        grow_scr[pl.ds(j, 1), :] = grow[tt - 1:tt, :]
        bonus = bonus + r * kp * tile_row(rk_ref, j)
        s_kk = s_kk + s_ref[j] * bcast(kk[0:1, :])
    bonus_scr[...] = bonus

    def row(ref, j, s):
        return bcast(ref[j, pl.ds(s, 1), :])

    def step(s, s_kk):
        sa = -s_kk
        v = v_ref[s]
        y = jnp.zeros_like(v)
        s_kk_next = jnp.zeros_like(v)
        for j in range(RWKV_HEAD):
            sn = s_ref[j] + sa * row(kb_scr, j, s) + v * row(kp_scr, j, s)
            s_ref[j] = sn
            y = y + sn * row(rs_scr, j, s)
            s_kk_next = s_kk_next + sn * row(kn_scr, j, s)
        mu = jnp.mean(y, axis=0, keepdims=True)
        yc = y - mu
        var = jnp.mean(yc * yc, axis=0, keepdims=True)
        y_ref[s] = yc * lax.rsqrt(var + RWKV_GN_EPS) * lnw_ref[...] + lnb_ref[...] + bonus_scr[pl.ds(s, 1), :] * v
        return s_kk_next

    lax.fori_loop(0, tt, step, s_kk)
    for j in range(RWKV_HEAD):
        s_ref[j] = s_ref[j] * bcast(tile_row(grow_scr, j))


def _rwkv_scan(r, lw, k, a, v, tiles, s0, tt):
    nb, _, _, nc = r.shape
    seq = pl.BlockSpec((1, RWKV_HEAD, tt, LANES), lambda g, t: (t, 0, 0, g))
    steps = pl.BlockSpec((tt, RWKV_HEAD, LANES), lambda g, t: (t, 0, g))
    tile = pl.BlockSpec((RWKV_HEAD, LANES), lambda g, t: (0, 0))
    st = pl.BlockSpec((RWKV_HEAD, RWKV_HEAD, LANES), lambda g, t: (0, 0, g))
    return pl.pallas_call(
        functools.partial(_rwkv_scan_kernel, tt=tt),
        grid=(nc // LANES, nb),
        in_specs=[seq] * 4 + [steps] + [tile] * 5 + [st],
        out_specs=[steps, st],
        out_shape=[jax.ShapeDtypeStruct((nb * tt, RWKV_HEAD, nc), F32),
                   jax.ShapeDtypeStruct((RWKV_HEAD, RWKV_HEAD, nc), F32)],
        scratch_shapes=[pltpu.VMEM((RWKV_HEAD, tt, LANES), F32)] * 4
        + [pltpu.VMEM((tt, LANES), F32), pltpu.VMEM((RWKV_HEAD, LANES), F32)],
        compiler_params=_cparams(("parallel", "arbitrary")),
        name="rwkv_scan",
    )(r, lw, k, a, v, *tiles, s0)


RELAYOUT_PITCH = LANES + SUBLANES
RELAYOUT_STEPS = LANES
SCAN_STEPS = 32


def _to_chains_kernel(x_ref, o_ref, z_scr, *, step_major):
    nseq = x_ref.shape[0]
    pairs = RWKV_HEADS // 2
    for b in range(nseq):
        for hp in range(pairs):
            q = b * pairs + hp
            z_scr[q * RELAYOUT_PITCH:q * RELAYOUT_PITCH + LANES, :] = x_ref[b, :, hp * LANES:(hp + 1) * LANES].T
    half = nseq * pairs
    for j in range(RWKV_HEAD):
        m = jnp.concatenate([z_scr[pl.ds(j, half, stride=RELAYOUT_PITCH), :],
                             z_scr[pl.ds(RWKV_HEAD + j, half, stride=RELAYOUT_PITCH), :]], axis=0)
        mt = m.T
        if step_major:
            o_ref[pl.ds(j, RELAYOUT_STEPS, stride=RWKV_HEAD), :] = mt
        else:
            for tq in range(RELAYOUT_STEPS // SCAN_STEPS):
                o_ref[tq, j] = mt[tq * SCAN_STEPS:(tq + 1) * SCAN_STEPS]


def _to_chains(x3, step_major):
    bsz, t_len, _ = x3.shape
    per = RELAYOUT_STEPS // SCAN_STEPS
    if step_major:
        out_spec = pl.BlockSpec((RELAYOUT_STEPS * RWKV_HEAD, LANES), lambda t: (t, 0))
        out_shape = jax.ShapeDtypeStruct((t_len * RWKV_HEAD, LANES), F32)
    else:
        out_spec = pl.BlockSpec((per, RWKV_HEAD, SCAN_STEPS, LANES), lambda t: (t, 0, 0, 0))
        out_shape = jax.ShapeDtypeStruct((t_len // SCAN_STEPS, RWKV_HEAD, SCAN_STEPS, LANES), F32)
    return pl.pallas_call(
        functools.partial(_to_chains_kernel, step_major=step_major),
        grid=(t_len // RELAYOUT_STEPS,),
        in_specs=[pl.BlockSpec((bsz, RELAYOUT_STEPS, RWKV_WIDTH), lambda t: (0, t, 0))],
        out_specs=out_spec,
        out_shape=out_shape,
        scratch_shapes=[pltpu.VMEM((bsz * (RWKV_HEADS // 2) * RELAYOUT_PITCH, LANES), F32)],
        compiler_params=_cparams(("arbitrary",)),
        name="to_chains",
    )(x3)


def _from_chains_kernel(y_ref, o_ref, z_scr):
    nseq = o_ref.shape[0]
    pairs = RWKV_HEADS // 2
    half = nseq * pairs
    for i in range(RWKV_HEAD):
        mt = y_ref[pl.ds(i, RELAYOUT_STEPS, stride=RWKV_HEAD), :].T
        z_scr[pl.ds(i, half, stride=RELAYOUT_PITCH), :] = mt[:half]
        z_scr[pl.ds(RWKV_HEAD + i, half, stride=RELAYOUT_PITCH), :] = mt[half:]
    for b in range(nseq):
        for hp in range(pairs):
            q = b * pairs + hp
            o_ref[b, :, hp * LANES:(hp + 1) * LANES] = z_scr[q * RELAYOUT_PITCH:q * RELAYOUT_PITCH + LANES, :].T


def _from_chains(y2, bsz):
    t_len = y2.shape[0] // RWKV_HEAD
    return pl.pallas_call(
        _from_chains_kernel,
        grid=(t_len // RELAYOUT_STEPS,),
        in_specs=[pl.BlockSpec((RELAYOUT_STEPS * RWKV_HEAD, LANES), lambda t: (t, 0))],
        out_specs=pl.BlockSpec((bsz, RELAYOUT_STEPS, RWKV_WIDTH), lambda t: (0, t, 0)),
        out_shape=jax.ShapeDtypeStruct((bsz, t_len, RWKV_WIDTH), F32),
        scratch_shapes=[pltpu.VMEM((bsz * (RWKV_HEADS // 2) * RELAYOUT_PITCH, LANES), F32)],
        compiler_params=_cparams(("arbitrary",)),
        name="from_chains",
    )(y2)


def _rwkv_recurrence(seqs, s_rwkv, w, bsz, t_len):
    pairs = RWKV_HEADS // 2
    *row_seqs, v = seqs
    if bsz * RWKV_HEADS == LANES and t_len % RELAYOUT_STEPS == 0:
        rows = [_to_chains(t.reshape(bsz, t_len, RWKV_WIDTH), False) for t in row_seqs]
        v_c = _to_chains(v.reshape(bsz, t_len, RWKV_WIDTH), True).reshape(t_len, RWKV_HEAD, LANES)
        s0 = s_rwkv.reshape(bsz, pairs, 2, RWKV_HEAD, RWKV_HEAD).transpose(4, 3, 2, 0, 1)
        y_c, s_c = _rwkv_scan(*rows, v_c, w["rwkv_tiles_paired"], s0.reshape(RWKV_HEAD, RWKV_HEAD, LANES),
                              SCAN_STEPS)
        y = _from_chains(y_c.reshape(t_len * RWKV_HEAD, LANES), bsz).reshape(bsz * t_len, RWKV_WIDTH)
        s_new = s_c.reshape(RWKV_HEAD, RWKV_HEAD, 2, bsz, pairs).transpose(3, 4, 2, 1, 0)
        return y, s_new.reshape(bsz, RWKV_HEADS, RWKV_HEAD, RWKV_HEAD)
    tt = min(t_len, SCAN_STEPS)
    nb = t_len // tt
    nc = bsz * RWKV_HEADS
    rows = [t.reshape(bsz, nb, tt, RWKV_HEADS, RWKV_HEAD).transpose(1, 4, 2, 0, 3).reshape(nb, RWKV_HEAD, tt, nc)
            for t in row_seqs]
    v_c = v.reshape(bsz, t_len, RWKV_HEADS, RWKV_HEAD).transpose(1, 3, 0, 2).reshape(t_len, RWKV_HEAD, nc)
    s0 = s_rwkv.transpose(3, 2, 0, 1).reshape(RWKV_HEAD, RWKV_HEAD, nc)
    y_c, s_c = _rwkv_scan(*rows, v_c, w["rwkv_tiles"], s0, tt)
    y = y_c.reshape(t_len, RWKV_HEAD, bsz, RWKV_HEADS).transpose(2, 0, 3, 1).reshape(bsz * t_len, RWKV_WIDTH)
    return y, s_c.reshape(RWKV_HEAD, RWKV_HEAD, bsz, RWKV_HEADS).transpose(2, 3, 1, 0)


def _chain_tile(p):
    t = p.reshape(RWKV_HEADS, RWKV_HEAD).T
    return jnp.tile(t, (1, LANES // RWKV_HEADS))


def _chain_tile_paired(p):
    pairs = RWKV_HEADS // 2
    t = p.reshape(pairs, 2, RWKV_HEAD).transpose(2, 1, 0)[:, :, None, :]
    return jnp.broadcast_to(t, (RWKV_HEAD, 2, LANES // RWKV_HEADS, pairs)).reshape(RWKV_HEAD, LANES)


def _lru_kernel(gate_ref, x_ref, cinit_ref, cw_ref, cb_ref, wri_ref, br_ref, bi_ref, lam_ref, h0_ref,
                o_ref, hfin_ref, halo, hprev, *, tt, bb):
    j = pl.program_id(1)

    @pl.when(j == 0)
    def _():
        halo[...] = cinit_ref[...]
        for i in range(bb):
            hprev[i] = jnp.broadcast_to(h0_ref[i], (SUBLANES, LRU_WIDTH))

    rows = _row_iota((tt, LRU_BLOCK))
    for i in range(bb):
        u = x_ref[i]
        ext = jnp.concatenate([halo[i], u], axis=0)
        xc = cb_ref[...] + _shifted(ext, 3, tt) * cw_ref[0:1, :]
        xc = xc + _shifted(ext, 2, tt) * cw_ref[1:2, :]
        xc = xc + _shifted(ext, 1, tt) * cw_ref[2:3, :]
        xc = xc + u * cw_ref[3:4, :]
        halo[i] = u[tt - SUBLANES:tt, :]

        for n in range(LRU_BLOCKS):
            cs = slice(n * LRU_BLOCK, (n + 1) * LRU_BLOCK)
            xb = xc[:, cs]
            ri = _dot(xb.astype(BF16), wri_ref[n])
            rg = jax.nn.sigmoid(ri[:, :LRU_BLOCK] + br_ref[:, cs])
            ig = jax.nn.sigmoid(ri[:, LRU_BLOCK:] + bi_ref[:, cs])
            log_a = -LRU_C * rg * _softplus(-lam_ref[:, cs])
            a = jnp.exp(log_a)
            b = jnp.sqrt(-jnp.tanh(log_a) * (a * a + 1.0)) * ig * xb
            d = 1
            while d < SUBLANES:
                keep = (rows & (SUBLANES - 1)) >= d
                a_sh = jnp.where(keep, pltpu.roll(a, d, axis=0), 1.0)
                b_sh = jnp.where(keep, pltpu.roll(b, d, axis=0), 0.0)
                b = b + a * b_sh
                a = a * a_sh
                d *= 2
            h_in = hprev[i, 0:1, cs]
            groups = []
            for r0 in range(0, tt, SUBLANES):
                h_grp = a[r0:r0 + SUBLANES] * h_in + b[r0:r0 + SUBLANES]
                groups.append(h_grp)
                h_in = h_grp[SUBLANES - 1:SUBLANES, :]
            h = jnp.concatenate(groups, axis=0)
            o_ref[i, :, cs] = (h * _gelu_tanh(gate_ref[i, :, cs])).astype(BF16)
            h_last = h[tt - 1:tt, :]
            hprev[i, :, cs] = jnp.broadcast_to(h_last, (SUBLANES, LRU_BLOCK))
            hfin_ref[i, :, cs] = h_last


def _lru(p1, cinit, cw, cb, wri, br, bi, lam, h0, bsz, t_len):
    tt = min(t_len, 64)
    nt = t_len // tt
    bb = _batch_block(bsz, t_len)
    p3 = p1.reshape(bsz, t_len, P1_COLS)
    rows = lambda w: (lambda b, j: (b, j, w))
    per_seq = lambda b, j: (b, 0, 0)
    const2 = lambda b, j: (0, 0)
    o, h_fin = pl.pallas_call(
        functools.partial(_lru_kernel, tt=tt, bb=bb),
        grid=(bsz // bb, nt),
        in_specs=[
            pl.BlockSpec((bb, tt, 1024), rows(0)),
            pl.BlockSpec((bb, tt, 1024), rows(1)),
            pl.BlockSpec((bb, SUBLANES, LRU_WIDTH), per_seq),
            pl.BlockSpec((SUBLANES, LRU_WIDTH), const2),
            pl.BlockSpec((1, LRU_WIDTH), const2),
            pl.BlockSpec((LRU_BLOCKS, LRU_BLOCK, 2 * LRU_BLOCK), lambda b, j: (0, 0, 0)),
            pl.BlockSpec((1, LRU_WIDTH), const2),
            pl.BlockSpec((1, LRU_WIDTH), const2),
            pl.BlockSpec((1, LRU_WIDTH), const2),
            pl.BlockSpec((bb, 1, LRU_WIDTH), per_seq),
        ],
        out_specs=[pl.BlockSpec((bb, tt, LRU_WIDTH), rows(0)),
                   pl.BlockSpec((bb, 1, LRU_WIDTH), per_seq)],
        out_shape=[jax.ShapeDtypeStruct((bsz, t_len, LRU_WIDTH), BF16),
                   jax.ShapeDtypeStruct((bsz, 1, LRU_WIDTH), F32)],
        scratch_shapes=[pltpu.VMEM((bb, SUBLANES, LRU_WIDTH), F32), pltpu.VMEM((bb, SUBLANES, LRU_WIDTH), F32)],
        compiler_params=_cparams(("parallel", "arbitrary")),
        name="lru",
    )(p3, p3, cinit, cw, cb, wri, br, bi, lam, h0)
    return o.reshape(bsz * t_len, LRU_WIDTH), h_fin


def _expand_heads(x, e_ref, terms):
    parts = []
    rest = x
    for _ in range(terms):
        part = rest.astype(BF16).astype(F32)
        parts.append(part)
        rest = rest - part
    rows = x.shape[0]
    out = _dot(jnp.concatenate(parts, axis=0), e_ref[...])
    acc = out[:rows]
    for t in range(1, terms):
        acc = acc + out[t * rows:(t + 1) * rows]
    return acc


def _ssd_kernel(z_ref, xbc_ref, dt_ref, cinit_ref, cw_ref, cb_ref, dtb_ref, alog_ref, dd_ref, nw_ref,
                ej_ref, ep_ref, s0_ref, o_ref, s_ref, halo, st_t, *, c, mm, bb):
    j = pl.program_id(1)
    hg = SSD_HEADS // SSD_GROUPS
    gw = SSD_INNER // SSD_GROUPS
    pair = 2 * SSD_HEADDIM

    @pl.when(j == 0)
    def _():
        halo[...] = cinit_ref[...]
        for i in range(bb):
            for q in range(SSD_HEADS // 2):
                two = jnp.concatenate([s0_ref[i, 2 * q], s0_ref[i, 2 * q + 1]], axis=0)
                st_t[i, :, q * pair:(q + 1) * pair] = two.T

    wj = SSD_HEADS * c
    row_t = _row_iota((c, wj))
    lane_t = lax.broadcasted_iota(jnp.int32, (c, wj), 1) & (c - 1)
    causal = lane_t <= row_t
    diag = lane_t == row_t
    bd_rows = hg * c
    bd_mask = ((_row_iota((bd_rows, gw)) >> (c.bit_length() - 1))
               == (lax.broadcasted_iota(jnp.int32, (bd_rows, gw), 1) >> (SSD_HEADDIM.bit_length() - 1)))
    neg_a = -jnp.exp(alog_ref[...])
    gs = SSD_GROUPS * SSD_STATE
    for i in range(bb):
        u = xbc_ref[i]
        ext = jnp.concatenate([halo[i], u], axis=0)
        xc = cb_ref[...] + _shifted(ext, 3, c) * cw_ref[0:1, :]
        xc = xc + _shifted(ext, 2, c) * cw_ref[1:2, :]
        xc = xc + _shifted(ext, 1, c) * cw_ref[2:3, :]
        xc = _silu(xc + u * cw_ref[3:4, :])
        halo[i] = u[c - SUBLANES:c, :]

        xs = xc[:, :SSD_INNER]
        bm = xc[:, SSD_INNER:SSD_INNER + gs].astype(mm)
        cm = xc[:, SSD_INNER + gs:].astype(mm)
        dt = _softplus(dt_ref[i] + dtb_ref[...])
        cs = _cumsum_rows(dt * neg_a, c)
        cs_p = _expand_heads(cs, ep_ref, 3)
        dt_p = _expand_heads(dt, ep_ref, 2)
        cs_j = cs_p if c == SSD_HEADDIM else _expand_heads(cs, ej_ref, 3)
        cs_row = jnp.sum(jnp.where(diag, cs_j, 0.0), axis=0, keepdims=True)
        lmat = jnp.exp(jnp.where(causal, cs_j - cs_row, -jnp.inf))
        last_p = cs_p[c - 1:c, :]
        e_cs = jnp.exp(cs_p)
        e_last = jnp.exp(last_p)
        xdt = xs * dt_p
        xw = (xdt * jnp.exp(last_p - cs_p)).astype(mm)
        xdt = xdt.astype(mm)
        y_groups = []
        for g in range(SSD_GROUPS):
            gl = slice(g * gw, (g + 1) * gw)
            ss = slice(g * SSD_STATE, (g + 1) * SSD_STATE)
            cb_x = _dot_nt(cm[:, ss], jnp.concatenate([bm[:, ss]] * hg, axis=0))
            m_x = (cb_x * lmat[:, g * bd_rows:(g + 1) * bd_rows]).astype(mm)
            blocks = jnp.where(bd_mask, jnp.concatenate([xdt[:, gl]] * hg, axis=0), jnp.zeros((), mm))
            st = st_t[i, :, gl]
            y_groups.append(_dot(m_x, blocks) + e_cs[:, gl] * _dot(cm[:, ss], st.astype(mm)))
            st_t[i, :, gl] = st * e_last[:, gl] + _dot_tn(bm[:, ss], xw[:, gl])
        y = (jnp.concatenate(y_groups, axis=1) + dd_ref[...] * xs) * _silu(z_ref[i])
        for g in range(SSD_GROUPS):
            gl = slice(g * gw, (g + 1) * gw)
            o_ref[i, :, gl] = _rms(y[:, gl], nw_ref[:, gl], 1e-5).astype(BF16)

    @pl.when(j == pl.num_programs(1) - 1)
    def _():
        for i in range(bb):
            for q in range(SSD_HEADS // 2):
                two = st_t[i, :, q * pair:(q + 1) * pair].T
                s_ref[i, 2 * q] = two[:SSD_HEADDIM]
                s_ref[i, 2 * q + 1] = two[SSD_HEADDIM:]


def _ssd(p1, cinit, cw, cb, dtb, alog, dd, nw, s0, bsz, t_len):
    c = min(SSD_CHUNK, t_len)
    nt = t_len // c
    bb = _batch_block(bsz, t_len)
    mm = BF16 if c % 16 == 0 else F32
    p3 = p1.reshape(bsz, t_len, P1_COLS)
    rows = lambda w: (lambda b, j: (b, j, w))
    const2 = lambda b, j: (0, 0)
    st_spec = pl.BlockSpec((bb, SSD_HEADS, SSD_HEADDIM, SSD_STATE), lambda b, j: (b, 0, 0, 0))
    head = jnp.arange(LANES, dtype=jnp.int32)[:, None]
    expand = lambda width: (jnp.arange(SSD_HEADS * width, dtype=jnp.int32)[None, :] // width == head).astype(F32)
    o, s_new = pl.pallas_call(
        functools.partial(_ssd_kernel, c=c, mm=mm, bb=bb),
        grid=(bsz // bb, nt),
        in_specs=[
            pl.BlockSpec((bb, c, 1024), rows(2)),
            pl.BlockSpec((bb, c, SSD_CONV_CH), rows(2)),
            pl.BlockSpec((bb, c, LANES), rows(36)),
            pl.BlockSpec((bb, SUBLANES, SSD_CONV_CH), lambda b, j: (b, 0, 0)),
            pl.BlockSpec((SUBLANES, SSD_CONV_CH), const2),
            pl.BlockSpec((1, SSD_CONV_CH), const2),
            pl.BlockSpec((1, LANES), const2),
            pl.BlockSpec((1, LANES), const2),
            pl.BlockSpec((1, SSD_INNER), const2),
            pl.BlockSpec((1, SSD_INNER), const2),
            pl.BlockSpec((LANES, SSD_HEADS * c), const2),
            pl.BlockSpec((LANES, SSD_INNER), const2),
            st_spec,
        ],
        out_specs=[pl.BlockSpec((bb, c, SSD_INNER), rows(0)), st_spec],
        out_shape=[jax.ShapeDtypeStruct((bsz, t_len, SSD_INNER), BF16),
                   jax.ShapeDtypeStruct((bsz, SSD_HEADS, SSD_HEADDIM, SSD_STATE), F32)],
        scratch_shapes=[pltpu.VMEM((bb, SUBLANES, SSD_CONV_CH), F32),
                        pltpu.VMEM((bb, SSD_STATE, SSD_INNER), F32)],
        compiler_params=_cparams(("parallel", "arbitrary")),
        name="ssd",
    )(p3, p3, p3, cinit, cw, cb, dtb, alog, dd, nw, expand(c), expand(SSD_HEADDIM), s0)
    return o.reshape(bsz * t_len, SSD_INNER), s_new


def _pad_rows_to_8(x):
    return jnp.pad(x, ((0, 0), (SUBLANES - x.shape[1], 0), (0, 0)))


def _pad_lanes(x, width):
    return jnp.pad(x, ((0, 0), (0, width - x.shape[1])))


def _trunk(x, states, w):
    s_gla, s_rwkv, s_shift, s_lru, s_lru_conv, s_ssd, s_ssd_conv = states
    bsz, t_len, d = x.shape
    n = bsz * t_len
    x2 = x.reshape(n, d)

    p0 = _in_proj(x2, w["g_mix0"], w["w_in0"], P0_TN)
    o_gla, s_gla_new = _gla(p0, s_gla, w["gla_wa2"], w["gla_ba"], w["gla_gn"], bsz, t_len)
    shift8 = jnp.broadcast_to(s_shift[:, None, :], (bsz, SUBLANES, RWKV_COLS)).reshape(bsz * SUBLANES, RWKV_COLS)
    r, wdec, k, v, a, g = _rwkv_prep(p0, shift8, w["rwkv_mu"], w["rwkv_w0"], w["rwkv_a0"], w["rwkv_wa"],
                                     w["rwkv_g2"], bsz, t_len)
    y_rwkv, s_rwkv_new = _rwkv_recurrence((r, wdec, k, a, v), s_rwkv, w, bsz, t_len)
    shift_new = p0.reshape(bsz, t_len, P0_COLS)[:, -1, 3072:3072 + RWKV_COLS]
    x2 = _out_ffn(x2, o_gla, y_rwkv, g, w["w_out0"], w["g_ffn0"], w["wg0"], w["wu0"], w["wd0"], None)

    p1 = _in_proj(x2, w["g_mix1"], w["w_in1"], P1_TN)
    lru_out, h_lru = _lru(p1, _pad_rows_to_8(s_lru_conv), w["lru_cw"], w["lru_cb"], w["lru_wri"], w["lru_br"],
                          w["lru_bi"], w["lru_lam"], s_lru.reshape(bsz, 1, LRU_WIDTH), bsz, t_len)
    y_ssd, s_ssd_new = _ssd(p1, _pad_rows_to_8(s_ssd_conv), w["ssd_cw"], w["ssd_cb"], w["ssd_dtb"], w["ssd_alog"],
                            w["ssd_dd"], w["ssd_nw"], s_ssd, bsz, t_len)
    keep = CONV_W - 1
    p1_tail = p1.reshape(bsz, t_len, P1_COLS)[:, -keep:]
    lru_conv_new = p1_tail[:, :, 1024:2048]
    ssd_conv_new = p1_tail[:, :, 3072:3072 + SSD_CONV_CH]
    y = _out_ffn(x2, lru_out, y_ssd, None, w["w_out1"], w["g_ffn1"], w["wg1"], w["wu1"], w["wd1"], w["g_final"])
    return y.reshape(bsz, t_len, d), (s_gla_new, s_rwkv_new, shift_new, h_lru.reshape(bsz, LRU_WIDTH),
                                      lru_conv_new, s_ssd_new, ssd_conv_new)


def kernel(x_prompt, x_sample, state_gla, state_rwkv, state_rwkv_shift, state_lru, state_lru_conv, state_ssd, state_ssd_conv, w_in0, gla_w_a2, gla_b_a, gla_g_norm, rwkv_mu, rwkv_w0, rwkv_w2, rwkv_a0, rwkv_a2, rwkv_g2, rwkv_k_k, rwkv_k_a, rwkv_r_k, rwkv_ln_w, rwkv_ln_b, w_out0, w_in1, lru_conv_w, lru_conv_b, lru_w_r, lru_b_r, lru_w_i, lru_b_i, lru_lambda, ssd_conv_w, ssd_conv_b, ssd_dt_bias, ssd_a_log, ssd_d, ssd_norm_w, w_out1, g_mix, g_ffn, w_ffn_gate, w_ffn_up, w_ffn_down, g_final):
    row = lambda p: p.reshape(1, -1).astype(F32)
    w0 = jnp.concatenate([w_in0[:, :2048], w_in0[:, 2064:3088], w_in0[:, 3088:],
                          _pad_lanes(w_in0[:, 2048:2064], LANES)], axis=1)
    zeros64 = jnp.zeros((64, RWKV_WIDTH), F32)
    rwkv_wa = jnp.concatenate([jnp.concatenate([rwkv_w2, zeros64], axis=1),
                               jnp.concatenate([zeros64, rwkv_a2], axis=1)], axis=0)
    w = {
        "w_in0": w0.astype(BF16),
        "w_in1": _pad_lanes(w_in1, P1_COLS).astype(BF16),
        "g_mix0": g_mix[0], "g_mix1": g_mix[1], "g_ffn0": g_ffn[0], "g_ffn1": g_ffn[1], "g_final": g_final,
        "gla_wa2": jnp.pad(gla_w_a2, ((0, LANES - GLA_GATE_RANK), (0, 0))).astype(BF16),
        "gla_ba": row(gla_b_a), "gla_gn": row(gla_g_norm),
        "rwkv_mu": row(rwkv_mu), "rwkv_w0": row(rwkv_w0), "rwkv_a0": row(rwkv_a0),
        "rwkv_wa": rwkv_wa.astype(BF16), "rwkv_g2": rwkv_g2.astype(BF16),
        "rwkv_tiles": [_chain_tile(p) for p in (rwkv_k_k, rwkv_k_a, rwkv_r_k, rwkv_ln_w, rwkv_ln_b)],
        "rwkv_tiles_paired": [_chain_tile_paired(p)
                              for p in (rwkv_k_k, rwkv_k_a, rwkv_r_k, rwkv_ln_w, rwkv_ln_b)],
        "w_out0": w_out0.astype(BF16), "w_out1": w_out1.astype(BF16),
        "wg0": w_ffn_gate[0].astype(BF16), "wu0": w_ffn_up[0].astype(BF16), "wd0": w_ffn_down[0].astype(BF16),
        "wg1": w_ffn_gate[1].astype(BF16), "wu1": w_ffn_up[1].astype(BF16), "wd1": w_ffn_down[1].astype(BF16),
        "lru_cw": jnp.pad(lru_conv_w, ((0, SUBLANES - CONV_W), (0, 0))), "lru_cb": row(lru_conv_b),
        "lru_wri": jnp.concatenate([lru_w_r, lru_w_i], axis=2).astype(BF16),
        "lru_br": row(lru_b_r), "lru_bi": row(lru_b_i), "lru_lam": row(lru_lambda),
        "ssd_cw": jnp.pad(ssd_conv_w, ((0, SUBLANES - CONV_W), (0, 0))), "ssd_cb": row(ssd_conv_b),
        "ssd_dtb": _pad_lanes(row(ssd_dt_bias), LANES), "ssd_alog": _pad_lanes(row(ssd_a_log), LANES),
        "ssd_dd": row(jnp.repeat(ssd_d, SSD_HEADDIM)), "ssd_nw": row(ssd_norm_w),
    }
    bp = x_prompt.shape[0]
    prompt_init = (
        jnp.zeros((bp, GLA_HEADS, GLA_DK, GLA_DV), F32),
        jnp.zeros((bp, RWKV_HEADS, RWKV_HEAD, RWKV_HEAD), F32),
        jnp.zeros((bp, RWKV_COLS), F32),
        jnp.zeros((bp, LRU_WIDTH), F32),
        jnp.zeros((bp, CONV_W - 1, LRU_WIDTH), F32),
        jnp.zeros((bp, SSD_HEADS, SSD_HEADDIM, SSD_STATE), F32),
        jnp.zeros((bp, CONV_W - 1, SSD_CONV_CH), F32),
    )
    sample_init = (state_gla, state_rwkv, state_rwkv_shift, state_lru, state_lru_conv, state_ssd, state_ssd_conv)
    y_prompt, p_states = _trunk(x_prompt, prompt_init, w)
    y_sample, s_states = _trunk(x_sample, sample_init, w)
    return (y_prompt, y_sample, *p_states, *s_states)
```

```python
import functools

import jax
import jax.numpy as jnp
from jax import lax
from jax.experimental import pallas as pl
from jax.experimental.pallas import tpu as pltpu

F32 = jnp.float32
BF16 = jnp.bfloat16

D_MODEL = 1024
NORM_EPS = 1e-6
GLA_HEADS = 4
GLA_DK = 128
GLA_DV = 256
GLA_GATE_RANK = 16
GLA_GATE_NORM = 16.0
GLA_CHUNK = 64
RWKV_HEAD = 64
RWKV_HEADS = 16
RWKV_WIDTH = 1024
RWKV_LORA = 256
RWKV_COLS = 3 * RWKV_WIDTH + RWKV_LORA
RWKV_GN_EPS = 64e-5
LRU_WIDTH = 1024
LRU_BLOCKS = 8
LRU_BLOCK = 128
LRU_C = 8.0
CONV_W = 4
SSD_INNER = 1024
SSD_HEADDIM = 64
SSD_HEADS = 16
SSD_GROUPS = 2
SSD_STATE = 128
SSD_CHUNK = 64
SSD_CONV_CH = SSD_INNER + 2 * SSD_GROUPS * SSD_STATE
D_FF = 2816

LANES = 128
SUBLANES = 8
VMEM_LIMIT_BYTES = 56 * 1024 * 1024

P0_COLS = 6528
P0_TN = 2176
P1_COLS = 4864
P1_TN = 2432


def _cparams(semantics):
    return pltpu.CompilerParams(dimension_semantics=semantics, vmem_limit_bytes=VMEM_LIMIT_BYTES)


def _dot(a, b):
    return jnp.dot(a, b, preferred_element_type=F32)


def _dot_nt(a, b):
    return lax.dot_general(a, b, (((1,), (1,)), ((), ())), preferred_element_type=F32)


def _dot_tn(a, b):
    return lax.dot_general(a, b, (((0,), (0,)), ((), ())), preferred_element_type=F32)


def _softplus(x):
    return jnp.maximum(x, 0.0) + jnp.log1p(jnp.exp(-jnp.abs(x)))


def _silu(x):
    return x * jax.nn.sigmoid(x)


def _gelu_tanh(x):
    return 0.5 * x * (1.0 + jnp.tanh(0.7978845608028654 * (x + 0.044715 * (x * x * x))))


def _rms(x, g, eps):
    return x * lax.rsqrt(jnp.mean(x * x, axis=-1, keepdims=True) + eps) * g


def _row_iota(shape):
    return lax.broadcasted_iota(jnp.int32, shape, 0)


def _cumsum_rows(x, seg):
    rows = _row_iota(x.shape) & (seg - 1)
    d = 1
    while d < seg:
        x = x + jnp.where(rows >= d, pltpu.roll(x, d, axis=0), 0.0)
        d *= 2
    return x


def _shifted(ext, k, rows):
    return pltpu.roll(ext, k, axis=0)[SUBLANES:SUBLANES + rows]


def _in_proj_kernel(x_ref, g_ref, w_ref, o_ref, h_scr):
    @pl.when(pl.program_id(1) == 0)
    def _():
        h_scr[...] = _rms(x_ref[...], g_ref[...], NORM_EPS).astype(BF16)

    o_ref[...] = _dot(h_scr[...], w_ref[...])


def _in_proj(x2d, g, w, tn):
    n, d = x2d.shape
    cols = w.shape[1]
    tm = min(n, 1024)
    return pl.pallas_call(
        _in_proj_kernel,
        grid=(n // tm, cols // tn),
        in_specs=[
            pl.BlockSpec((tm, d), lambda i, j: (i, 0)),
            pl.BlockSpec((1, d), lambda i, j: (0, 0)),
            pl.BlockSpec((d, tn), lambda i, j: (0, j)),
        ],
        out_specs=pl.BlockSpec((tm, tn), lambda i, j: (i, j)),
        out_shape=jax.ShapeDtypeStruct((n, cols), F32),
        scratch_shapes=[pltpu.VMEM((tm, d), BF16)],
        compiler_params=_cparams(("parallel", "arbitrary")),
        name="in_proj",
    )(x2d, g.reshape(1, d), w)


def _out_ffn_kernel(*refs, has_gate, final_norm):
    refs = list(refs)
    x_ref, oa_ref, ob_ref = refs[:3]
    pos = 3
    gate_ref = None
    if has_gate:
        gate_ref = refs[pos]
        pos += 1
    woa_ref, wob_ref, gffn_ref, wg_ref, wu_ref, wd_ref = refs[pos:pos + 6]
    pos += 6
    gfin_ref = None
    if final_norm:
        gfin_ref = refs[pos]
        pos += 1
    out_ref, x1_scr, h_scr, acc_scr = refs[pos:pos + 4]

    k = pl.program_id(1)

    @pl.when(k == 0)
    def _():
        ob = ob_ref[...]
        if has_gate:
            ob = (ob * gate_ref[...]).astype(BF16)
        x1 = x_ref[...] + (_dot(oa_ref[...], woa_ref[...]) + _dot(ob, wob_ref[...]))
        x1_scr[...] = x1
        h_scr[...] = _rms(x1, gffn_ref[...], NORM_EPS).astype(BF16)
        acc_scr[...] = jnp.zeros_like(acc_scr)

    h = h_scr[...]
    act = (_silu(_dot(h, wg_ref[...])) * _dot(h, wu_ref[...])).astype(BF16)
    acc_scr[...] += _dot(act, wd_ref[...])

    @pl.when(k == pl.num_programs(1) - 1)
    def _():
        y = x1_scr[...] + acc_scr[...]
        if final_norm:
            y = _rms(y, gfin_ref[...], NORM_EPS)
        out_ref[...] = y


def _out_ffn(x2d, oa, ob, gate, wo, g_ffn, wg, wu, wd, g_final):
    n, d = x2d.shape
    tm = min(n, 512)
    tf = D_FF // 2
    has_gate = gate is not None
    final_norm = g_final is not None
    row = lambda i, k: (i, 0)
    const = lambda i, k: (0, 0)
    args = [x2d, oa, ob]
    in_specs = [pl.BlockSpec((tm, d), row)] * 3
    if has_gate:
        args.append(gate)
        in_specs.append(pl.BlockSpec((tm, d), row))
    half = wo.shape[0] // 2
    args += [wo[:half], wo[half:], g_ffn.reshape(1, d), wg, wu, wd]
    in_specs += [
        pl.BlockSpec((half, d), const),
        pl.BlockSpec((half, d), const),
        pl.BlockSpec((1, d), const),
        pl.BlockSpec((d, tf), lambda i, k: (0, k)),
        pl.BlockSpec((d, tf), lambda i, k: (0, k)),
        pl.BlockSpec((tf, d), lambda i, k: (k, 0)),
    ]
    if final_norm:
        args.append(g_final.reshape(1, d))
        in_specs.append(pl.BlockSpec((1, d), const))
    return pl.pallas_call(
        functools.partial(_out_ffn_kernel, has_gate=has_gate, final_norm=final_norm),
        grid=(n // tm, D_FF // tf),
        in_specs=in_specs,
        out_specs=pl.BlockSpec((tm, d), row),
        out_shape=jax.ShapeDtypeStruct((n, d), F32),
        scratch_shapes=[pltpu.VMEM((tm, d), F32), pltpu.VMEM((tm, d), BF16), pltpu.VMEM((tm, d), F32)],
        compiler_params=_cparams(("parallel", "arbitrary")),
        name="out_ffn",
    )(*args)


def _gla_kernel(q_ref, k_ref, v_ref, og_ref, al_ref, wa2_ref, ba_ref, gn_ref, s0_ref,
                o_ref, sout_ref, st_scr, *, c, mm, bb):
    n = pl.program_id(1)

    @pl.when(n == 0)
    def _():
        for i in range(bb):
            for h in range(GLA_HEADS):
                st_scr[i * GLA_HEADS + h] = s0_ref[i, h].T

    mask = _row_iota((c, c)) >= lax.broadcasted_iota(jnp.int32, (c, c), 1)
    gn = gn_ref[...]
    for i in range(bb):
        la_all = _dot(al_ref[i].astype(BF16), wa2_ref[...]) + ba_ref[...]
        la_all = -_softplus(-la_all) * (1.0 / GLA_GATE_NORM)
        b_all = _cumsum_rows(la_all, c)
        for h in range(GLA_HEADS):
            ks = slice(h * GLA_DK, (h + 1) * GLA_DK)
            vs = slice(h * GLA_DV, (h + 1) * GLA_DV)
            b = b_all[:, ks]
            b_last = b[c - 1:c, :]
            q = q_ref[i, :, ks] * (GLA_DK ** -0.5)
            k = k_ref[i, :, ks]
            v = v_ref[i, :, vs].astype(mm)
            qd = (q * jnp.exp(b)).astype(mm)
            kd = (k * jnp.exp(-b)).astype(mm)
            kc = (k * jnp.exp(b_last - b)).astype(mm)
            scores = jnp.where(mask, _dot_nt(qd, kd), 0.0)
            st = st_scr[i * GLA_HEADS + h]
            o = _dot(scores.astype(mm), v) + _dot_nt(qd, st.astype(mm))
            st_scr[i * GLA_HEADS + h] = st * jnp.exp(b_last) + _dot_tn(v, kc)
            o = _rms(o, gn, 1e-5)
            o_ref[i, :, vs] = (o * _silu(og_ref[i, :, vs])).astype(BF16)

    @pl.when(n == pl.num_programs(1) - 1)
    def _():
        for i in range(bb):
            for h in range(GLA_HEADS):
                sout_ref[i, h] = st_scr[i * GLA_HEADS + h].T


def _batch_block(bsz, t_len):
    return 2 if t_len >= 64 else min(bsz, 8)


def _gla(p0, s0, wa2, ba, gn, bsz, t_len):
    c = min(GLA_CHUNK, t_len)
    nt = t_len // c
    bb = _batch_block(bsz, t_len)
    mm = BF16 if c % 16 == 0 else F32
    p3 = p0.reshape(bsz, t_len, P0_COLS)
    rows = lambda w: (lambda b, j: (b, j, w))
    const = lambda b, j: (0, 0)
    st_spec = pl.BlockSpec((bb, GLA_HEADS, GLA_DK, GLA_DV), lambda b, j: (b, 0, 0, 0))
    o, s_new = pl.pallas_call(
        functools.partial(_gla_kernel, c=c, mm=mm, bb=bb),
        grid=(bsz // bb, nt),
        in_specs=[
            pl.BlockSpec((bb, c, 512), rows(0)),
            pl.BlockSpec((bb, c, 512), rows(1)),
            pl.BlockSpec((bb, c, 1024), rows(1)),
            pl.BlockSpec((bb, c, 1024), rows(2)),
            pl.BlockSpec((bb, c, LANES), rows(50)),
            pl.BlockSpec((LANES, 512), const),
            pl.BlockSpec((1, 512), const),
            pl.BlockSpec((1, GLA_DV), const),
            st_spec,
        ],
        out_specs=[pl.BlockSpec((bb, c, 1024), rows(0)), st_spec],
        out_shape=[jax.ShapeDtypeStruct((bsz, t_len, 1024), BF16),
                   jax.ShapeDtypeStruct((bsz, GLA_HEADS, GLA_DK, GLA_DV), F32)],
        scratch_shapes=[pltpu.VMEM((bb * GLA_HEADS, GLA_DV, GLA_DK), F32)],
        compiler_params=_cparams(("parallel", "arbitrary")),
        name="gla",
    )(p3, p3, p3, p3, p3, wa2, ba, gn, s0)
    return o.reshape(bsz * t_len, 1024), s_new


def _rwkv_prep_kernel(r_ref, k_ref, v_ref, l_ref, ir_ref, ik_ref, iv_ref, il_ref,
                      mu_ref, w0_ref, a0_ref, wa_ref, g2_ref,
                      ro_ref, wo_ref, ko_ref, vo_ref, ao_ref, go_ref, carry,
                      *, tm, t_len, multi_seq):
    j = pl.program_id(1)
    cols = ((0, 1024), (1024, 2048), (2048, 3072), (3072, RWKV_COLS))

    if not multi_seq:
        @pl.when(j == 0)
        def _():
            for (lo, hi), iref in zip(cols, (ir_ref, ik_ref, iv_ref, il_ref)):
                carry[:, lo:hi] = iref[...]

    def mixed(x_ref, i_ref, lo, hi):
        x = x_ref[...]
        rolled = pltpu.roll(x, 1, axis=0)
        rows = _row_iota(x.shape)
        if multi_seq:
            prev = jnp.where((rows & (t_len - 1)) == 0, i_ref[...], rolled)
        else:
            prev = jnp.where(rows == 0, carry[SUBLANES - 1:SUBLANES, lo:hi], rolled)
            carry[:, lo:hi] = x[tm - SUBLANES:tm, :]
        return x + (prev - x) * mu_ref[:, lo:hi]

    r = mixed(r_ref, ir_ref, *cols[0])
    k = mixed(k_ref, ik_ref, *cols[1])
    v = mixed(v_ref, iv_ref, *cols[2])
    lo_rank = mixed(l_ref, il_ref, *cols[3])

    wa_in = lo_rank[:, :LANES]
    lane = lax.broadcasted_iota(jnp.int32, wa_in.shape, 1)
    wa_in = jnp.where(lane < 64, jnp.tanh(wa_in), wa_in).astype(BF16)
    wa = _dot(wa_in, wa_ref[...])
    g = _dot(jax.nn.sigmoid(lo_rank[:, LANES:]).astype(BF16), g2_ref[...])
    w = -_softplus(-(w0_ref[...] + wa[:, :RWKV_WIDTH])) - 0.5
    ro_ref[...] = r
    wo_ref[...] = -jnp.exp(w)
    ko_ref[...] = k
    vo_ref[...] = v
    ao_ref[...] = jax.nn.sigmoid(a0_ref[...] + wa[:, RWKV_WIDTH:])
    go_ref[...] = g


def _rwkv_prep(p0, shift8, mu, w0, a0, wa, g2, bsz, t_len):
    n = bsz * t_len
    multi_seq = t_len < 256
    tm = min(n, 512) if multi_seq else 256
    nt = 1 if multi_seq else t_len // tm
    nb = n // tm if multi_seq else bsz
    ti = tm if multi_seq else SUBLANES
    rows = lambda w: (lambda i, j: (i * nt + j, w))
    init = lambda w: (lambda i, j: (i, w))
    const = lambda i, j: (0, 0)
    out = jax.ShapeDtypeStruct((n, RWKV_WIDTH), F32)
    return pl.pallas_call(
        functools.partial(_rwkv_prep_kernel, tm=tm, t_len=t_len, multi_seq=multi_seq),
        grid=(nb, nt),
        in_specs=[
            pl.BlockSpec((tm, 1024), rows(3)),
            pl.BlockSpec((tm, 1024), rows(4)),
            pl.BlockSpec((tm, 1024), rows(5)),
            pl.BlockSpec((tm, RWKV_LORA), rows(24)),
            pl.BlockSpec((ti, 1024), init(0)),
            pl.BlockSpec((ti, 1024), init(1)),
            pl.BlockSpec((ti, 1024), init(2)),
            pl.BlockSpec((ti, RWKV_LORA), init(12)),
            pl.BlockSpec((1, RWKV_COLS), const),
            pl.BlockSpec((1, RWKV_WIDTH), const),
            pl.BlockSpec((1, RWKV_WIDTH), const),
            pl.BlockSpec((LANES, 2 * RWKV_WIDTH), const),
            pl.BlockSpec((LANES, RWKV_WIDTH), const),
        ],
        out_specs=[pl.BlockSpec((tm, RWKV_WIDTH), rows(0))] * 6,
        out_shape=[out] * 6,
        scratch_shapes=[pltpu.VMEM((SUBLANES, RWKV_COLS), F32)],
        compiler_params=_cparams(("parallel", "arbitrary")),
        name="rwkv_prep",
    )(p0, p0, p0, p0, shift8, shift8, shift8, shift8, mu, w0, a0, wa, g2)


def _rwkv_scan_kernel(r_ref, lw_ref, k_ref, a_ref, v_ref, kk_ref, ka_ref, rk_ref, lnw_ref, lnb_ref,
                      s0_ref, y_ref, s_ref, kb_scr, kp_scr, rs_scr, kn_scr, bonus_scr, grow_scr, *, tt):
    @pl.when(pl.program_id(1) == 0)
    def _():
        s_ref[...] = s0_ref[...]

    def tile_row(ref, j):
        return ref[pl.ds(j, 1), :]

    def bcast(x):
        return jnp.broadcast_to(x, (RWKV_HEAD, LANES))

    norm2 = jnp.zeros((tt, LANES), F32)
    for j in range(RWKV_HEAD):
        kj = k_ref[0, j] * tile_row(kk_ref, j)
        norm2 = norm2 + kj * kj
    norm = jnp.maximum(jnp.sqrt(norm2), 1e-12)
    bonus = jnp.zeros((tt, LANES), F32)
    s_kk = jnp.zeros((RWKV_HEAD, LANES), F32)
    for j in range(RWKV_HEAD):
        k = k_ref[0, j]
        a = a_ref[0, j]
        r = r_ref[0, j]
        kk = k * tile_row(kk_ref, j) / norm
        kp = k * (1.0 + (a - 1.0) * tile_row(ka_ref, j))
        log_g = _cumsum_rows(lw_ref[0, j], tt)
        grow = jnp.exp(log_g)
        shrink = jnp.exp(-log_g)
        kb_scr[j] = kk * a * shrink
        kp_scr[j] = kp * shrink
        rs_scr[j] = r * grow
        kn_scr[j] = pltpu.roll(kk, tt - 1, axis=0) * grow
        grow_scr[pl.ds(j, 1), :] = grow[tt - 1:tt, :]
        bonus = bonus + r * kp * tile_row(rk_ref, j)
        s_kk = s_kk + s_ref[j] * bcast(kk[0:1, :])
    bonus_scr[...] = bonus

    def row(ref, j, s):
        return bcast(ref[j, pl.ds(s, 1), :])

    def step(s, s_kk):
        sa = -s_kk
        v = v_ref[s]
        y = jnp.zeros_like(v)
        s_kk_next = jnp.zeros_like(v)
        for j in range(RWKV_HEAD):
            sn = s_ref[j] + sa * row(kb_scr, j, s) + v * row(kp_scr, j, s)
            s_ref[j] = sn
            y = y + sn * row(rs_scr, j, s)
            s_kk_next = s_kk_next + sn * row(kn_scr, j, s)
        mu = jnp.mean(y, axis=0, keepdims=True)
        yc = y - mu
        var = jnp.mean(yc * yc, axis=0, keepdims=True)
        y_ref[s] = yc * lax.rsqrt(var + RWKV_GN_EPS) * lnw_ref[...] + lnb_ref[...] + bonus_scr[pl.ds(s, 1), :] * v
        return s_kk_next

    lax.fori_loop(0, tt, step, s_kk)
    for j in range(RWKV_HEAD):
        s_ref[j] = s_ref[j] * bcast(tile_row(grow_scr, j))


def _rwkv_scan(r, lw, k, a, v, tiles, s0, tt):
    nb, _, _, nc = r.shape
    seq = pl.BlockSpec((1, RWKV_HEAD, tt, LANES), lambda g, t: (t, 0, 0, g))
    steps = pl.BlockSpec((tt, RWKV_HEAD, LANES), lambda g, t: (t, 0, g))
    tile = pl.BlockSpec((RWKV_HEAD, LANES), lambda g, t: (0, 0))
    st = pl.BlockSpec((RWKV_HEAD, RWKV_HEAD, LANES), lambda g, t: (0, 0, g))
    return pl.pallas_call(
        functools.partial(_rwkv_scan_kernel, tt=tt),
        grid=(nc // LANES, nb),
        in_specs=[seq] * 4 + [steps] + [tile] * 5 + [st],
        out_specs=[steps, st],
        out_shape=[jax.ShapeDtypeStruct((nb * tt, RWKV_HEAD, nc), F32),
                   jax.ShapeDtypeStruct((RWKV_HEAD, RWKV_HEAD, nc), F32)],
        scratch_shapes=[pltpu.VMEM((RWKV_HEAD, tt, LANES), F32)] * 4
        + [pltpu.VMEM((tt, LANES), F32), pltpu.VMEM((RWKV_HEAD, LANES), F32)],
        compiler_params=_cparams(("parallel", "arbitrary")),
        name="rwkv_scan",
    )(r, lw, k, a, v, *tiles, s0)


RELAYOUT_PITCH = LANES + SUBLANES
RELAYOUT_STEPS = LANES
SCAN_STEPS = 32


def _chain_input_kernel(*refs, kind, step_major):
    x_ref, init_ref, mu_ref = refs[:3]
    bias_ref = wmat_ref = z_scr = None
    if kind == "plain":
        o_ref, carry, z_scr = refs[3:]
    elif kind == "gate":
        wmat_ref, o_ref, carry = refs[3:]
    else:
        bias_ref, wmat_ref, o_ref, carry, z_scr = refs[3:]

    @pl.when(pl.program_id(0) == 0)
    def _():
        carry[...] = init_ref[...]

    nseq, steps, width = x_ref.shape
    pairs = RWKV_HEADS // 2
    rows = _row_iota((steps, width))
    for b in range(nseq):
        x = x_ref[b]
        prev = jnp.where(rows == 0, carry[b, SUBLANES - 1:SUBLANES, :], pltpu.roll(x, 1, axis=0))
        carry[b] = x[steps - SUBLANES:, :]
        mixed = x + (prev - x) * mu_ref[...]
        if kind == "plain":
            val = mixed
        elif kind == "gate":
            o_ref[b] = _dot(jax.nn.sigmoid(mixed[:, LANES:]).astype(BF16), wmat_ref[...])
            continue
        else:
            low = mixed[:, :LANES]
            lane = lax.broadcasted_iota(jnp.int32, low.shape, 1)
            low = jnp.where(lane < 64, jnp.tanh(low), low).astype(BF16)
            pre = bias_ref[...] + _dot(low, wmat_ref[...])
            val = -jnp.exp(-_softplus(-pre) - 0.5) if kind == "decay" else jax.nn.sigmoid(pre)
        for hp in range(pairs):
            q = b * pairs + hp
            z_scr[q * RELAYOUT_PITCH:q * RELAYOUT_PITCH + LANES, :] = val[:, hp * LANES:(hp + 1) * LANES].T
    if kind == "gate":
        return
    half = nseq * pairs
    for j in range(RWKV_HEAD):
        m = jnp.concatenate([z_scr[pl.ds(j, half, stride=RELAYOUT_PITCH), :],
                             z_scr[pl.ds(RWKV_HEAD + j, half, stride=RELAYOUT_PITCH), :]], axis=0)
        mt = m.T
        if step_major:
            o_ref[pl.ds(j, steps, stride=RWKV_HEAD), :] = mt
        else:
            for tq in range(steps // SCAN_STEPS):
                o_ref[tq, j] = mt[tq * SCAN_STEPS:(tq + 1) * SCAN_STEPS]


def _chain_input(p3, shift3, mu, col, init_col, kind, step_major=False, bias=None, wmat=None):
    bsz, t_len, _ = p3.shape
    width = mu.shape[1]
    const = lambda t: (0, 0)
    args = [p3, shift3, mu]
    in_specs = [pl.BlockSpec((bsz, RELAYOUT_STEPS, width), lambda t: (0, t, col)),
                pl.BlockSpec((bsz, SUBLANES, width), lambda t: (0, 0, init_col)),
                pl.BlockSpec((1, width), const)]
    if bias is not None:
        args.append(bias)
        in_specs.append(pl.BlockSpec(bias.shape, const))
    if wmat is not None:
        args.append(wmat)
        in_specs.append(pl.BlockSpec(wmat.shape, const))
    scratch = [pltpu.VMEM((bsz, SUBLANES, width), F32)]
    if kind == "gate":
        out_spec = pl.BlockSpec((bsz, RELAYOUT_STEPS, RWKV_WIDTH), lambda t: (0, t, 0))
        out_shape = jax.ShapeDtypeStruct((bsz, t_len, RWKV_WIDTH), F32)
    else:
        scratch.append(pltpu.VMEM((bsz * (RWKV_HEADS // 2) * RELAYOUT_PITCH, LANES), F32))
        if step_major:
            out_spec = pl.BlockSpec((RELAYOUT_STEPS * RWKV_HEAD, LANES), lambda t: (t, 0))
            out_shape = jax.ShapeDtypeStruct((t_len * RWKV_HEAD, LANES), F32)
        else:
            per = RELAYOUT_STEPS // SCAN_STEPS
            out_spec = pl.BlockSpec((per, RWKV_HEAD, SCAN_STEPS, LANES), lambda t: (t, 0, 0, 0))
            out_shape = jax.ShapeDtypeStruct((t_len // SCAN_STEPS, RWKV_HEAD, SCAN_STEPS, LANES), F32)
    return pl.pallas_call(
        functools.partial(_chain_input_kernel, kind=kind, step_major=step_major),
        grid=(t_len // RELAYOUT_STEPS,),
        in_specs=in_specs,
        out_specs=out_spec,
        out_shape=out_shape,
        scratch_shapes=scratch,
        compiler_params=_cparams(("arbitrary",)),
        name="rwkv_" + kind,
    )(*args)


def _from_chains_kernel(y_ref, o_ref, z_scr):
    nseq = o_ref.shape[0]
    pairs = RWKV_HEADS // 2
    half = nseq * pairs
    for i in range(RWKV_HEAD):
        mt = y_ref[pl.ds(i, RELAYOUT_STEPS, stride=RWKV_HEAD), :].T
        z_scr[pl.ds(i, half, stride=RELAYOUT_PITCH), :] = mt[:half]
        z_scr[pl.ds(RWKV_HEAD + i, half, stride=RELAYOUT_PITCH), :] = mt[half:]
    for b in range(nseq):
        for hp in range(pairs):
            q = b * pairs + hp
            o_ref[b, :, hp * LANES:(hp + 1) * LANES] = z_scr[q * RELAYOUT_PITCH:q * RELAYOUT_PITCH + LANES, :].T


def _from_chains(y2, bsz):
    t_len = y2.shape[0] // RWKV_HEAD
    return pl.pallas_call(
        _from_chains_kernel,
        grid=(t_len // RELAYOUT_STEPS,),
        in_specs=[pl.BlockSpec((RELAYOUT_STEPS * RWKV_HEAD, LANES), lambda t: (t, 0))],
        out_specs=pl.BlockSpec((bsz, RELAYOUT_STEPS, RWKV_WIDTH), lambda t: (0, t, 0)),
        out_shape=jax.ShapeDtypeStruct((bsz, t_len, RWKV_WIDTH), F32),
        scratch_shapes=[pltpu.VMEM((bsz * (RWKV_HEADS // 2) * RELAYOUT_PITCH, LANES), F32)],
        compiler_params=_cparams(("arbitrary",)),
        name="from_chains",
    )(y2)


def _rwkv_paired(p0, s_shift, s_rwkv, w, bsz, t_len):
    pairs = RWKV_HEADS // 2
    p3 = p0.reshape(bsz, t_len, P0_COLS)
    shift3 = jnp.broadcast_to(s_shift[:, None, :], (bsz, SUBLANES, RWKV_COLS))
    mu = w["rwkv_mu"]
    mu_main = [mu[:, i * RWKV_WIDTH:(i + 1) * RWKV_WIDTH] for i in range(3)]
    mu_low = mu[:, 3 * RWKV_WIDTH:]
    wa = w["rwkv_wa"]
    low_col = 3 * RWKV_WIDTH // RWKV_LORA
    r = _chain_input(p3, shift3, mu_main[0], 3, 0, "plain")
    k = _chain_input(p3, shift3, mu_main[1], 4, 1, "plain")
    v = _chain_input(p3, shift3, mu_main[2], 5, 2, "plain", step_major=True)
    lw = _chain_input(p3, shift3, mu_low, 24, low_col, "decay", bias=w["rwkv_w0"], wmat=wa[:, :RWKV_WIDTH])
    a = _chain_input(p3, shift3, mu_low, 24, low_col, "a", bias=w["rwkv_a0"], wmat=wa[:, RWKV_WIDTH:])
    g = _chain_input(p3, shift3, mu_low, 24, low_col, "gate", wmat=w["rwkv_g2"])
    s0 = s_rwkv.reshape(bsz, pairs, 2, RWKV_HEAD, RWKV_HEAD).transpose(4, 3, 2, 0, 1)
    y_c, s_c = _rwkv_scan(r, lw, k, a, v.reshape(t_len, RWKV_HEAD, LANES), w["rwkv_tiles_paired"],
                          s0.reshape(RWKV_HEAD, RWKV_HEAD, LANES), SCAN_STEPS)
    y = _from_chains(y_c.reshape(t_len * RWKV_HEAD, LANES), bsz)
    s_new = s_c.reshape(RWKV_HEAD, RWKV_HEAD, 2, bsz, pairs).transpose(3, 4, 2, 1, 0)
    return (y.reshape(bsz * t_len, RWKV_WIDTH), g.reshape(bsz * t_len, RWKV_WIDTH),
            s_new.reshape(bsz, RWKV_HEADS, RWKV_HEAD, RWKV_HEAD))


def _rwkv_general(p0, s_shift, s_rwkv, w, bsz, t_len):
    shift8 = jnp.broadcast_to(s_shift[:, None, :], (bsz, SUBLANES, RWKV_COLS)).reshape(bsz * SUBLANES, RWKV_COLS)
    r, lw, k, v, a, g = _rwkv_prep(p0, shift8, w["rwkv_mu"], w["rwkv_w0"], w["rwkv_a0"], w["rwkv_wa"],
                                   w["rwkv_g2"], bsz, t_len)
    row_seqs = (r, lw, k, a)
    tt = min(t_len, SCAN_STEPS)
    nb = t_len // tt
    nc = bsz * RWKV_HEADS
    rows = [t.reshape(bsz, nb, tt, RWKV_HEADS, RWKV_HEAD).transpose(1, 4, 2, 0, 3).reshape(nb, RWKV_HEAD, tt, nc)
            for t in row_seqs]
    v_c = v.reshape(bsz, t_len, RWKV_HEADS, RWKV_HEAD).transpose(1, 3, 0, 2).reshape(t_len, RWKV_HEAD, nc)
    s0 = s_rwkv.transpose(3, 2, 0, 1).reshape(RWKV_HEAD, RWKV_HEAD, nc)
    y_c, s_c = _rwkv_scan(*rows, v_c, w["rwkv_tiles"], s0, tt)
    y = y_c.reshape(t_len, RWKV_HEAD, bsz, RWKV_HEADS).transpose(2, 0, 3, 1).reshape(bsz * t_len, RWKV_WIDTH)
    return y, g, s_c.reshape(RWKV_HEAD, RWKV_HEAD, bsz, RWKV_HEADS).transpose(2, 3, 1, 0)


def _chain_tile(p):
    t = p.reshape(RWKV_HEADS, RWKV_HEAD).T
    return jnp.tile(t, (1, LANES // RWKV_HEADS))


def _chain_tile_paired(p):
    pairs = RWKV_HEADS // 2
    t = p.reshape(pairs, 2, RWKV_HEAD).transpose(2, 1, 0)[:, :, None, :]
    return jnp.broadcast_to(t, (RWKV_HEAD, 2, LANES // RWKV_HEADS, pairs)).reshape(RWKV_HEAD, LANES)


def _lru_kernel(gate_ref, x_ref, cinit_ref, cw_ref, cb_ref, wri_ref, br_ref, bi_ref, lam_ref, h0_ref,
                o_ref, hfin_ref, halo, hprev, *, tt, bb):
    j = pl.program_id(1)

    @pl.when(j == 0)
    def _():
        halo[...] = cinit_ref[...]
        for i in range(bb):
            hprev[i] = jnp.broadcast_to(h0_ref[i], (SUBLANES, LRU_WIDTH))

    rows = _row_iota((tt, LRU_BLOCK))
    for i in range(bb):
        u = x_ref[i]
        ext = jnp.concatenate([halo[i], u], axis=0)
        xc = cb_ref[...] + _shifted(ext, 3, tt) * cw_ref[0:1, :]
        xc = xc + _shifted(ext, 2, tt) * cw_ref[1:2, :]
        xc = xc + _shifted(ext, 1, tt) * cw_ref[2:3, :]
        xc = xc + u * cw_ref[3:4, :]
        halo[i] = u[tt - SUBLANES:tt, :]

        for n in range(LRU_BLOCKS):
            cs = slice(n * LRU_BLOCK, (n + 1) * LRU_BLOCK)
            xb = xc[:, cs]
            ri = _dot(xb.astype(BF16), wri_ref[n])
            rg = jax.nn.sigmoid(ri[:, :LRU_BLOCK] + br_ref[:, cs])
            ig = jax.nn.sigmoid(ri[:, LRU_BLOCK:] + bi_ref[:, cs])
            log_a = -LRU_C * rg * _softplus(-lam_ref[:, cs])
            a = jnp.exp(log_a)
            b = jnp.sqrt(-jnp.tanh(log_a) * (a * a + 1.0)) * ig * xb
            d = 1
            while d < SUBLANES:
                keep = (rows & (SUBLANES - 1)) >= d
                a_sh = jnp.where(keep, pltpu.roll(a, d, axis=0), 1.0)
                b_sh = jnp.where(keep, pltpu.roll(b, d, axis=0), 0.0)
                b = b + a * b_sh
                a = a * a_sh
                d *= 2
            h_in = hprev[i, 0:1, cs]
            groups = []
            for r0 in range(0, tt, SUBLANES):
                h_grp = a[r0:r0 + SUBLANES] * h_in + b[r0:r0 + SUBLANES]
                groups.append(h_grp)
                h_in = h_grp[SUBLANES - 1:SUBLANES, :]
            h = jnp.concatenate(groups, axis=0)
            o_ref[i, :, cs] = (h * _gelu_tanh(gate_ref[i, :, cs])).astype(BF16)
            h_last = h[tt - 1:tt, :]
            hprev[i, :, cs] = jnp.broadcast_to(h_last, (SUBLANES, LRU_BLOCK))
            hfin_ref[i, :, cs] = h_last


def _lru(p1, cinit, cw, cb, wri, br, bi, lam, h0, bsz, t_len):
    tt = min(t_len, 64)
    nt = t_len // tt
    bb = _batch_block(bsz, t_len)
    p3 = p1.reshape(bsz, t_len, P1_COLS)
    rows = lambda w: (lambda b, j: (b, j, w))
    per_seq = lambda b, j: (b, 0, 0)
    const2 = lambda b, j: (0, 0)
    o, h_fin = pl.pallas_call(
        functools.partial(_lru_kernel, tt=tt, bb=bb),
        grid=(bsz // bb, nt),
        in_specs=[
            pl.BlockSpec((bb, tt, 1024), rows(0)),
            pl.BlockSpec((bb, tt, 1024), rows(1)),
            pl.BlockSpec((bb, SUBLANES, LRU_WIDTH), per_seq),
            pl.BlockSpec((SUBLANES, LRU_WIDTH), const2),
            pl.BlockSpec((1, LRU_WIDTH), const2),
            pl.BlockSpec((LRU_BLOCKS, LRU_BLOCK, 2 * LRU_BLOCK), lambda b, j: (0, 0, 0)),
            pl.BlockSpec((1, LRU_WIDTH), const2),
            pl.BlockSpec((1, LRU_WIDTH), const2),
            pl.BlockSpec((1, LRU_WIDTH), const2),
            pl.BlockSpec((bb, 1, LRU_WIDTH), per_seq),
        ],
        out_specs=[pl.BlockSpec((bb, tt, LRU_WIDTH), rows(0)),
                   pl.BlockSpec((bb, 1, LRU_WIDTH), per_seq)],
        out_shape=[jax.ShapeDtypeStruct((bsz, t_len, LRU_WIDTH), BF16),
                   jax.ShapeDtypeStruct((bsz, 1, LRU_WIDTH), F32)],
        scratch_shapes=[pltpu.VMEM((bb, SUBLANES, LRU_WIDTH), F32), pltpu.VMEM((bb, SUBLANES, LRU_WIDTH), F32)],
        compiler_params=_cparams(("parallel", "arbitrary")),
        name="lru",
    )(p3, p3, cinit, cw, cb, wri, br, bi, lam, h0)
    return o.reshape(bsz * t_len, LRU_WIDTH), h_fin


def _expand_heads(x, e_ref, terms):
    parts = []
    rest = x
    for _ in range(terms):
        part = rest.astype(BF16).astype(F32)
        parts.append(part)
        rest = rest - part
    rows = x.shape[0]
    out = _dot(jnp.concatenate(parts, axis=0), e_ref[...])
    acc = out[:rows]
    for t in range(1, terms):
        acc = acc + out[t * rows:(t + 1) * rows]
    return acc


def _ssd_kernel(z_ref, xbc_ref, dt_ref, cinit_ref, cw_ref, cb_ref, dtb_ref, alog_ref, dd_ref, nw_ref,
                ej_ref, ep_ref, s0_ref, o_ref, s_ref, halo, st_t, *, c, mm, bb):
    j = pl.program_id(1)
    hg = SSD_HEADS // SSD_GROUPS
    gw = SSD_INNER // SSD_GROUPS
    pair = 2 * SSD_HEADDIM

    @pl.when(j == 0)
    def _():
        halo[...] = cinit_ref[...]
        for i in range(bb):
            for q in range(SSD_HEADS // 2):
                two = jnp.concatenate([s0_ref[i, 2 * q], s0_ref[i, 2 * q + 1]], axis=0)
                st_t[i, :, q * pair:(q + 1) * pair] = two.T

    wj = SSD_HEADS * c
    row_t = _row_iota((c, wj))
    lane_t = lax.broadcasted_iota(jnp.int32, (c, wj), 1) & (c - 1)
    causal = lane_t <= row_t
    diag = lane_t == row_t
    bd_rows = hg * c
    bd_mask = ((_row_iota((bd_rows, gw)) >> (c.bit_length() - 1))
               == (lax.broadcasted_iota(jnp.int32, (bd_rows, gw), 1) >> (SSD_HEADDIM.bit_length() - 1)))
    neg_a = -jnp.exp(alog_ref[...])
    gs = SSD_GROUPS * SSD_STATE
    for i in range(bb):
        u = xbc_ref[i]
        ext = jnp.concatenate([halo[i], u], axis=0)
        xc = cb_ref[...] + _shifted(ext, 3, c) * cw_ref[0:1, :]
        xc = xc + _shifted(ext, 2, c) * cw_ref[1:2, :]
        xc = xc + _shifted(ext, 1, c) * cw_ref[2:3, :]
        xc = _silu(xc + u * cw_ref[3:4, :])
        halo[i] = u[c - SUBLANES:c, :]

        xs = xc[:, :SSD_INNER]
        bm = xc[:, SSD_INNER:SSD_INNER + gs].astype(mm)
        cm = xc[:, SSD_INNER + gs:].astype(mm)
        dt = _softplus(dt_ref[i] + dtb_ref[...])
        cs = _cumsum_rows(dt * neg_a, c)
        cs_p = _expand_heads(cs, ep_ref, 3)
        dt_p = _expand_heads(dt, ep_ref, 2)
        cs_j = cs_p if c == SSD_HEADDIM else _expand_heads(cs, ej_ref, 3)
        cs_row = jnp.sum(jnp.where(diag, cs_j, 0.0), axis=0, keepdims=True)
        lmat = jnp.exp(jnp.where(causal, cs_j - cs_row, -jnp.inf))
        last_p = cs_p[c - 1:c, :]
        e_cs = jnp.exp(cs_p)
        e_last = jnp.exp(last_p)
        xdt = xs * dt_p
        xw = (xdt * jnp.exp(last_p - cs_p)).astype(mm)
        xdt = xdt.astype(mm)
        y_groups = []
        for g in range(SSD_GROUPS):
            gl = slice(g * gw, (g + 1) * gw)
            ss = slice(g * SSD_STATE, (g + 1) * SSD_STATE)
            cb_x = _dot_nt(cm[:, ss], jnp.concatenate([bm[:, ss]] * hg, axis=0))
            m_x = (cb_x * lmat[:, g * bd_rows:(g + 1) * bd_rows]).astype(mm)
            blocks = jnp.where(bd_mask, jnp.concatenate([xdt[:, gl]] * hg, axis=0), jnp.zeros((), mm))
            st = st_t[i, :, gl]
            y_groups.append(_dot(m_x, blocks) + e_cs[:, gl] * _dot(cm[:, ss], st.astype(mm)))
            st_t[i, :, gl] = st * e_last[:, gl] + _dot_tn(bm[:, ss], xw[:, gl])
        y = (jnp.concatenate(y_groups, axis=1) + dd_ref[...] * xs) * _silu(z_ref[i])
        for g in range(SSD_GROUPS):
            gl = slice(g * gw, (g + 1) * gw)
            o_ref[i, :, gl] = _rms(y[:, gl], nw_ref[:, gl], 1e-5).astype(BF16)

    @pl.when(j == pl.num_programs(1) - 1)
    def _():
        for i in range(bb):
            for q in range(SSD_HEADS // 2):
                two = st_t[i, :, q * pair:(q + 1) * pair].T
                s_ref[i, 2 * q] = two[:SSD_HEADDIM]
                s_ref[i, 2 * q + 1] = two[SSD_HEADDIM:]


def _ssd(p1, cinit, cw, cb, dtb, alog, dd, nw, s0, bsz, t_len):
    c = min(SSD_CHUNK, t_len)
    nt = t_len // c
    bb = _batch_block(bsz, t_len)
    mm = BF16 if c % 16 == 0 else F32
    p3 = p1.reshape(bsz, t_len, P1_COLS)
    rows = lambda w: (lambda b, j: (b, j, w))
    const2 = lambda b, j: (0, 0)
    st_spec = pl.BlockSpec((bb, SSD_HEADS, SSD_HEADDIM, SSD_STATE), lambda b, j: (b, 0, 0, 0))
    head = jnp.arange(LANES, dtype=jnp.int32)[:, None]
    expand = lambda width: (jnp.arange(SSD_HEADS * width, dtype=jnp.int32)[None, :] // width == head).astype(F32)
    o, s_new = pl.pallas_call(
        functools.partial(_ssd_kernel, c=c, mm=mm, bb=bb),
        grid=(bsz // bb, nt),
        in_specs=[
            pl.BlockSpec((bb, c, 1024), rows(2)),
            pl.BlockSpec((bb, c, SSD_CONV_CH), rows(2)),
            pl.BlockSpec((bb, c, LANES), rows(36)),
            pl.BlockSpec((bb, SUBLANES, SSD_CONV_CH), lambda b, j: (b, 0, 0)),
            pl.BlockSpec((SUBLANES, SSD_CONV_CH), const2),
            pl.BlockSpec((1, SSD_CONV_CH), const2),
            pl.BlockSpec((1, LANES), const2),
            pl.BlockSpec((1, LANES), const2),
            pl.BlockSpec((1, SSD_INNER), const2),
            pl.BlockSpec((1, SSD_INNER), const2),
            pl.BlockSpec((LANES, SSD_HEADS * c), const2),
            pl.BlockSpec((LANES, SSD_INNER), const2),
            st_spec,
        ],
        out_specs=[pl.BlockSpec((bb, c, SSD_INNER), rows(0)), st_spec],
        out_shape=[jax.ShapeDtypeStruct((bsz, t_len, SSD_INNER), BF16),
                   jax.ShapeDtypeStruct((bsz, SSD_HEADS, SSD_HEADDIM, SSD_STATE), F32)],
        scratch_shapes=[pltpu.VMEM((bb, SUBLANES, SSD_CONV_CH), F32),
                        pltpu.VMEM((bb, SSD_STATE, SSD_INNER), F32)],
        compiler_params=_cparams(("parallel", "arbitrary")),
        name="ssd",
    )(p3, p3, p3, cinit, cw, cb, dtb, alog, dd, nw, expand(c), expand(SSD_HEADDIM), s0)
    return o.reshape(bsz * t_len, SSD_INNER), s_new


def _pad_rows_to_8(x):
    return jnp.pad(x, ((0, 0), (SUBLANES - x.shape[1], 0), (0, 0)))


def _pad_lanes(x, width):
    return jnp.pad(x, ((0, 0), (0, width - x.shape[1])))


def _trunk(x, states, w):
    s_gla, s_rwkv, s_shift, s_lru, s_lru_conv, s_ssd, s_ssd_conv = states
    bsz, t_len, d = x.shape
    n = bsz * t_len
    x2 = x.reshape(n, d)

    p0 = _in_proj(x2, w["g_mix0"], w["w_in0"], P0_TN)
    o_gla, s_gla_new = _gla(p0, s_gla, w["gla_wa2"], w["gla_ba"], w["gla_gn"], bsz, t_len)
    if bsz * RWKV_HEADS == LANES and t_len % RELAYOUT_STEPS == 0:
        y_rwkv, g, s_rwkv_new = _rwkv_paired(p0, s_shift, s_rwkv, w, bsz, t_len)
    else:
        y_rwkv, g, s_rwkv_new = _rwkv_general(p0, s_shift, s_rwkv, w, bsz, t_len)
    shift_new = p0.reshape(bsz, t_len, P0_COLS)[:, -1, 3072:3072 + RWKV_COLS]
    x2 = _out_ffn(x2, o_gla, y_rwkv, g, w["w_out0"], w["g_ffn0"], w["wg0"], w["wu0"], w["wd0"], None)

    p1 = _in_proj(x2, w["g_mix1"], w["w_in1"], P1_TN)
    lru_out, h_lru = _lru(p1, _pad_rows_to_8(s_lru_conv), w["lru_cw"], w["lru_cb"], w["lru_wri"], w["lru_br"],
                          w["lru_bi"], w["lru_lam"], s_lru.reshape(bsz, 1, LRU_WIDTH), bsz, t_len)
    y_ssd, s_ssd_new = _ssd(p1, _pad_rows_to_8(s_ssd_conv), w["ssd_cw"], w["ssd_cb"], w["ssd_dtb"], w["ssd_alog"],
                            w["ssd_dd"], w["ssd_nw"], s_ssd, bsz, t_len)
    keep = CONV_W - 1
    p1_tail = p1.reshape(bsz, t_len, P1_COLS)[:, -keep:]
    lru_conv_new = p1_tail[:, :, 1024:2048]
    ssd_conv_new = p1_tail[:, :, 3072:3072 + SSD_CONV_CH]
    y = _out_ffn(x2, lru_out, y_ssd, None, w["w_out1"], w["g_ffn1"], w["wg1"], w["wu1"], w["wd1"], w["g_final"])
    return y.reshape(bsz, t_len, d), (s_gla_new, s_rwkv_new, shift_new, h_lru.reshape(bsz, LRU_WIDTH),
                                      lru_conv_new, s_ssd_new, ssd_conv_new)


def kernel(x_prompt, x_sample, state_gla, state_rwkv, state_rwkv_shift, state_lru, state_lru_conv, state_ssd, state_ssd_conv, w_in0, gla_w_a2, gla_b_a, gla_g_norm, rwkv_mu, rwkv_w0, rwkv_w2, rwkv_a0, rwkv_a2, rwkv_g2, rwkv_k_k, rwkv_k_a, rwkv_r_k, rwkv_ln_w, rwkv_ln_b, w_out0, w_in1, lru_conv_w, lru_conv_b, lru_w_r, lru_b_r, lru_w_i, lru_b_i, lru_lambda, ssd_conv_w, ssd_conv_b, ssd_dt_bias, ssd_a_log, ssd_d, ssd_norm_w, w_out1, g_mix, g_ffn, w_ffn_gate, w_ffn_up, w_ffn_down, g_final):
    row = lambda p: p.reshape(1, -1).astype(F32)
    w0 = jnp.concatenate([w_in0[:, :2048], w_in0[:, 2064:3088], w_in0[:, 3088:],
                          _pad_lanes(w_in0[:, 2048:2064], LANES)], axis=1)
    zeros64 = jnp.zeros((64, RWKV_WIDTH), F32)
    rwkv_wa = jnp.concatenate([jnp.concatenate([rwkv_w2, zeros64], axis=1),
                               jnp.concatenate([zeros64, rwkv_a2], axis=1)], axis=0)
    w = {
        "w_in0": w0.astype(BF16),
        "w_in1": _pad_lanes(w_in1, P1_COLS).astype(BF16),
        "g_mix0": g_mix[0], "g_mix1": g_mix[1], "g_ffn0": g_ffn[0], "g_ffn1": g_ffn[1], "g_final": g_final,
        "gla_wa2": jnp.pad(gla_w_a2, ((0, LANES - GLA_GATE_RANK), (0, 0))).astype(BF16),
        "gla_ba": row(gla_b_a), "gla_gn": row(gla_g_norm),
        "rwkv_mu": row(rwkv_mu), "rwkv_w0": row(rwkv_w0), "rwkv_a0": row(rwkv_a0),
        "rwkv_wa": rwkv_wa.astype(BF16), "rwkv_g2": rwkv_g2.astype(BF16),
        "rwkv_tiles": [_chain_tile(p) for p in (rwkv_k_k, rwkv_k_a, rwkv_r_k, rwkv_ln_w, rwkv_ln_b)],
        "rwkv_tiles_paired": [_chain_tile_paired(p)
                              for p in (rwkv_k_k, rwkv_k_a, rwkv_r_k, rwkv_ln_w, rwkv_ln_b)],
        "w_out0": w_out0.astype(BF16), "w_out1": w_out1.astype(BF16),
        "wg0": w_ffn_gate[0].astype(BF16), "wu0": w_ffn_up[0].astype(BF16), "wd0": w_ffn_down[0].astype(BF16),
        "wg1": w_ffn_gate[1].astype(BF16), "wu1": w_ffn_up[1].astype(BF16), "wd1": w_ffn_down[1].astype(BF16),
        "lru_cw": jnp.pad(lru_conv_w, ((0, SUBLANES - CONV_W), (0, 0))), "lru_cb": row(lru_conv_b),
        "lru_wri": jnp.concatenate([lru_w_r, lru_w_i], axis=2).astype(BF16),
        "lru_br": row(lru_b_r), "lru_bi": row(lru_b_i), "lru_lam": row(lru_lambda),
        "ssd_cw": jnp.pad(ssd_conv_w, ((0, SUBLANES - CONV_W), (0, 0))), "ssd_cb": row(ssd_conv_b),
        "ssd_dtb": _pad_lanes(row(ssd_dt_bias), LANES), "ssd_alog": _pad_lanes(row(ssd_a_log), LANES),
        "ssd_dd": row(jnp.repeat(ssd_d, SSD_HEADDIM)), "ssd_nw": row(ssd_norm_w),
    }
    bp = x_prompt.shape[0]
    prompt_init = (
        jnp.zeros((bp, GLA_HEADS, GLA_DK, GLA_DV), F32),
        jnp.zeros((bp, RWKV_HEADS, RWKV_HEAD, RWKV_HEAD), F32),
        jnp.zeros((bp, RWKV_COLS), F32),
        jnp.zeros((bp, LRU_WIDTH), F32),
        jnp.zeros((bp, CONV_W - 1, LRU_WIDTH), F32),
        jnp.zeros((bp, SSD_HEADS, SSD_HEADDIM, SSD_STATE), F32),
        jnp.zeros((bp, CONV_W - 1, SSD_CONV_CH), F32),
    )
    sample_init = (state_gla, state_rwkv, state_rwkv_shift, state_lru, state_lru_conv, state_ssd, state_ssd_conv)
    y_prompt, p_states = _trunk(x_prompt, prompt_init, w)
    y_sample, s_states = _trunk(x_sample, sample_init, w)
    return (y_prompt, y_sample, *p_states, *s_states)
```

```python
import functools

import jax
import jax.numpy as jnp
from jax import lax
from jax.experimental import pallas as pl
from jax.experimental.pallas import tpu as pltpu

F32 = jnp.float32
BF16 = jnp.bfloat16

D_MODEL = 1024
NORM_EPS = 1e-6
GLA_HEADS = 4
GLA_DK = 128
GLA_DV = 256
GLA_GATE_RANK = 16
GLA_GATE_NORM = 16.0
GLA_CHUNK = 64
RWKV_HEAD = 64
RWKV_HEADS = 16
RWKV_WIDTH = 1024
RWKV_LORA = 256
RWKV_COLS = 3 * RWKV_WIDTH + RWKV_LORA
RWKV_GN_EPS = 64e-5
LRU_WIDTH = 1024
LRU_BLOCKS = 8
LRU_BLOCK = 128
LRU_C = 8.0
CONV_W = 4
SSD_INNER = 1024
SSD_HEADDIM = 64
SSD_HEADS = 16
SSD_GROUPS = 2
SSD_STATE = 128
SSD_CHUNK = 64
SSD_CONV_CH = SSD_INNER + 2 * SSD_GROUPS * SSD_STATE
D_FF = 2816

LANES = 128
SUBLANES = 8
VMEM_LIMIT_BYTES = 56 * 1024 * 1024

P0_COLS = 6528
P0_TN = 2176
P1_COLS = 4864
P1_TN = 2432


def _cparams(semantics):
    return pltpu.CompilerParams(dimension_semantics=semantics, vmem_limit_bytes=VMEM_LIMIT_BYTES)


def _dot(a, b):
    return jnp.dot(a, b, preferred_element_type=F32)


def _dot_nt(a, b):
    return lax.dot_general(a, b, (((1,), (1,)), ((), ())), preferred_element_type=F32)


def _dot_tn(a, b):
    return lax.dot_general(a, b, (((0,), (0,)), ((), ())), preferred_element_type=F32)


def _softplus(x):
    return jnp.maximum(x, 0.0) + jnp.log1p(jnp.exp(-jnp.abs(x)))


def _silu(x):
    return x * jax.nn.sigmoid(x)


def _gelu_tanh(x):
    return 0.5 * x * (1.0 + jnp.tanh(0.7978845608028654 * (x + 0.044715 * (x * x * x))))


def _rms(x, g, eps):
    return x * lax.rsqrt(jnp.mean(x * x, axis=-1, keepdims=True) + eps) * g


def _row_iota(shape):
    return lax.broadcasted_iota(jnp.int32, shape, 0)


def _cumsum_rows(x, seg):
    rows = _row_iota(x.shape) & (seg - 1)
    d = 1
    while d < seg:
        x = x + jnp.where(rows >= d, pltpu.roll(x, d, axis=0), 0.0)
        d *= 2
    return x


def _shifted(ext, k, rows):
    return pltpu.roll(ext, k, axis=0)[SUBLANES:SUBLANES + rows]


def _in_proj_kernel(x_ref, g_ref, w_ref, o_ref, h_scr):
    @pl.when(pl.program_id(1) == 0)
    def _():
        h_scr[...] = _rms(x_ref[...], g_ref[...], NORM_EPS).astype(BF16)

    o_ref[...] = _dot(h_scr[...], w_ref[...])


def _in_proj(x2d, g, w, tn):
    n, d = x2d.shape
    cols = w.shape[1]
    tm = min(n, 1024)
    return pl.pallas_call(
        _in_proj_kernel,
        grid=(n // tm, cols // tn),
        in_specs=[
            pl.BlockSpec((tm, d), lambda i, j: (i, 0)),
            pl.BlockSpec((1, d), lambda i, j: (0, 0)),
            pl.BlockSpec((d, tn), lambda i, j: (0, j)),
        ],
        out_specs=pl.BlockSpec((tm, tn), lambda i, j: (i, j)),
        out_shape=jax.ShapeDtypeStruct((n, cols), F32),
        scratch_shapes=[pltpu.VMEM((tm, d), BF16)],
        compiler_params=_cparams(("parallel", "arbitrary")),
        name="in_proj",
    )(x2d, g.reshape(1, d), w)


def _out_ffn_kernel(*refs, has_gate, final_norm):
    refs = list(refs)
    x_ref, oa_ref, ob_ref = refs[:3]
    pos = 3
    gate_ref = None
    if has_gate:
        gate_ref = refs[pos]
        pos += 1
    woa_ref, wob_ref, gffn_ref, wg_ref, wu_ref, wd_ref = refs[pos:pos + 6]
    pos += 6
    gfin_ref = None
    if final_norm:
        gfin_ref = refs[pos]
        pos += 1
    out_ref, h_scr = refs[pos:pos + 2]

    k = pl.program_id(1)

    @pl.when(k == 0)
    def _():
        ob = ob_ref[...]
        if has_gate:
            ob = (ob * gate_ref[...]).astype(BF16)
        x1 = x_ref[...] + (_dot(oa_ref[...], woa_ref[...]) + _dot(ob, wob_ref[...]))
        out_ref[...] = x1
        h_scr[...] = _rms(x1, gffn_ref[...], NORM_EPS).astype(BF16)

    h = h_scr[...]
    act = (_silu(_dot(h, wg_ref[...])) * _dot(h, wu_ref[...])).astype(BF16)
    out_ref[...] += _dot(act, wd_ref[...])

    if final_norm:
        @pl.when(k == pl.num_programs(1) - 1)
        def _():
            out_ref[...] = _rms(out_ref[...], gfin_ref[...], NORM_EPS)


def _out_ffn(x2d, oa, ob, gate, wo, g_ffn, wg, wu, wd, g_final):
    n, d = x2d.shape
    has_gate = gate is not None
    final_norm = g_final is not None
    tm = min(n, 512)
    tf = D_FF // 2
    row = lambda i, k: (i, 0)
    const = lambda i, k: (0, 0)
    args = [x2d, oa, ob]
    in_specs = [pl.BlockSpec((tm, d), row)] * 3
    if has_gate:
        args.append(gate)
        in_specs.append(pl.BlockSpec((tm, d), row))
    half = wo.shape[0] // 2
    args += [wo, wo, g_ffn.reshape(1, d), wg, wu, wd]
    in_specs += [
        pl.BlockSpec((half, d), const),
        pl.BlockSpec((half, d), lambda i, k: (1, 0)),
        pl.BlockSpec((1, d), const),
        pl.BlockSpec((d, tf), lambda i, k: (0, k)),
        pl.BlockSpec((d, tf), lambda i, k: (0, k)),
        pl.BlockSpec((tf, d), lambda i, k: (k, 0)),
    ]
    if final_norm:
        args.append(g_final.reshape(1, d))
        in_specs.append(pl.BlockSpec((1, d), const))
    return pl.pallas_call(
        functools.partial(_out_ffn_kernel, has_gate=has_gate, final_norm=final_norm),
        grid=(n // tm, D_FF // tf),
        in_specs=in_specs,
        out_specs=pl.BlockSpec((tm, d), row),
        out_shape=jax.ShapeDtypeStruct((n, d), F32),
        scratch_shapes=[pltpu.VMEM((tm, d), BF16)],
        compiler_params=_cparams(("parallel", "arbitrary")),
        name="out_ffn",
    )(*args)


def _gla_kernel(q_ref, k_ref, v_ref, og_ref, al_ref, wa2_ref, ba_ref, gn_ref, s0_ref,
                o_ref, sout_ref, st_scr, *, c, mm, bb):
    n = pl.program_id(1)

    @pl.when(n == 0)
    def _():
        for i in range(bb):
            for h in range(GLA_HEADS):
                st_scr[i * GLA_HEADS + h] = s0_ref[i, h].T

    mask = _row_iota((c, c)) >= lax.broadcasted_iota(jnp.int32, (c, c), 1)
    gn = gn_ref[...]
    for i in range(bb):
        la_all = _dot(al_ref[i].astype(BF16), wa2_ref[...]) + ba_ref[...]
        la_all = -_softplus(-la_all) * (1.0 / GLA_GATE_NORM)
        b_all = _cumsum_rows(la_all, c)
        for h in range(GLA_HEADS):
            ks = slice(h * GLA_DK, (h + 1) * GLA_DK)
            vs = slice(h * GLA_DV, (h + 1) * GLA_DV)
            b = b_all[:, ks]
            b_last = b[c - 1:c, :]
            q = q_ref[i, :, ks] * (GLA_DK ** -0.5)
            k = k_ref[i, :, ks]
            v = v_ref[i, :, vs].astype(mm)
            qd = (q * jnp.exp(b)).astype(mm)
            kd = (k * jnp.exp(-b)).astype(mm)
            kc = (k * jnp.exp(b_last - b)).astype(mm)
            scores = jnp.where(mask, _dot_nt(qd, kd), 0.0)
            st = st_scr[i * GLA_HEADS + h]
            o = _dot(scores.astype(mm), v) + _dot_nt(qd, st.astype(mm))
            st_scr[i * GLA_HEADS + h] = st * jnp.exp(b_last) + _dot_tn(v, kc)
            o = _rms(o, gn, 1e-5)
            o_ref[i, :, vs] = (o * _silu(og_ref[i, :, vs])).astype(BF16)

    @pl.when(n == pl.num_programs(1) - 1)
    def _():
        for i in range(bb):
            for h in range(GLA_HEADS):
                sout_ref[i, h] = st_scr[i * GLA_HEADS + h].T


def _batch_block(bsz, t_len):
    return 2 if t_len >= 64 else min(bsz, 8)


def _gla(p0, s0, wa2, ba, gn, bsz, t_len):
    c = min(GLA_CHUNK, t_len)
    nt = t_len // c
    bb = _batch_block(bsz, t_len)
    mm = BF16 if c % 16 == 0 else F32
    p3 = p0.reshape(bsz, t_len, P0_COLS)
    rows = lambda w: (lambda b, j: (b, j, w))
    const = lambda b, j: (0, 0)
    st_spec = pl.BlockSpec((bb, GLA_HEADS, GLA_DK, GLA_DV), lambda b, j: (b, 0, 0, 0))
    o, s_new = pl.pallas_call(
        functools.partial(_gla_kernel, c=c, mm=mm, bb=bb),
        grid=(bsz // bb, nt),
        in_specs=[
            pl.BlockSpec((bb, c, 512), rows(0)),
            pl.BlockSpec((bb, c, 512), rows(1)),
            pl.BlockSpec((bb, c, 1024), rows(1)),
            pl.BlockSpec((bb, c, 1024), rows(2)),
            pl.BlockSpec((bb, c, LANES), rows(50)),
            pl.BlockSpec((LANES, 512), const),
            pl.BlockSpec((1, 512), const),
            pl.BlockSpec((1, GLA_DV), const),
            st_spec,
        ],
        out_specs=[pl.BlockSpec((bb, c, 1024), rows(0)), st_spec],
        out_shape=[jax.ShapeDtypeStruct((bsz, t_len, 1024), BF16),
                   jax.ShapeDtypeStruct((bsz, GLA_HEADS, GLA_DK, GLA_DV), F32)],
        scratch_shapes=[pltpu.VMEM((bb * GLA_HEADS, GLA_DV, GLA_DK), F32)],
        compiler_params=_cparams(("parallel", "arbitrary")),
        name="gla",
    )(p3, p3, p3, p3, p3, wa2, ba, gn, s0)
    return o.reshape(bsz * t_len, 1024), s_new


def _rwkv_prep_kernel(r_ref, k_ref, v_ref, l_ref, ir_ref, ik_ref, iv_ref, il_ref,
                      mu_ref, w0_ref, a0_ref, wa_ref, g2_ref,
                      ro_ref, wo_ref, ko_ref, vo_ref, ao_ref, go_ref, carry,
                      *, tm, t_len, multi_seq):
    j = pl.program_id(1)
    cols = ((0, 1024), (1024, 2048), (2048, 3072), (3072, RWKV_COLS))

    if not multi_seq:
        @pl.when(j == 0)
        def _():
            for (lo, hi), iref in zip(cols, (ir_ref, ik_ref, iv_ref, il_ref)):
                carry[:, lo:hi] = iref[...]

    def mixed(x_ref, i_ref, lo, hi):
        x = x_ref[...]
        rolled = pltpu.roll(x, 1, axis=0)
        rows = _row_iota(x.shape)
        if multi_seq:
            prev = jnp.where((rows & (t_len - 1)) == 0, i_ref[...], rolled)
        else:
            prev = jnp.where(rows == 0, carry[SUBLANES - 1:SUBLANES, lo:hi], rolled)
            carry[:, lo:hi] = x[tm - SUBLANES:tm, :]
        return x + (prev - x) * mu_ref[:, lo:hi]

    r = mixed(r_ref, ir_ref, *cols[0])
    k = mixed(k_ref, ik_ref, *cols[1])
    v = mixed(v_ref, iv_ref, *cols[2])
    lo_rank = mixed(l_ref, il_ref, *cols[3])

    wa_in = lo_rank[:, :LANES]
    lane = lax.broadcasted_iota(jnp.int32, wa_in.shape, 1)
    wa_in = jnp.where(lane < 64, jnp.tanh(wa_in), wa_in).astype(BF16)
    wa = _dot(wa_in, wa_ref[...])
    g = _dot(jax.nn.sigmoid(lo_rank[:, LANES:]).astype(BF16), g2_ref[...])
    w = -_softplus(-(w0_ref[...] + wa[:, :RWKV_WIDTH])) - 0.5
    ro_ref[...] = r
    wo_ref[...] = -jnp.exp(w)
    ko_ref[...] = k
    vo_ref[...] = v
    ao_ref[...] = jax.nn.sigmoid(a0_ref[...] + wa[:, RWKV_WIDTH:])
    go_ref[...] = g


def _rwkv_prep(p0, shift8, mu, w0, a0, wa, g2, bsz, t_len):
    n = bsz * t_len
    multi_seq = t_len < 256
    tm = min(n, 512) if multi_seq else 256
    nt = 1 if multi_seq else t_len // tm
    nb = n // tm if multi_seq else bsz
    ti = tm if multi_seq else SUBLANES
    rows = lambda w: (lambda i, j: (i * nt + j, w))
    init = lambda w: (lambda i, j: (i, w))
    const = lambda i, j: (0, 0)
    out = jax.ShapeDtypeStruct((n, RWKV_WIDTH), F32)
    return pl.pallas_call(
        functools.partial(_rwkv_prep_kernel, tm=tm, t_len=t_len, multi_seq=multi_seq),
        grid=(nb, nt),
        in_specs=[
            pl.BlockSpec((tm, 1024), rows(3)),
            pl.BlockSpec((tm, 1024), rows(4)),
            pl.BlockSpec((tm, 1024), rows(5)),
            pl.BlockSpec((tm, RWKV_LORA), rows(24)),
            pl.BlockSpec((ti, 1024), init(0)),
            pl.BlockSpec((ti, 1024), init(1)),
            pl.BlockSpec((ti, 1024), init(2)),
            pl.BlockSpec((ti, RWKV_LORA), init(12)),
            pl.BlockSpec((1, RWKV_COLS), const),
            pl.BlockSpec((1, RWKV_WIDTH), const),
            pl.BlockSpec((1, RWKV_WIDTH), const),
            pl.BlockSpec((LANES, 2 * RWKV_WIDTH), const),
            pl.BlockSpec((LANES, RWKV_WIDTH), const),
        ],
        out_specs=[pl.BlockSpec((tm, RWKV_WIDTH), rows(0))] * 6,
        out_shape=[out] * 6,
        scratch_shapes=[pltpu.VMEM((SUBLANES, RWKV_COLS), F32)],
        compiler_params=_cparams(("parallel", "arbitrary")),
        name="rwkv_prep",
    )(p0, p0, p0, p0, shift8, shift8, shift8, shift8, mu, w0, a0, wa, g2)


def _rwkv_scan_kernel(r_ref, lw_ref, k_ref, a_ref, v_ref, kk_ref, ka_ref, rk_ref, lnw_ref, lnb_ref,
                      s0_ref, y_ref, s_ref, kb_scr, kp_scr, rs_scr, kn_scr, bonus_scr, grow_scr, *, tt):
    @pl.when(pl.program_id(1) == 0)
    def _():
        s_ref[...] = s0_ref[...]

    def tile_row(ref, j):
        return ref[pl.ds(j, 1), :]

    def bcast(x):
        return jnp.broadcast_to(x, (RWKV_HEAD, LANES))

    norm2 = jnp.zeros((tt, LANES), F32)
    for j in range(RWKV_HEAD):
        kj = k_ref[0, j] * tile_row(kk_ref, j)
        norm2 = norm2 + kj * kj
    norm = jnp.maximum(jnp.sqrt(norm2), 1e-12)
    bonus = jnp.zeros((tt, LANES), F32)
    s_kk = jnp.zeros((RWKV_HEAD, LANES), F32)
    for j in range(RWKV_HEAD):
        k = k_ref[0, j]
        a = a_ref[0, j]
        r = r_ref[0, j]
        kk = k * tile_row(kk_ref, j) / norm
        kp = k * (1.0 + (a - 1.0) * tile_row(ka_ref, j))
        log_g = _cumsum_rows(lw_ref[0, j], tt)
        grow = jnp.exp(log_g)
        shrink = jnp.exp(-log_g)
        kb_scr[j] = kk * a * shrink
        kp_scr[j] = kp * shrink
        rs_scr[j] = r * grow
        kn_scr[j] = pltpu.roll(kk, tt - 1, axis=0) * grow
        grow_scr[pl.ds(j, 1), :] = grow[tt - 1:tt, :]
        bonus = bonus + r * kp * tile_row(rk_ref, j)
        s_kk = s_kk + s_ref[j] * bcast(kk[0:1, :])
    bonus_scr[...] = bonus

    def row(ref, j, s):
        return bcast(ref[j, pl.ds(s, 1), :])

    def step(s, s_kk):
        sa = -s_kk
        v = v_ref[s]
        y = jnp.zeros_like(v)
        s_kk_next = jnp.zeros_like(v)
        for j in range(RWKV_HEAD):
            sn = s_ref[j] + sa * row(kb_scr, j, s) + v * row(kp_scr, j, s)
            s_ref[j] = sn
            y = y + sn * row(rs_scr, j, s)
            s_kk_next = s_kk_next + sn * row(kn_scr, j, s)
        mu = jnp.mean(y, axis=0, keepdims=True)
        yc = y - mu
        var = jnp.mean(yc * yc, axis=0, keepdims=True)
        y_ref[s] = yc * lax.rsqrt(var + RWKV_GN_EPS) * lnw_ref[...] + lnb_ref[...] + bonus_scr[pl.ds(s, 1), :] * v
        return s_kk_next

    lax.fori_loop(0, tt, step, s_kk)
    for j in range(RWKV_HEAD):
        s_ref[j] = s_ref[j] * bcast(tile_row(grow_scr, j))


def _rwkv_scan(r, lw, k, a, v, tiles, s0, tt):
    nb, _, _, nc = r.shape
    seq = pl.BlockSpec((1, RWKV_HEAD, tt, LANES), lambda g, t: (t, 0, 0, g))
    steps = pl.BlockSpec((tt, RWKV_HEAD, LANES), lambda g, t: (t, 0, g))
    tile = pl.BlockSpec((RWKV_HEAD, LANES), lambda g, t: (0, 0))
    st = pl.BlockSpec((RWKV_HEAD, RWKV_HEAD, LANES), lambda g, t: (0, 0, g))
    return pl.pallas_call(
        functools.partial(_rwkv_scan_kernel, tt=tt),
        grid=(nc // LANES, nb),
        in_specs=[seq] * 4 + [steps] + [tile] * 5 + [st],
        out_specs=[steps, st],
        out_shape=[jax.ShapeDtypeStruct((nb * tt, RWKV_HEAD, nc), F32),
                   jax.ShapeDtypeStruct((RWKV_HEAD, RWKV_HEAD, nc), F32)],
        scratch_shapes=[pltpu.VMEM((RWKV_HEAD, tt, LANES), F32)] * 4
        + [pltpu.VMEM((tt, LANES), F32), pltpu.VMEM((RWKV_HEAD, LANES), F32)],
        compiler_params=_cparams(("parallel", "arbitrary")),
        name="rwkv_scan",
    )(r, lw, k, a, v, *tiles, s0)


RELAYOUT_PITCH = LANES + SUBLANES
RELAYOUT_STEPS = LANES
SCAN_STEPS = 32


def _chain_input_kernel(*refs, kind, step_major):
    x_ref, init_ref, mu_ref = refs[:3]
    bias_ref = wmat_ref = z_scr = None
    if kind == "plain":
        o_ref, carry, z_scr = refs[3:]
    elif kind == "gate":
        wmat_ref, o_ref, carry = refs[3:]
    else:
        bias_ref, wmat_ref, o_ref, carry, z_scr = refs[3:]

    @pl.when(pl.program_id(0) == 0)
    def _():
        carry[...] = init_ref[...]

    nseq, steps, width = x_ref.shape
    pairs = RWKV_HEADS // 2
    rows = _row_iota((steps, width))
    for b in range(nseq):
        x = x_ref[b]
        prev = jnp.where(rows == 0, carry[b, SUBLANES - 1:SUBLANES, :], pltpu.roll(x, 1, axis=0))
        carry[b] = x[steps - SUBLANES:, :]
        mixed = x + (prev - x) * mu_ref[...]
        if kind == "plain":
            val = mixed
        elif kind == "gate":
            o_ref[b] = _dot(jax.nn.sigmoid(mixed[:, LANES:]).astype(BF16), wmat_ref[...])
            continue
        else:
            low = mixed[:, :LANES]
            lane = lax.broadcasted_iota(jnp.int32, low.shape, 1)
            low = jnp.where(lane < 64, jnp.tanh(low), low).astype(BF16)
            pre = bias_ref[...] + _dot(low, wmat_ref[...])
            val = -jnp.exp(-_softplus(-pre) - 0.5) if kind == "decay" else jax.nn.sigmoid(pre)
        for hp in range(pairs):
            q = b * pairs + hp
            z_scr[q * RELAYOUT_PITCH:q * RELAYOUT_PITCH + LANES, :] = val[:, hp * LANES:(hp + 1) * LANES].T
    if kind == "gate":
        return
    half = nseq * pairs
    for j in range(RWKV_HEAD):
        m = jnp.concatenate([z_scr[pl.ds(j, half, stride=RELAYOUT_PITCH), :],
                             z_scr[pl.ds(RWKV_HEAD + j, half, stride=RELAYOUT_PITCH), :]], axis=0)
        mt = m.T
        if step_major:
            o_ref[pl.ds(j, steps, stride=RWKV_HEAD), :] = mt
        else:
            for tq in range(steps // SCAN_STEPS):
                o_ref[tq, j] = mt[tq * SCAN_STEPS:(tq + 1) * SCAN_STEPS]


def _chain_input(p3, shift3, mu, col, init_col, kind, step_major=False, bias=None, wmat=None):
    bsz, t_len, _ = p3.shape
    width = mu.shape[1]
    const = lambda t: (0, 0)
    args = [p3, shift3, mu]
    in_specs = [pl.BlockSpec((bsz, RELAYOUT_STEPS, width), lambda t: (0, t, col)),
                pl.BlockSpec((bsz, SUBLANES, width), lambda t: (0, 0, init_col)),
                pl.BlockSpec((1, width), const)]
    if bias is not None:
        args.append(bias)
        in_specs.append(pl.BlockSpec(bias.shape, const))
    if wmat is not None:
        args.append(wmat)
        in_specs.append(pl.BlockSpec(wmat.shape, const))
    scratch = [pltpu.VMEM((bsz, SUBLANES, width), F32)]
    if kind == "gate":
        out_spec = pl.BlockSpec((bsz, RELAYOUT_STEPS, RWKV_WIDTH), lambda t: (0, t, 0))
        out_shape = jax.ShapeDtypeStruct((bsz, t_len, RWKV_WIDTH), F32)
    else:
        scratch.append(pltpu.VMEM((bsz * (RWKV_HEADS // 2) * RELAYOUT_PITCH, LANES), F32))
        if step_major:
            out_spec = pl.BlockSpec((RELAYOUT_STEPS * RWKV_HEAD, LANES), lambda t: (t, 0))
            out_shape = jax.ShapeDtypeStruct((t_len * RWKV_HEAD, LANES), F32)
        else:
            per = RELAYOUT_STEPS // SCAN_STEPS
            out_spec = pl.BlockSpec((per, RWKV_HEAD, SCAN_STEPS, LANES), lambda t: (t, 0, 0, 0))
            out_shape = jax.ShapeDtypeStruct((t_len // SCAN_STEPS, RWKV_HEAD, SCAN_STEPS, LANES), F32)
    return pl.pallas_call(
        functools.partial(_chain_input_kernel, kind=kind, step_major=step_major),
        grid=(t_len // RELAYOUT_STEPS,),
        in_specs=in_specs,
        out_specs=out_spec,
        out_shape=out_shape,
        scratch_shapes=scratch,
        compiler_params=_cparams(("arbitrary",)),
        name="rwkv_" + kind,
    )(*args)


def _from_chains_kernel(y_ref, g_ref, o_ref, z_scr):
    nseq = o_ref.shape[0]
    pairs = RWKV_HEADS // 2
    half = nseq * pairs
    for i in range(RWKV_HEAD):
        mt = y_ref[pl.ds(i, RELAYOUT_STEPS, stride=RWKV_HEAD), :].T
        z_scr[pl.ds(i, half, stride=RELAYOUT_PITCH), :] = mt[:half]
        z_scr[pl.ds(RWKV_HEAD + i, half, stride=RELAYOUT_PITCH), :] = mt[half:]
    for b in range(nseq):
        for hp in range(pairs):
            q = b * pairs + hp
            lanes = slice(hp * LANES, (hp + 1) * LANES)
            y = z_scr[q * RELAYOUT_PITCH:q * RELAYOUT_PITCH + LANES, :].T
            o_ref[b, :, lanes] = (y * g_ref[b, :, lanes]).astype(BF16)


def _from_chains(y2, g3):
    bsz, t_len, _ = g3.shape
    natural = pl.BlockSpec((bsz, RELAYOUT_STEPS, RWKV_WIDTH), lambda t: (0, t, 0))
    return pl.pallas_call(
        _from_chains_kernel,
        grid=(t_len // RELAYOUT_STEPS,),
        in_specs=[pl.BlockSpec((RELAYOUT_STEPS * RWKV_HEAD, LANES), lambda t: (t, 0)), natural],
        out_specs=natural,
        out_shape=jax.ShapeDtypeStruct((bsz, t_len, RWKV_WIDTH), BF16),
        scratch_shapes=[pltpu.VMEM((bsz * (RWKV_HEADS // 2) * RELAYOUT_PITCH, LANES), F32)],
        compiler_params=_cparams(("arbitrary",)),
        name="from_chains",
    )(y2, g3)


def _rwkv_paired(p0, s_shift, s_rwkv, w, bsz, t_len):
    pairs = RWKV_HEADS // 2
    p3 = p0.reshape(bsz, t_len, P0_COLS)
    shift3 = jnp.broadcast_to(s_shift[:, None, :], (bsz, SUBLANES, RWKV_COLS))
    mu = w["rwkv_mu"]
    mu_main = [mu[:, i * RWKV_WIDTH:(i + 1) * RWKV_WIDTH] for i in range(3)]
    mu_low = mu[:, 3 * RWKV_WIDTH:]
    wa = w["rwkv_wa"]
    low_col = 3 * RWKV_WIDTH // RWKV_LORA
    r = _chain_input(p3, shift3, mu_main[0], 3, 0, "plain")
    k = _chain_input(p3, shift3, mu_main[1], 4, 1, "plain")
    v = _chain_input(p3, shift3, mu_main[2], 5, 2, "plain", step_major=True)
    lw = _chain_input(p3, shift3, mu_low, 24, low_col, "decay", bias=w["rwkv_w0"], wmat=wa[:, :RWKV_WIDTH])
    a = _chain_input(p3, shift3, mu_low, 24, low_col, "a", bias=w["rwkv_a0"], wmat=wa[:, RWKV_WIDTH:])
    g = _chain_input(p3, shift3, mu_low, 24, low_col, "gate", wmat=w["rwkv_g2"])
    s0 = s_rwkv.reshape(bsz, pairs, 2, RWKV_HEAD, RWKV_HEAD).transpose(4, 3, 2, 0, 1)
    y_c, s_c = _rwkv_scan(r, lw, k, a, v.reshape(t_len, RWKV_HEAD, LANES), w["rwkv_tiles_paired"],
                          s0.reshape(RWKV_HEAD, RWKV_HEAD, LANES), SCAN_STEPS)
    y = _from_chains(y_c.reshape(t_len * RWKV_HEAD, LANES), g)
    s_new = s_c.reshape(RWKV_HEAD, RWKV_HEAD, 2, bsz, pairs).transpose(3, 4, 2, 1, 0)
    return y.reshape(bsz * t_len, RWKV_WIDTH), None, s_new.reshape(bsz, RWKV_HEADS, RWKV_HEAD, RWKV_HEAD)


def _rwkv_general(p0, s_shift, s_rwkv, w, bsz, t_len):
    shift8 = jnp.broadcast_to(s_shift[:, None, :], (bsz, SUBLANES, RWKV_COLS)).reshape(bsz * SUBLANES, RWKV_COLS)
    r, lw, k, v, a, g = _rwkv_prep(p0, shift8, w["rwkv_mu"], w["rwkv_w0"], w["rwkv_a0"], w["rwkv_wa"],
                                   w["rwkv_g2"], bsz, t_len)
    row_seqs = (r, lw, k, a)
    tt = min(t_len, SCAN_STEPS)
    nb = t_len // tt
    nc = bsz * RWKV_HEADS
    rows = [t.reshape(bsz, nb, tt, RWKV_HEADS, RWKV_HEAD).transpose(1, 4, 2, 0, 3).reshape(nb, RWKV_HEAD, tt, nc)
            for t in row_seqs]
    v_c = v.reshape(bsz, t_len, RWKV_HEADS, RWKV_HEAD).transpose(1, 3, 0, 2).reshape(t_len, RWKV_HEAD, nc)
    s0 = s_rwkv.transpose(3, 2, 0, 1).reshape(RWKV_HEAD, RWKV_HEAD, nc)
    y_c, s_c = _rwkv_scan(*rows, v_c, w["rwkv_tiles"], s0, tt)
    y = y_c.reshape(t_len, RWKV_HEAD, bsz, RWKV_HEADS).transpose(2, 0, 3, 1).reshape(bsz * t_len, RWKV_WIDTH)
    return y, g, s_c.reshape(RWKV_HEAD, RWKV_HEAD, bsz, RWKV_HEADS).transpose(2, 3, 1, 0)


def _chain_tile(p):
    t = p.reshape(RWKV_HEADS, RWKV_HEAD).T
    return jnp.tile(t, (1, LANES // RWKV_HEADS))


def _chain_tile_paired(p):
    pairs = RWKV_HEADS // 2
    t = p.reshape(pairs, 2, RWKV_HEAD).transpose(2, 1, 0)[:, :, None, :]
    return jnp.broadcast_to(t, (RWKV_HEAD, 2, LANES // RWKV_HEADS, pairs)).reshape(RWKV_HEAD, LANES)


def _lru_kernel(gate_ref, x_ref, cinit_ref, cw_ref, cb_ref, wri_ref, br_ref, bi_ref, lam_ref, h0_ref,
                o_ref, hfin_ref, halo, hprev, *, tt, bb):
    j = pl.program_id(1)

    @pl.when(j == 0)
    def _():
        halo[...] = cinit_ref[...]
        for i in range(bb):
            hprev[i] = jnp.broadcast_to(h0_ref[i], (SUBLANES, LRU_WIDTH))

    rows = _row_iota((tt, LRU_BLOCK))
    for i in range(bb):
        u = x_ref[i]
        ext = jnp.concatenate([halo[i], u], axis=0)
        xc = cb_ref[...] + _shifted(ext, 3, tt) * cw_ref[0:1, :]
        xc = xc + _shifted(ext, 2, tt) * cw_ref[1:2, :]
        xc = xc + _shifted(ext, 1, tt) * cw_ref[2:3, :]
        xc = xc + u * cw_ref[3:4, :]
        halo[i] = u[tt - SUBLANES:tt, :]

        for n in range(LRU_BLOCKS):
            cs = slice(n * LRU_BLOCK, (n + 1) * LRU_BLOCK)
            xb = xc[:, cs]
            ri = _dot(xb.astype(BF16), wri_ref[n])
            rg = jax.nn.sigmoid(ri[:, :LRU_BLOCK] + br_ref[:, cs])
            ig = jax.nn.sigmoid(ri[:, LRU_BLOCK:] + bi_ref[:, cs])
            log_a = -LRU_C * rg * _softplus(-lam_ref[:, cs])
            a = jnp.exp(log_a)
            b = jnp.sqrt(-jnp.tanh(log_a) * (a * a + 1.0)) * ig * xb
            d = 1
            while d < SUBLANES:
                keep = (rows & (SUBLANES - 1)) >= d
                a_sh = jnp.where(keep, pltpu.roll(a, d, axis=0), 1.0)
                b_sh = jnp.where(keep, pltpu.roll(b, d, axis=0), 0.0)
                b = b + a * b_sh
                a = a * a_sh
                d *= 2
            h_in = hprev[i, 0:1, cs]
            groups = []
            for r0 in range(0, tt, SUBLANES):
                h_grp = a[r0:r0 + SUBLANES] * h_in + b[r0:r0 + SUBLANES]
                groups.append(h_grp)
                h_in = h_grp[SUBLANES - 1:SUBLANES, :]
            h = jnp.concatenate(groups, axis=0)
            o_ref[i, :, cs] = (h * _gelu_tanh(gate_ref[i, :, cs])).astype(BF16)
            h_last = h[tt - 1:tt, :]
            hprev[i, :, cs] = jnp.broadcast_to(h_last, (SUBLANES, LRU_BLOCK))
            hfin_ref[i, :, cs] = h_last


def _lru(p1, cinit, cw, cb, wri, br, bi, lam, h0, bsz, t_len):
    tt = min(t_len, 64)
    nt = t_len // tt
    bb = _batch_block(bsz, t_len)
    p3 = p1.reshape(bsz, t_len, P1_COLS)
    rows = lambda w: (lambda b, j: (b, j, w))
    per_seq = lambda b, j: (b, 0, 0)
    const2 = lambda b, j: (0, 0)
    o, h_fin = pl.pallas_call(
        functools.partial(_lru_kernel, tt=tt, bb=bb),
        grid=(bsz // bb, nt),
        in_specs=[
            pl.BlockSpec((bb, tt, 1024), rows(0)),
            pl.BlockSpec((bb, tt, 1024), rows(1)),
            pl.BlockSpec((bb, SUBLANES, LRU_WIDTH), per_seq),
            pl.BlockSpec((SUBLANES, LRU_WIDTH), const2),
            pl.BlockSpec((1, LRU_WIDTH), const2),
            pl.BlockSpec((LRU_BLOCKS, LRU_BLOCK, 2 * LRU_BLOCK), lambda b, j: (0, 0, 0)),
            pl.BlockSpec((1, LRU_WIDTH), const2),
            pl.BlockSpec((1, LRU_WIDTH), const2),
            pl.BlockSpec((1, LRU_WIDTH), const2),
            pl.BlockSpec((bb, 1, LRU_WIDTH), per_seq),
        ],
        out_specs=[pl.BlockSpec((bb, tt, LRU_WIDTH), rows(0)),
                   pl.BlockSpec((bb, 1, LRU_WIDTH), per_seq)],
        out_shape=[jax.ShapeDtypeStruct((bsz, t_len, LRU_WIDTH), BF16),
                   jax.ShapeDtypeStruct((bsz, 1, LRU_WIDTH), F32)],
        scratch_shapes=[pltpu.VMEM((bb, SUBLANES, LRU_WIDTH), F32), pltpu.VMEM((bb, SUBLANES, LRU_WIDTH), F32)],
        compiler_params=_cparams(("parallel", "arbitrary")),
        name="lru",
    )(p3, p3, cinit, cw, cb, wri, br, bi, lam, h0)
    return o.reshape(bsz * t_len, LRU_WIDTH), h_fin


def _expand_heads(x, e_ref, terms):
    parts = []
    rest = x
    for _ in range(terms):
        part = rest.astype(BF16).astype(F32)
        parts.append(part)
        rest = rest - part
    rows = x.shape[0]
    out = _dot(jnp.concatenate(parts, axis=0), e_ref[...])
    acc = out[:rows]
    for t in range(1, terms):
        acc = acc + out[t * rows:(t + 1) * rows]
    return acc


def _ssd_kernel(z_ref, xbc_ref, dt_ref, cinit_ref, cw_ref, cb_ref, dtb_ref, alog_ref, dd_ref, nw_ref,
                ej_ref, ep_ref, s0_ref, o_ref, s_ref, halo, st_t, *, c, mm, bb):
    j = pl.program_id(1)
    hg = SSD_HEADS // SSD_GROUPS
    gw = SSD_INNER // SSD_GROUPS
    pair = 2 * SSD_HEADDIM

    @pl.when(j == 0)
    def _():
        halo[...] = cinit_ref[...]
        for i in range(bb):
            for q in range(SSD_HEADS // 2):
                two = jnp.concatenate([s0_ref[i, 2 * q], s0_ref[i, 2 * q + 1]], axis=0)
                st_t[i, :, q * pair:(q + 1) * pair] = two.T

    wj = SSD_HEADS * c
    row_t = _row_iota((c, wj))
    lane_t = lax.broadcasted_iota(jnp.int32, (c, wj), 1) & (c - 1)
    causal = lane_t <= row_t
    diag = lane_t == row_t
    bd_rows = hg * c
    bd_mask = ((_row_iota((bd_rows, gw)) >> (c.bit_length() - 1))
               == (lax.broadcasted_iota(jnp.int32, (bd_rows, gw), 1) >> (SSD_HEADDIM.bit_length() - 1)))
    neg_a = -jnp.exp(alog_ref[...])
    gs = SSD_GROUPS * SSD_STATE
    for i in range(bb):
        u = xbc_ref[i]
        ext = jnp.concatenate([halo[i], u], axis=0)
        xc = cb_ref[...] + _shifted(ext, 3, c) * cw_ref[0:1, :]
        xc = xc + _shifted(ext, 2, c) * cw_ref[1:2, :]
        xc = xc + _shifted(ext, 1, c) * cw_ref[2:3, :]
        xc = _silu(xc + u * cw_ref[3:4, :])
        halo[i] = u[c - SUBLANES:c, :]

        xs = xc[:, :SSD_INNER]
        bm = xc[:, SSD_INNER:SSD_INNER + gs].astype(mm)
        cm = xc[:, SSD_INNER + gs:].astype(mm)
        dt = _softplus(dt_ref[i] + dtb_ref[...])
        cs = _cumsum_rows(dt * neg_a, c)
        cs_p = _expand_heads(cs, ep_ref, 3)
        dt_p = _expand_heads(dt, ep_ref, 2)
        cs_j = cs_p if c == SSD_HEADDIM else _expand_heads(cs, ej_ref, 3)
        cs_row = jnp.sum(jnp.where(diag, cs_j, 0.0), axis=0, keepdims=True)
        lmat = jnp.exp(jnp.where(causal, cs_j - cs_row, -jnp.inf))
        last_p = cs_p[c - 1:c, :]
        e_cs = jnp.exp(cs_p)
        e_last = jnp.exp(last_p)
        xdt = xs * dt_p
        xw = (xdt * jnp.exp(last_p - cs_p)).astype(mm)
        xdt = xdt.astype(mm)
        y_groups = []
        for g in range(SSD_GROUPS):
            gl = slice(g * gw, (g + 1) * gw)
            ss = slice(g * SSD_STATE, (g + 1) * SSD_STATE)
            cb_x = _dot_nt(cm[:, ss], jnp.concatenate([bm[:, ss]] * hg, axis=0))
            m_x = (cb_x * lmat[:, g * bd_rows:(g + 1) * bd_rows]).astype(mm)
            blocks = jnp.where(bd_mask, jnp.concatenate([xdt[:, gl]] * hg, axis=0), jnp.zeros((), mm))
            st = st_t[i, :, gl]
            y_groups.append(_dot(m_x, blocks) + e_cs[:, gl] * _dot(cm[:, ss], st.astype(mm)))
            st_t[i, :, gl] = st * e_last[:, gl] + _dot_tn(bm[:, ss], xw[:, gl])
        y = (jnp.concatenate(y_groups, axis=1) + dd_ref[...] * xs) * _silu(z_ref[i])
        for g in range(SSD_GROUPS):
            gl = slice(g * gw, (g + 1) * gw)
            o_ref[i, :, gl] = _rms(y[:, gl], nw_ref[:, gl], 1e-5).astype(BF16)

    @pl.when(j == pl.num_programs(1) - 1)
    def _():
        for i in range(bb):
            for q in range(SSD_HEADS // 2):
                two = st_t[i, :, q * pair:(q + 1) * pair].T
                s_ref[i, 2 * q] = two[:SSD_HEADDIM]
                s_ref[i, 2 * q + 1] = two[SSD_HEADDIM:]


def _ssd(p1, cinit, cw, cb, dtb, alog, dd, nw, s0, bsz, t_len):
    c = min(SSD_CHUNK, t_len)
    nt = t_len // c
    bb = _batch_block(bsz, t_len)
    mm = BF16 if c % 16 == 0 else F32
    p3 = p1.reshape(bsz, t_len, P1_COLS)
    rows = lambda w: (lambda b, j: (b, j, w))
    const2 = lambda b, j: (0, 0)
    st_spec = pl.BlockSpec((bb, SSD_HEADS, SSD_HEADDIM, SSD_STATE), lambda b, j: (b, 0, 0, 0))
    head = jnp.arange(LANES, dtype=jnp.int32)[:, None]
    expand = lambda width: (jnp.arange(SSD_HEADS * width, dtype=jnp.int32)[None, :] // width == head).astype(F32)
    o, s_new = pl.pallas_call(
        functools.partial(_ssd_kernel, c=c, mm=mm, bb=bb),
        grid=(bsz // bb, nt),
        in_specs=[
            pl.BlockSpec((bb, c, 1024), rows(2)),
            pl.BlockSpec((bb, c, SSD_CONV_CH), rows(2)),
            pl.BlockSpec((bb, c, LANES), rows(36)),
            pl.BlockSpec((bb, SUBLANES, SSD_CONV_CH), lambda b, j: (b, 0, 0)),
            pl.BlockSpec((SUBLANES, SSD_CONV_CH), const2),
            pl.BlockSpec((1, SSD_CONV_CH), const2),
            pl.BlockSpec((1, LANES), const2),
            pl.BlockSpec((1, LANES), const2),
            pl.BlockSpec((1, SSD_INNER), const2),
            pl.BlockSpec((1, SSD_INNER), const2),
            pl.BlockSpec((LANES, SSD_HEADS * c), const2),
            pl.BlockSpec((LANES, SSD_INNER), const2),
            st_spec,
        ],
        out_specs=[pl.BlockSpec((bb, c, SSD_INNER), rows(0)), st_spec],
        out_shape=[jax.ShapeDtypeStruct((bsz, t_len, SSD_INNER), BF16),
                   jax.ShapeDtypeStruct((bsz, SSD_HEADS, SSD_HEADDIM, SSD_STATE), F32)],
        scratch_shapes=[pltpu.VMEM((bb, SUBLANES, SSD_CONV_CH), F32),
                        pltpu.VMEM((bb, SSD_STATE, SSD_INNER), F32)],
        compiler_params=_cparams(("parallel", "arbitrary")),
        name="ssd",
    )(p3, p3, p3, cinit, cw, cb, dtb, alog, dd, nw, expand(c), expand(SSD_HEADDIM), s0)
    return o.reshape(bsz * t_len, SSD_INNER), s_new


def _pad_rows_to_8(x):
    return jnp.pad(x, ((0, 0), (SUBLANES - x.shape[1], 0), (0, 0)))


def _pad_lanes(x, width):
    return jnp.pad(x, ((0, 0), (0, width - x.shape[1])))


def _trunk(x, states, w):
    s_gla, s_rwkv, s_shift, s_lru, s_lru_conv, s_ssd, s_ssd_conv = states
    bsz, t_len, d = x.shape
    n = bsz * t_len
    x2 = x.reshape(n, d)

    p0 = _in_proj(x2, w["g_mix0"], w["w_in0"], P0_TN)
    o_gla, s_gla_new = _gla(p0, s_gla, w["gla_wa2"], w["gla_ba"], w["gla_gn"], bsz, t_len)
    if bsz * RWKV_HEADS == LANES and t_len % RELAYOUT_STEPS == 0:
        y_rwkv, g, s_rwkv_new = _rwkv_paired(p0, s_shift, s_rwkv, w, bsz, t_len)
    else:
        y_rwkv, g, s_rwkv_new = _rwkv_general(p0, s_shift, s_rwkv, w, bsz, t_len)
    shift_new = p0.reshape(bsz, t_len, P0_COLS)[:, -1, 3072:3072 + RWKV_COLS]
    x2 = _out_ffn(x2, o_gla, y_rwkv, g, w["w_out0"], w["g_ffn0"], w["wg0"], w["wu0"], w["wd0"], None)

    p1 = _in_proj(x2, w["g_mix1"], w["w_in1"], P1_TN)
    lru_out, h_lru = _lru(p1, _pad_rows_to_8(s_lru_conv), w["lru_cw"], w["lru_cb"], w["lru_wri"], w["lru_br"],
                          w["lru_bi"], w["lru_lam"], s_lru.reshape(bsz, 1, LRU_WIDTH), bsz, t_len)
    y_ssd, s_ssd_new = _ssd(p1, _pad_rows_to_8(s_ssd_conv), w["ssd_cw"], w["ssd_cb"], w["ssd_dtb"], w["ssd_alog"],
                            w["ssd_dd"], w["ssd_nw"], s_ssd, bsz, t_len)
    keep = CONV_W - 1
    p1_tail = p1.reshape(bsz, t_len, P1_COLS)[:, -keep:]
    lru_conv_new = p1_tail[:, :, 1024:2048]
    ssd_conv_new = p1_tail[:, :, 3072:3072 + SSD_CONV_CH]
    y = _out_ffn(x2, lru_out, y_ssd, None, w["w_out1"], w["g_ffn1"], w["wg1"], w["wu1"], w["wd1"], w["g_final"])
    return y.reshape(bsz, t_len, d), (s_gla_new, s_rwkv_new, shift_new, h_lru.reshape(bsz, LRU_WIDTH),
                                      lru_conv_new, s_ssd_new, ssd_conv_new)


def kernel(x_prompt, x_sample, state_gla, state_rwkv, state_rwkv_shift, state_lru, state_lru_conv, state_ssd, state_ssd_conv, w_in0, gla_w_a2, gla_b_a, gla_g_norm, rwkv_mu, rwkv_w0, rwkv_w2, rwkv_a0, rwkv_a2, rwkv_g2, rwkv_k_k, rwkv_k_a, rwkv_r_k, rwkv_ln_w, rwkv_ln_b, w_out0, w_in1, lru_conv_w, lru_conv_b, lru_w_r, lru_b_r, lru_w_i, lru_b_i, lru_lambda, ssd_conv_w, ssd_conv_b, ssd_dt_bias, ssd_a_log, ssd_d, ssd_norm_w, w_out1, g_mix, g_ffn, w_ffn_gate, w_ffn_up, w_ffn_down, g_final):
    row = lambda p: p.reshape(1, -1).astype(F32)
    w0 = jnp.concatenate([w_in0[:, :2048], w_in0[:, 2064:3088], w_in0[:, 3088:],
                          _pad_lanes(w_in0[:, 2048:2064], LANES)], axis=1)
    zeros64 = jnp.zeros((64, RWKV_WIDTH), F32)
    rwkv_wa = jnp.concatenate([jnp.concatenate([rwkv_w2, zeros64], axis=1),
                               jnp.concatenate([zeros64, rwkv_a2], axis=1)], axis=0)
    w = {
        "w_in0": w0.astype(BF16),
        "w_in1": _pad_lanes(w_in1, P1_COLS).astype(BF16),
        "g_mix0": g_mix[0], "g_mix1": g_mix[1], "g_ffn0": g_ffn[0], "g_ffn1": g_ffn[1], "g_final": g_final,
        "gla_wa2": jnp.pad(gla_w_a2, ((0, LANES - GLA_GATE_RANK), (0, 0))).astype(BF16),
        "gla_ba": row(gla_b_a), "gla_gn": row(gla_g_norm),
        "rwkv_mu": row(rwkv_mu), "rwkv_w0": row(rwkv_w0), "rwkv_a0": row(rwkv_a0),
        "rwkv_wa": rwkv_wa.astype(BF16), "rwkv_g2": rwkv_g2.astype(BF16),
        "rwkv_tiles": [_chain_tile(p) for p in (rwkv_k_k, rwkv_k_a, rwkv_r_k, rwkv_ln_w, rwkv_ln_b)],
        "rwkv_tiles_paired": [_chain_tile_paired(p)
                              for p in (rwkv_k_k, rwkv_k_a, rwkv_r_k, rwkv_ln_w, rwkv_ln_b)],
        "w_out0": w_out0.astype(BF16), "w_out1": w_out1.astype(BF16),
        "wg0": w_ffn_gate[0].astype(BF16), "wu0": w_ffn_up[0].astype(BF16), "wd0": w_ffn_down[0].astype(BF16),
        "wg1": w_ffn_gate[1].astype(BF16), "wu1": w_ffn_up[1].astype(BF16), "wd1": w_ffn_down[1].astype(BF16),
        "lru_cw": jnp.pad(lru_conv_w, ((0, SUBLANES - CONV_W), (0, 0))), "lru_cb": row(lru_conv_b),
        "lru_wri": jnp.concatenate([lru_w_r, lru_w_i], axis=2).astype(BF16),
        "lru_br": row(lru_b_r), "lru_bi": row(lru_b_i), "lru_lam": row(lru_lambda),
        "ssd_cw": jnp.pad(ssd_conv_w, ((0, SUBLANES - CONV_W), (0, 0))), "ssd_cb": row(ssd_conv_b),
        "ssd_dtb": _pad_lanes(row(ssd_dt_bias), LANES), "ssd_alog": _pad_lanes(row(ssd_a_log), LANES),
        "ssd_dd": row(jnp.repeat(ssd_d, SSD_HEADDIM)), "ssd_nw": row(ssd_norm_w),
    }
    bp = x_prompt.shape[0]
    prompt_init = (
        jnp.zeros((bp, GLA_HEADS, GLA_DK, GLA_DV), F32),
        jnp.zeros((bp, RWKV_HEADS, RWKV_HEAD, RWKV_HEAD), F32),
        jnp.zeros((bp, RWKV_COLS), F32),
        jnp.zeros((bp, LRU_WIDTH), F32),
        jnp.zeros((bp, CONV_W - 1, LRU_WIDTH), F32),
        jnp.zeros((bp, SSD_HEADS, SSD_HEADDIM, SSD_STATE), F32),
        jnp.zeros((bp, CONV_W - 1, SSD_CONV_CH), F32),
    )
    sample_init = (state_gla, state_rwkv, state_rwkv_shift, state_lru, state_lru_conv, state_ssd, state_ssd_conv)
    y_prompt, p_states = _trunk(x_prompt, prompt_init, w)
    y_sample, s_states = _trunk(x_sample, sample_init, w)
    return (y_prompt, y_sample, *p_states, *s_states)
```

```python
import functools

import jax
import jax.numpy as jnp
from jax import lax
from jax.experimental import pallas as pl
from jax.experimental.pallas import tpu as pltpu

F32 = jnp.float32
BF16 = jnp.bfloat16

D_MODEL = 1024
NORM_EPS = 1e-6
GLA_HEADS = 4
GLA_DK = 128
GLA_DV = 256
GLA_GATE_RANK = 16
GLA_GATE_NORM = 16.0
GLA_CHUNK = 64
RWKV_HEAD = 64
RWKV_HEADS = 16
RWKV_WIDTH = 1024
RWKV_LORA = 256
RWKV_COLS = 3 * RWKV_WIDTH + RWKV_LORA
RWKV_GN_EPS = 64e-5
LRU_WIDTH = 1024
LRU_BLOCKS = 8
LRU_BLOCK = 128
LRU_C = 8.0
CONV_W = 4
SSD_INNER = 1024
SSD_HEADDIM = 64
SSD_HEADS = 16
SSD_GROUPS = 2
SSD_STATE = 128
SSD_CHUNK = 64
SSD_CONV_CH = SSD_INNER + 2 * SSD_GROUPS * SSD_STATE
D_FF = 2816

LANES = 128
SUBLANES = 8
VMEM_LIMIT_BYTES = 56 * 1024 * 1024

P0_COLS = 6528
P0_TN = 2176
P1_COLS = 4864
P1_TN = 2432


def _cparams(semantics):
    return pltpu.CompilerParams(dimension_semantics=semantics, vmem_limit_bytes=VMEM_LIMIT_BYTES)


def _dot(a, b):
    return jnp.dot(a, b, preferred_element_type=F32)


def _dot_nt(a, b):
    return lax.dot_general(a, b, (((1,), (1,)), ((), ())), preferred_element_type=F32)


def _dot_tn(a, b):
    return lax.dot_general(a, b, (((0,), (0,)), ((), ())), preferred_element_type=F32)


def _softplus(x):
    return jnp.maximum(x, 0.0) + jnp.log1p(jnp.exp(-jnp.abs(x)))


def _silu(x):
    return x * jax.nn.sigmoid(x)


def _gelu_tanh(x):
    return 0.5 * x * (1.0 + jnp.tanh(0.7978845608028654 * (x + 0.044715 * (x * x * x))))


def _rms(x, g, eps):
    return x * lax.rsqrt(jnp.mean(x * x, axis=-1, keepdims=True) + eps) * g


def _row_iota(shape):
    return lax.broadcasted_iota(jnp.int32, shape, 0)


def _cumsum_rows(x, seg):
    rows = _row_iota(x.shape) & (seg - 1)
    d = 1
    while d < seg:
        x = x + jnp.where(rows >= d, pltpu.roll(x, d, axis=0), 0.0)
        d *= 2
    return x


def _shifted(ext, k, rows):
    return pltpu.roll(ext, k, axis=0)[SUBLANES:SUBLANES + rows]


def _in_proj_kernel(x_ref, g_ref, w_ref, o_ref, h_scr):
    @pl.when(pl.program_id(1) == 0)
    def _():
        h_scr[...] = _rms(x_ref[...], g_ref[...], NORM_EPS).astype(BF16)

    o_ref[...] = _dot(h_scr[...], w_ref[...])


def _in_proj(x2d, g, w, tn):
    n, d = x2d.shape
    cols = w.shape[1]
    tm = min(n, 1024)
    return pl.pallas_call(
        _in_proj_kernel,
        grid=(n // tm, cols // tn),
        in_specs=[
            pl.BlockSpec((tm, d), lambda i, j: (i, 0)),
            pl.BlockSpec((1, d), lambda i, j: (0, 0)),
            pl.BlockSpec((d, tn), lambda i, j: (0, j)),
        ],
        out_specs=pl.BlockSpec((tm, tn), lambda i, j: (i, j)),
        out_shape=jax.ShapeDtypeStruct((n, cols), F32),
        scratch_shapes=[pltpu.VMEM((tm, d), BF16)],
        compiler_params=_cparams(("parallel", "arbitrary")),
        name="in_proj",
    )(x2d, g.reshape(1, d), w)


def _out_ffn_kernel(*refs, has_gate, final_norm):
    refs = list(refs)
    x_ref, oa_ref, ob_ref = refs[:3]
    pos = 3
    gate_ref = None
    if has_gate:
        gate_ref = refs[pos]
        pos += 1
    woa_ref, wob_ref, gffn_ref, wg_ref, wu_ref, wd_ref = refs[pos:pos + 6]
    pos += 6
    gfin_ref = None
    if final_norm:
        gfin_ref = refs[pos]
        pos += 1
    out_ref, h_scr = refs[pos:pos + 2]

    k = pl.program_id(1)

    @pl.when(k == 0)
    def _():
        ob = ob_ref[...]
        if has_gate:
            ob = (ob * gate_ref[...]).astype(BF16)
        x1 = x_ref[...] + (_dot(oa_ref[...], woa_ref[...]) + _dot(ob, wob_ref[...]))
        out_ref[...] = x1
        h_scr[...] = _rms(x1, gffn_ref[...], NORM_EPS).astype(BF16)

    h = h_scr[...]
    act = (_silu(_dot(h, wg_ref[...])) * _dot(h, wu_ref[...])).astype(BF16)
    out_ref[...] += _dot(act, wd_ref[...])

    if final_norm:
        @pl.when(k == pl.num_programs(1) - 1)
        def _():
            out_ref[...] = _rms(out_ref[...], gfin_ref[...], NORM_EPS)


def _out_ffn(x2d, oa, ob, gate, wo, g_ffn, wg, wu, wd, g_final):
    n, d = x2d.shape
    has_gate = gate is not None
    final_norm = g_final is not None
    tm = min(n, 512)
    tf = D_FF // 2
    row = lambda i, k: (i, 0)
    const = lambda i, k: (0, 0)
    args = [x2d, oa, ob]
    in_specs = [pl.BlockSpec((tm, d), row)] * 3
    if has_gate:
        args.append(gate)
        in_specs.append(pl.BlockSpec((tm, d), row))
    half = wo.shape[0] // 2
    args += [wo, wo, g_ffn.reshape(1, d), wg, wu, wd]
    in_specs += [
        pl.BlockSpec((half, d), const),
        pl.BlockSpec((half, d), lambda i, k: (1, 0)),
        pl.BlockSpec((1, d), const),
        pl.BlockSpec((d, tf), lambda i, k: (0, k)),
        pl.BlockSpec((d, tf), lambda i, k: (0, k)),
        pl.BlockSpec((tf, d), lambda i, k: (k, 0)),
    ]
    if final_norm:
        args.append(g_final.reshape(1, d))
        in_specs.append(pl.BlockSpec((1, d), const))
    return pl.pallas_call(
        functools.partial(_out_ffn_kernel, has_gate=has_gate, final_norm=final_norm),
        grid=(n // tm, D_FF // tf),
        in_specs=in_specs,
        out_specs=pl.BlockSpec((tm, d), row),
        out_shape=jax.ShapeDtypeStruct((n, d), F32),
        scratch_shapes=[pltpu.VMEM((tm, d), BF16)],
        compiler_params=_cparams(("parallel", "arbitrary")),
        name="out_ffn",
    )(*args)


def _gla_kernel(q_ref, k_ref, v_ref, og_ref, al_ref, wa2_ref, ba_ref, gn_ref, s0_ref,
                o_ref, sout_ref, st_scr, *, c, mm, bb):
    n = pl.program_id(1)

    @pl.when(n == 0)
    def _():
        for i in range(bb):
            for h in range(GLA_HEADS):
                st_scr[i * GLA_HEADS + h] = s0_ref[i, h].T

    mask = _row_iota((c, c)) >= lax.broadcasted_iota(jnp.int32, (c, c), 1)
    gn = gn_ref[...]
    for i in range(bb):
        la_all = _dot(al_ref[i].astype(BF16), wa2_ref[...]) + ba_ref[...]
        la_all = -_softplus(-la_all) * (1.0 / GLA_GATE_NORM)
        b_all = _cumsum_rows(la_all, c)
        for h in range(GLA_HEADS):
            ks = slice(h * GLA_DK, (h + 1) * GLA_DK)
            vs = slice(h * GLA_DV, (h + 1) * GLA_DV)
            b = b_all[:, ks]
            b_last = b[c - 1:c, :]
            q = q_ref[i, :, ks] * (GLA_DK ** -0.5)
            k = k_ref[i, :, ks]
            v = v_ref[i, :, vs].astype(mm)
            qd = (q * jnp.exp(b)).astype(mm)
            kd = (k * jnp.exp(-b)).astype(mm)
            kc = (k * jnp.exp(b_last - b)).astype(mm)
            scores = jnp.where(mask, _dot_nt(qd, kd), 0.0)
            st = st_scr[i * GLA_HEADS + h]
            o = _dot(scores.astype(mm), v) + _dot_nt(qd, st.astype(mm))
            st_scr[i * GLA_HEADS + h] = st * jnp.exp(b_last) + _dot_tn(v, kc)
            o = _rms(o, gn, 1e-5)
            o_ref[i, :, vs] = (o * _silu(og_ref[i, :, vs])).astype(BF16)

    @pl.when(n == pl.num_programs(1) - 1)
    def _():
        for i in range(bb):
            for h in range(GLA_HEADS):
                sout_ref[i, h] = st_scr[i * GLA_HEADS + h].T


def _batch_block(bsz, t_len):
    return min(bsz, 4) if t_len >= 64 else min(bsz, 8)


def _gla(p0, s0, wa2, ba, gn, bsz, t_len):
    c = min(GLA_CHUNK, t_len)
    nt = t_len // c
    bb = _batch_block(bsz, t_len)
    mm = BF16 if c % 16 == 0 else F32
    p3 = p0.reshape(bsz, t_len, P0_COLS)
    rows = lambda w: (lambda b, j: (b, j, w))
    const = lambda b, j: (0, 0)
    st_spec = pl.BlockSpec((bb, GLA_HEADS, GLA_DK, GLA_DV), lambda b, j: (b, 0, 0, 0))
    o, s_new = pl.pallas_call(
        functools.partial(_gla_kernel, c=c, mm=mm, bb=bb),
        grid=(bsz // bb, nt),
        in_specs=[
            pl.BlockSpec((bb, c, 512), rows(0)),
            pl.BlockSpec((bb, c, 512), rows(1)),
            pl.BlockSpec((bb, c, 1024), rows(1)),
            pl.BlockSpec((bb, c, 1024), rows(2)),
            pl.BlockSpec((bb, c, LANES), rows(50)),
            pl.BlockSpec((LANES, 512), const),
            pl.BlockSpec((1, 512), const),
            pl.BlockSpec((1, GLA_DV), const),
            st_spec,
        ],
        out_specs=[pl.BlockSpec((bb, c, 1024), rows(0)), st_spec],
        out_shape=[jax.ShapeDtypeStruct((bsz, t_len, 1024), BF16),
                   jax.ShapeDtypeStruct((bsz, GLA_HEADS, GLA_DK, GLA_DV), F32)],
        scratch_shapes=[pltpu.VMEM((bb * GLA_HEADS, GLA_DV, GLA_DK), F32)],
        compiler_params=_cparams(("parallel", "arbitrary")),
        name="gla",
    )(p3, p3, p3, p3, p3, wa2, ba, gn, s0)
    return o.reshape(bsz * t_len, 1024), s_new


def _rwkv_prep_kernel(r_ref, k_ref, v_ref, l_ref, ir_ref, ik_ref, iv_ref, il_ref,
                      mu_ref, w0_ref, a0_ref, wa_ref, g2_ref,
                      ro_ref, wo_ref, ko_ref, vo_ref, ao_ref, go_ref, carry,
                      *, tm, t_len, multi_seq):
    j = pl.program_id(1)
    cols = ((0, 1024), (1024, 2048), (2048, 3072), (3072, RWKV_COLS))

    if not multi_seq:
        @pl.when(j == 0)
        def _():
            for (lo, hi), iref in zip(cols, (ir_ref, ik_ref, iv_ref, il_ref)):
                carry[:, lo:hi] = iref[...]

    def mixed(x_ref, i_ref, lo, hi):
        x = x_ref[...]
        rolled = pltpu.roll(x, 1, axis=0)
        rows = _row_iota(x.shape)
        if multi_seq:
            prev = jnp.where((rows & (t_len - 1)) == 0, i_ref[...], rolled)
        else:
            prev = jnp.where(rows == 0, carry[SUBLANES - 1:SUBLANES, lo:hi], rolled)
            carry[:, lo:hi] = x[tm - SUBLANES:tm, :]
        return x + (prev - x) * mu_ref[:, lo:hi]

    r = mixed(r_ref, ir_ref, *cols[0])
    k = mixed(k_ref, ik_ref, *cols[1])
    v = mixed(v_ref, iv_ref, *cols[2])
    lo_rank = mixed(l_ref, il_ref, *cols[3])

    wa_in = lo_rank[:, :LANES]
    lane = lax.broadcasted_iota(jnp.int32, wa_in.shape, 1)
    wa_in = jnp.where(lane < 64, jnp.tanh(wa_in), wa_in).astype(BF16)
    wa = _dot(wa_in, wa_ref[...])
    g = _dot(jax.nn.sigmoid(lo_rank[:, LANES:]).astype(BF16), g2_ref[...])
    w = -_softplus(-(w0_ref[...] + wa[:, :RWKV_WIDTH])) - 0.5
    ro_ref[...] = r
    wo_ref[...] = -jnp.exp(w)
    ko_ref[...] = k
    vo_ref[...] = v
    ao_ref[...] = jax.nn.sigmoid(a0_ref[...] + wa[:, RWKV_WIDTH:])
    go_ref[...] = g


def _rwkv_prep(p0, shift8, mu, w0, a0, wa, g2, bsz, t_len):
    n = bsz * t_len
    multi_seq = t_len < 256
    tm = min(n, 512) if multi_seq else 256
    nt = 1 if multi_seq else t_len // tm
    nb = n // tm if multi_seq else bsz
    ti = tm if multi_seq else SUBLANES
    rows = lambda w: (lambda i, j: (i * nt + j, w))
    init = lambda w: (lambda i, j: (i, w))
    const = lambda i, j: (0, 0)
    out = jax.ShapeDtypeStruct((n, RWKV_WIDTH), F32)
    return pl.pallas_call(
        functools.partial(_rwkv_prep_kernel, tm=tm, t_len=t_len, multi_seq=multi_seq),
        grid=(nb, nt),
        in_specs=[
            pl.BlockSpec((tm, 1024), rows(3)),
            pl.BlockSpec((tm, 1024), rows(4)),
            pl.BlockSpec((tm, 1024), rows(5)),
            pl.BlockSpec((tm, RWKV_LORA), rows(24)),
            pl.BlockSpec((ti, 1024), init(0)),
            pl.BlockSpec((ti, 1024), init(1)),
            pl.BlockSpec((ti, 1024), init(2)),
            pl.BlockSpec((ti, RWKV_LORA), init(12)),
            pl.BlockSpec((1, RWKV_COLS), const),
            pl.BlockSpec((1, RWKV_WIDTH), const),
            pl.BlockSpec((1, RWKV_WIDTH), const),
            pl.BlockSpec((LANES, 2 * RWKV_WIDTH), const),
            pl.BlockSpec((LANES, RWKV_WIDTH), const),
        ],
        out_specs=[pl.BlockSpec((tm, RWKV_WIDTH), rows(0))] * 6,
        out_shape=[out] * 6,
        scratch_shapes=[pltpu.VMEM((SUBLANES, RWKV_COLS), F32)],
        compiler_params=_cparams(("parallel", "arbitrary")),
        name="rwkv_prep",
    )(p0, p0, p0, p0, shift8, shift8, shift8, shift8, mu, w0, a0, wa, g2)


def _rwkv_scan_kernel(r_ref, lw_ref, k_ref, a_ref, v_ref, kk_ref, ka_ref, rk_ref, lnw_ref, lnb_ref,
                      s0_ref, y_ref, s_ref, kb_scr, kp_scr, rs_scr, kn_scr, bonus_scr, grow_scr, *, tt):
    @pl.when(pl.program_id(1) == 0)
    def _():
        s_ref[...] = s0_ref[...]

    def tile_row(ref, j):
        return ref[pl.ds(j, 1), :]

    def bcast(x):
        return jnp.broadcast_to(x, (RWKV_HEAD, LANES))

    norm2 = jnp.zeros((tt, LANES), F32)
    for j in range(RWKV_HEAD):
        kj = k_ref[0, j] * tile_row(kk_ref, j)
        norm2 = norm2 + kj * kj
    norm = jnp.maximum(jnp.sqrt(norm2), 1e-12)
    bonus = jnp.zeros((tt, LANES), F32)
    s_kk = jnp.zeros((RWKV_HEAD, LANES), F32)
    for j in range(RWKV_HEAD):
        k = k_ref[0, j]
        a = a_ref[0, j]
        r = r_ref[0, j]
        kk = k * tile_row(kk_ref, j) / norm
        kp = k * (1.0 + (a - 1.0) * tile_row(ka_ref, j))
        log_g = _cumsum_rows(lw_ref[0, j], tt)
        grow = jnp.exp(log_g)
        shrink = jnp.exp(-log_g)
        kb_scr[j] = kk * a * shrink
        kp_scr[j] = kp * shrink
        rs_scr[j] = r * grow
        kn_scr[j] = pltpu.roll(kk, tt - 1, axis=0) * grow
        grow_scr[pl.ds(j, 1), :] = grow[tt - 1:tt, :]
        bonus = bonus + r * kp * tile_row(rk_ref, j)
        s_kk = s_kk + s_ref[j] * bcast(kk[0:1, :])
    bonus_scr[...] = bonus

    def row(ref, j, s):
        return bcast(ref[j, pl.ds(s, 1), :])

    def sweep(s, s_kk):
        sa = -s_kk
        v = v_ref[s]
        y = jnp.zeros_like(v)
        s_kk_next = jnp.zeros_like(v)
        for j in range(RWKV_HEAD):
            sn = s_ref[j] + sa * row(kb_scr, j, s) + v * row(kp_scr, j, s)
            s_ref[j] = sn
            y = y + sn * row(rs_scr, j, s)
            s_kk_next = s_kk_next + sn * row(kn_scr, j, s)
        return s_kk_next, y

    def emit(s, y):
        mu = jnp.mean(y, axis=0, keepdims=True)
        yc = y - mu
        var = jnp.mean(yc * yc, axis=0, keepdims=True)
        y_ref[s] = (yc * lax.rsqrt(var + RWKV_GN_EPS) * lnw_ref[...] + lnb_ref[...]
                    + bonus_scr[pl.ds(s, 1), :] * v_ref[s])

    def step(s, carry):
        s_kk, y_prev = carry
        emit(s - 1, y_prev)
        return sweep(s, s_kk)

    _, y_last = lax.fori_loop(1, tt, step, sweep(0, s_kk))
    emit(tt - 1, y_last)
    for j in range(RWKV_HEAD):
        s_ref[j] = s_ref[j] * bcast(tile_row(grow_scr, j))


def _rwkv_scan(r, lw, k, a, v, tiles, s0, tt):
    nb, _, _, nc = r.shape
    seq = pl.BlockSpec((1, RWKV_HEAD, tt, LANES), lambda g, t: (t, 0, 0, g))
    steps = pl.BlockSpec((tt, RWKV_HEAD, LANES), lambda g, t: (t, 0, g))
    tile = pl.BlockSpec((RWKV_HEAD, LANES), lambda g, t: (0, 0))
    st = pl.BlockSpec((RWKV_HEAD, RWKV_HEAD, LANES), lambda g, t: (0, 0, g))
    return pl.pallas_call(
        functools.partial(_rwkv_scan_kernel, tt=tt),
        grid=(nc // LANES, nb),
        in_specs=[seq] * 4 + [steps] + [tile] * 5 + [st],
        out_specs=[steps, st],
        out_shape=[jax.ShapeDtypeStruct((nb * tt, RWKV_HEAD, nc), F32),
                   jax.ShapeDtypeStruct((RWKV_HEAD, RWKV_HEAD, nc), F32)],
        scratch_shapes=[pltpu.VMEM((RWKV_HEAD, tt, LANES), F32)] * 4
        + [pltpu.VMEM((tt, LANES), F32), pltpu.VMEM((RWKV_HEAD, LANES), F32)],
        compiler_params=_cparams(("parallel", "arbitrary")),
        name="rwkv_scan",
    )(r, lw, k, a, v, *tiles, s0)


RELAYOUT_PITCH = LANES + SUBLANES
RELAYOUT_STEPS = LANES
SCAN_STEPS = 32


def _chain_input_kernel(*refs, kind, step_major):
    x_ref, init_ref, mu_ref = refs[:3]
    bias_ref = wmat_ref = z_scr = None
    if kind == "plain":
        o_ref, carry, z_scr = refs[3:]
    elif kind == "gate":
        wmat_ref, o_ref, carry = refs[3:]
    else:
        bias_ref, wmat_ref, o_ref, carry, z_scr = refs[3:]

    @pl.when(pl.program_id(0) == 0)
    def _():
        carry[...] = init_ref[...]

    nseq, steps, width = x_ref.shape
    pairs = RWKV_HEADS // 2
    rows = _row_iota((steps, width))
    for b in range(nseq):
        x = x_ref[b]
        prev = jnp.where(rows == 0, carry[b, SUBLANES - 1:SUBLANES, :], pltpu.roll(x, 1, axis=0))
        carry[b] = x[steps - SUBLANES:, :]
        mixed = x + (prev - x) * mu_ref[...]
        if kind == "plain":
            val = mixed
        elif kind == "gate":
            o_ref[b] = _dot(jax.nn.sigmoid(mixed[:, LANES:]).astype(BF16), wmat_ref[...])
            continue
        else:
            low = mixed[:, :LANES]
            lane = lax.broadcasted_iota(jnp.int32, low.shape, 1)
            low = jnp.where(lane < 64, jnp.tanh(low), low).astype(BF16)
            pre = bias_ref[...] + _dot(low, wmat_ref[...])
            val = -jnp.exp(-_softplus(-pre) - 0.5) if kind == "decay" else jax.nn.sigmoid(pre)
        for hp in range(pairs):
            q = b * pairs + hp
            z_scr[q * RELAYOUT_PITCH:q * RELAYOUT_PITCH + LANES, :] = val[:, hp * LANES:(hp + 1) * LANES].T
    if kind == "gate":
        return
    half = nseq * pairs
    for j in range(RWKV_HEAD):
        m = jnp.concatenate([z_scr[pl.ds(j, half, stride=RELAYOUT_PITCH), :],
                             z_scr[pl.ds(RWKV_HEAD + j, half, stride=RELAYOUT_PITCH), :]], axis=0)
        mt = m.T
        if step_major:
            o_ref[pl.ds(j, steps, stride=RWKV_HEAD), :] = mt
        else:
            for tq in range(steps // SCAN_STEPS):
                o_ref[tq, j] = mt[tq * SCAN_STEPS:(tq + 1) * SCAN_STEPS]


def _chain_input(p3, shift3, mu, col, init_col, kind, step_major=False, bias=None, wmat=None):
    bsz, t_len, _ = p3.shape
    width = mu.shape[1]
    const = lambda t: (0, 0)
    args = [p3, shift3, mu]
    in_specs = [pl.BlockSpec((bsz, RELAYOUT_STEPS, width), lambda t: (0, t, col)),
                pl.BlockSpec((bsz, SUBLANES, width), lambda t: (0, 0, init_col)),
                pl.BlockSpec((1, width), const)]
    if bias is not None:
        args.append(bias)
        in_specs.append(pl.BlockSpec(bias.shape, const))
    if wmat is not None:
        args.append(wmat)
        in_specs.append(pl.BlockSpec(wmat.shape, const))
    scratch = [pltpu.VMEM((bsz, SUBLANES, width), F32)]
    if kind == "gate":
        out_spec = pl.BlockSpec((bsz, RELAYOUT_STEPS, RWKV_WIDTH), lambda t: (0, t, 0))
        out_shape = jax.ShapeDtypeStruct((bsz, t_len, RWKV_WIDTH), F32)
    else:
        scratch.append(pltpu.VMEM((bsz * (RWKV_HEADS // 2) * RELAYOUT_PITCH, LANES), F32))
        if step_major:
            out_spec = pl.BlockSpec((RELAYOUT_STEPS * RWKV_HEAD, LANES), lambda t: (t, 0))
            out_shape = jax.ShapeDtypeStruct((t_len * RWKV_HEAD, LANES), F32)
        else:
            per = RELAYOUT_STEPS // SCAN_STEPS
            out_spec = pl.BlockSpec((per, RWKV_HEAD, SCAN_STEPS, LANES), lambda t: (t, 0, 0, 0))
            out_shape = jax.ShapeDtypeStruct((t_len // SCAN_STEPS, RWKV_HEAD, SCAN_STEPS, LANES), F32)
    return pl.pallas_call(
        functools.partial(_chain_input_kernel, kind=kind, step_major=step_major),
        grid=(t_len // RELAYOUT_STEPS,),
        in_specs=in_specs,
        out_specs=out_spec,
        out_shape=out_shape,
        scratch_shapes=scratch,
        compiler_params=_cparams(("arbitrary",)),
        name="rwkv_" + kind,
    )(*args)


def _from_chains_kernel(y_ref, g_ref, o_ref, z_scr):
    nseq = o_ref.shape[0]
    pairs = RWKV_HEADS // 2
    half = nseq * pairs
    for i in range(RWKV_HEAD):
        mt = y_ref[pl.ds(i, RELAYOUT_STEPS, stride=RWKV_HEAD), :].T
        z_scr[pl.ds(i, half, stride=RELAYOUT_PITCH), :] = mt[:half]
        z_scr[pl.ds(RWKV_HEAD + i, half, stride=RELAYOUT_PITCH), :] = mt[half:]
    for b in range(nseq):
        for hp in range(pairs):
            q = b * pairs + hp
            lanes = slice(hp * LANES, (hp + 1) * LANES)
            y = z_scr[q * RELAYOUT_PITCH:q * RELAYOUT_PITCH + LANES, :].T
            o_ref[b, :, lanes] = (y * g_ref[b, :, lanes]).astype(BF16)


def _from_chains(y2, g3):
    bsz, t_len, _ = g3.shape
    natural = pl.BlockSpec((bsz, RELAYOUT_STEPS, RWKV_WIDTH), lambda t: (0, t, 0))
    return pl.pallas_call(
        _from_chains_kernel,
        grid=(t_len // RELAYOUT_STEPS,),
        in_specs=[pl.BlockSpec((RELAYOUT_STEPS * RWKV_HEAD, LANES), lambda t: (t, 0)), natural],
        out_specs=natural,
        out_shape=jax.ShapeDtypeStruct((bsz, t_len, RWKV_WIDTH), BF16),
        scratch_shapes=[pltpu.VMEM((bsz * (RWKV_HEADS // 2) * RELAYOUT_PITCH, LANES), F32)],
        compiler_params=_cparams(("arbitrary",)),
        name="from_chains",
    )(y2, g3)


def _rwkv_paired(p0, s_shift, s_rwkv, w, bsz, t_len):
    pairs = RWKV_HEADS // 2
    p3 = p0.reshape(bsz, t_len, P0_COLS)
    shift3 = jnp.broadcast_to(s_shift[:, None, :], (bsz, SUBLANES, RWKV_COLS))
    mu = w["rwkv_mu"]
    mu_main = [mu[:, i * RWKV_WIDTH:(i + 1) * RWKV_WIDTH] for i in range(3)]
    mu_low = mu[:, 3 * RWKV_WIDTH:]
    wa = w["rwkv_wa"]
    low_col = 3 * RWKV_WIDTH // RWKV_LORA
    r = _chain_input(p3, shift3, mu_main[0], 3, 0, "plain")
    k = _chain_input(p3, shift3, mu_main[1], 4, 1, "plain")
    v = _chain_input(p3, shift3, mu_main[2], 5, 2, "plain", step_major=True)
    lw = _chain_input(p3, shift3, mu_low, 24, low_col, "decay", bias=w["rwkv_w0"], wmat=wa[:, :RWKV_WIDTH])
    a = _chain_input(p3, shift3, mu_low, 24, low_col, "a", bias=w["rwkv_a0"], wmat=wa[:, RWKV_WIDTH:])
    g = _chain_input(p3, shift3, mu_low, 24, low_col, "gate", wmat=w["rwkv_g2"])
    s0 = s_rwkv.reshape(bsz, pairs, 2, RWKV_HEAD, RWKV_HEAD).transpose(4, 3, 2, 0, 1)
    y_c, s_c = _rwkv_scan(r, lw, k, a, v.reshape(t_len, RWKV_HEAD, LANES), w["rwkv_tiles_paired"],
                          s0.reshape(RWKV_HEAD, RWKV_HEAD, LANES), SCAN_STEPS)
    y = _from_chains(y_c.reshape(t_len * RWKV_HEAD, LANES), g)
    s_new = s_c.reshape(RWKV_HEAD, RWKV_HEAD, 2, bsz, pairs).transpose(3, 4, 2, 1, 0)
    return y.reshape(bsz * t_len, RWKV_WIDTH), None, s_new.reshape(bsz, RWKV_HEADS, RWKV_HEAD, RWKV_HEAD)


def _rwkv_general(p0, s_shift, s_rwkv, w, bsz, t_len):
    shift8 = jnp.broadcast_to(s_shift[:, None, :], (bsz, SUBLANES, RWKV_COLS)).reshape(bsz * SUBLANES, RWKV_COLS)
    r, lw, k, v, a, g = _rwkv_prep(p0, shift8, w["rwkv_mu"], w["rwkv_w0"], w["rwkv_a0"], w["rwkv_wa"],
                                   w["rwkv_g2"], bsz, t_len)
    row_seqs = (r, lw, k, a)
    tt = min(t_len, SCAN_STEPS)
    nb = t_len // tt
    nc = bsz * RWKV_HEADS
    rows = [t.reshape(bsz, nb, tt, RWKV_HEADS, RWKV_HEAD).transpose(1, 4, 2, 0, 3).reshape(nb, RWKV_HEAD, tt, nc)
            for t in row_seqs]
    v_c = v.reshape(bsz, t_len, RWKV_HEADS, RWKV_HEAD).transpose(1, 3, 0, 2).reshape(t_len, RWKV_HEAD, nc)
    s0 = s_rwkv.transpose(3, 2, 0, 1).reshape(RWKV_HEAD, RWKV_HEAD, nc)
    y_c, s_c = _rwkv_scan(*rows, v_c, w["rwkv_tiles"], s0, tt)
    y = y_c.reshape(t_len, RWKV_HEAD, bsz, RWKV_HEADS).transpose(2, 0, 3, 1).reshape(bsz * t_len, RWKV_WIDTH)
    return y, g, s_c.reshape(RWKV_HEAD, RWKV_HEAD, bsz, RWKV_HEADS).transpose(2, 3, 1, 0)


def _chain_tile(p):
    t = p.reshape(RWKV_HEADS, RWKV_HEAD).T
    return jnp.tile(t, (1, LANES // RWKV_HEADS))


def _chain_tile_paired(p):
    pairs = RWKV_HEADS // 2
    t = p.reshape(pairs, 2, RWKV_HEAD).transpose(2, 1, 0)[:, :, None, :]
    return jnp.broadcast_to(t, (RWKV_HEAD, 2, LANES // RWKV_HEADS, pairs)).reshape(RWKV_HEAD, LANES)


def _lru_kernel(gate_ref, x_ref, cinit_ref, cw_ref, cb_ref, wri_ref, br_ref, bi_ref, lam_ref, h0_ref,
                o_ref, hfin_ref, halo, hprev, *, tt, bb):
    j = pl.program_id(1)

    @pl.when(j == 0)
    def _():
        halo[...] = cinit_ref[...]
        for i in range(bb):
            hprev[i] = jnp.broadcast_to(h0_ref[i], (SUBLANES, LRU_WIDTH))

    rows = _row_iota((tt, LRU_BLOCK))
    for i in range(bb):
        u = x_ref[i]
        ext = jnp.concatenate([halo[i], u], axis=0)
        xc = cb_ref[...] + _shifted(ext, 3, tt) * cw_ref[0:1, :]
        xc = xc + _shifted(ext, 2, tt) * cw_ref[1:2, :]
        xc = xc + _shifted(ext, 1, tt) * cw_ref[2:3, :]
        xc = xc + u * cw_ref[3:4, :]
        halo[i] = u[tt - SUBLANES:tt, :]

        for n in range(LRU_BLOCKS):
            cs = slice(n * LRU_BLOCK, (n + 1) * LRU_BLOCK)
            xb = xc[:, cs]
            ri = _dot(xb.astype(BF16), wri_ref[n])
            rg = jax.nn.sigmoid(ri[:, :LRU_BLOCK] + br_ref[:, cs])
            ig = jax.nn.sigmoid(ri[:, LRU_BLOCK:] + bi_ref[:, cs])
            log_a = -LRU_C * rg * _softplus(-lam_ref[:, cs])
            a = jnp.exp(log_a)
            b = jnp.sqrt(-jnp.tanh(log_a) * (a * a + 1.0)) * ig * xb
            d = 1
            while d < SUBLANES:
                keep = (rows & (SUBLANES - 1)) >= d
                a_sh = jnp.where(keep, pltpu.roll(a, d, axis=0), 1.0)
                b_sh = jnp.where(keep, pltpu.roll(b, d, axis=0), 0.0)
                b = b + a * b_sh
                a = a * a_sh
                d *= 2
            h_in = hprev[i, 0:1, cs]
            groups = []
            for r0 in range(0, tt, SUBLANES):
                h_grp = a[r0:r0 + SUBLANES] * h_in + b[r0:r0 + SUBLANES]
                groups.append(h_grp)
                h_in = h_grp[SUBLANES - 1:SUBLANES, :]
            h = jnp.concatenate(groups, axis=0)
            o_ref[i, :, cs] = (h * _gelu_tanh(gate_ref[i, :, cs])).astype(BF16)
            h_last = h[tt - 1:tt, :]
            hprev[i, :, cs] = jnp.broadcast_to(h_last, (SUBLANES, LRU_BLOCK))
            hfin_ref[i, :, cs] = h_last


def _lru(p1, cinit, cw, cb, wri, br, bi, lam, h0, bsz, t_len):
    tt = min(t_len, 64)
    nt = t_len // tt
    bb = _batch_block(bsz, t_len)
    p3 = p1.reshape(bsz, t_len, P1_COLS)
    rows = lambda w: (lambda b, j: (b, j, w))
    per_seq = lambda b, j: (b, 0, 0)
    const2 = lambda b, j: (0, 0)
    o, h_fin = pl.pallas_call(
        functools.partial(_lru_kernel, tt=tt, bb=bb),
        grid=(bsz // bb, nt),
        in_specs=[
            pl.BlockSpec((bb, tt, 1024), rows(0)),
            pl.BlockSpec((bb, tt, 1024), rows(1)),
            pl.BlockSpec((bb, SUBLANES, LRU_WIDTH), per_seq),
            pl.BlockSpec((SUBLANES, LRU_WIDTH), const2),
            pl.BlockSpec((1, LRU_WIDTH), const2),
            pl.BlockSpec((LRU_BLOCKS, LRU_BLOCK, 2 * LRU_BLOCK), lambda b, j: (0, 0, 0)),
            pl.BlockSpec((1, LRU_WIDTH), const2),
            pl.BlockSpec((1, LRU_WIDTH), const2),
            pl.BlockSpec((1, LRU_WIDTH), const2),
            pl.BlockSpec((bb, 1, LRU_WIDTH), per_seq),
        ],
        out_specs=[pl.BlockSpec((bb, tt, LRU_WIDTH), rows(0)),
                   pl.BlockSpec((bb, 1, LRU_WIDTH), per_seq)],
        out_shape=[jax.ShapeDtypeStruct((bsz, t_len, LRU_WIDTH), BF16),
                   jax.ShapeDtypeStruct((bsz, 1, LRU_WIDTH), F32)],
        scratch_shapes=[pltpu.VMEM((bb, SUBLANES, LRU_WIDTH), F32), pltpu.VMEM((bb, SUBLANES, LRU_WIDTH), F32)],
        compiler_params=_cparams(("parallel", "arbitrary")),
        name="lru",
    )(p3, p3, cinit, cw, cb, wri, br, bi, lam, h0)
    return o.reshape(bsz * t_len, LRU_WIDTH), h_fin


def _expand_heads(x, e_ref, terms):
    parts = []
    rest = x
    for _ in range(terms):
        part = rest.astype(BF16).astype(F32)
        parts.append(part)
        rest = rest - part
    rows = x.shape[0]
    out = _dot(jnp.concatenate(parts, axis=0), e_ref[...])
    acc = out[:rows]
    for t in range(1, terms):
        acc = acc + out[t * rows:(t + 1) * rows]
    return acc


def _ssd_kernel(z_ref, xbc_ref, dt_ref, cinit_ref, cw_ref, cb_ref, dtb_ref, alog_ref, dd_ref, nw_ref,
                ej_ref, ep_ref, s0_ref, o_ref, s_ref, halo, st_t, *, c, mm, bb):
    j = pl.program_id(1)
    hg = SSD_HEADS // SSD_GROUPS
    gw = SSD_INNER // SSD_GROUPS
    pair = 2 * SSD_HEADDIM

    @pl.when(j == 0)
    def _():
        halo[...] = cinit_ref[...]
        for i in range(bb):
            for q in range(SSD_HEADS // 2):
                two = jnp.concatenate([s0_ref[i, 2 * q], s0_ref[i, 2 * q + 1]], axis=0)
                st_t[i, :, q * pair:(q + 1) * pair] = two.T

    wj = SSD_HEADS * c
    row_t = _row_iota((c, wj))
    lane_t = lax.broadcasted_iota(jnp.int32, (c, wj), 1) & (c - 1)
    causal = lane_t <= row_t
    diag = lane_t == row_t
    bd_rows = hg * c
    bd_mask = ((_row_iota((bd_rows, gw)) >> (c.bit_length() - 1))
               == (lax.broadcasted_iota(jnp.int32, (bd_rows, gw), 1) >> (SSD_HEADDIM.bit_length() - 1)))
    neg_a = -jnp.exp(alog_ref[...])
    gs = SSD_GROUPS * SSD_STATE
    for i in range(bb):
        u = xbc_ref[i]
        ext = jnp.concatenate([halo[i], u], axis=0)
        xc = cb_ref[...] + _shifted(ext, 3, c) * cw_ref[0:1, :]
        xc = xc + _shifted(ext, 2, c) * cw_ref[1:2, :]
        xc = xc + _shifted(ext, 1, c) * cw_ref[2:3, :]
        xc = _silu(xc + u * cw_ref[3:4, :])
        halo[i] = u[c - SUBLANES:c, :]

        xs = xc[:, :SSD_INNER]
        bm = xc[:, SSD_INNER:SSD_INNER + gs].astype(mm)
        cm = xc[:, SSD_INNER + gs:].astype(mm)
        dt = _softplus(dt_ref[i] + dtb_ref[...])
        cs = _cumsum_rows(dt * neg_a, c)
        cs_p = _expand_heads(cs, ep_ref, 3)
        dt_p = _expand_heads(dt, ep_ref, 2)
        cs_j = cs_p if c == SSD_HEADDIM else _expand_heads(cs, ej_ref, 3)
        cs_row = jnp.sum(jnp.where(diag, cs_j, 0.0), axis=0, keepdims=True)
        lmat = jnp.exp(jnp.where(causal, cs_j - cs_row, -jnp.inf))
        last_p = cs_p[c - 1:c, :]
        e_cs = jnp.exp(cs_p)
        e_last = jnp.exp(last_p)
        xdt = xs * dt_p
        xw = (xdt * jnp.exp(last_p - cs_p)).astype(mm)
        xdt = xdt.astype(mm)
        y_groups = []
        for g in range(SSD_GROUPS):
            gl = slice(g * gw, (g + 1) * gw)
            ss = slice(g * SSD_STATE, (g + 1) * SSD_STATE)
            cb_x = _dot_nt(cm[:, ss], jnp.concatenate([bm[:, ss]] * hg, axis=0))
            m_x = (cb_x * lmat[:, g * bd_rows:(g + 1) * bd_rows]).astype(mm)
            blocks = jnp.where(bd_mask, jnp.concatenate([xdt[:, gl]] * hg, axis=0), jnp.zeros((), mm))
            st = st_t[i, :, gl]
            y_groups.append(_dot(m_x, blocks) + e_cs[:, gl] * _dot(cm[:, ss], st.astype(mm)))
            st_t[i, :, gl] = st * e_last[:, gl] + _dot_tn(bm[:, ss], xw[:, gl])
        y = (jnp.concatenate(y_groups, axis=1) + dd_ref[...] * xs) * _silu(z_ref[i])
        for g in range(SSD_GROUPS):
            gl = slice(g * gw, (g + 1) * gw)
            o_ref[i, :, gl] = _rms(y[:, gl], nw_ref[:, gl], 1e-5).astype(BF16)

    @pl.when(j == pl.num_programs(1) - 1)
    def _():
        for i in range(bb):
            for q in range(SSD_HEADS // 2):
                two = st_t[i, :, q * pair:(q + 1) * pair].T
                s_ref[i, 2 * q] = two[:SSD_HEADDIM]
                s_ref[i, 2 * q + 1] = two[SSD_HEADDIM:]


def _ssd(p1, cinit, cw, cb, dtb, alog, dd, nw, s0, bsz, t_len):
    c = min(SSD_CHUNK, t_len)
    nt = t_len // c
    bb = _batch_block(bsz, t_len)
    mm = BF16 if c % 16 == 0 else F32
    p3 = p1.reshape(bsz, t_len, P1_COLS)
    rows = lambda w: (lambda b, j: (b, j, w))
    const2 = lambda b, j: (0, 0)
    st_spec = pl.BlockSpec((bb, SSD_HEADS, SSD_HEADDIM, SSD_STATE), lambda b, j: (b, 0, 0, 0))
    head = jnp.arange(LANES, dtype=jnp.int32)[:, None]
    expand = lambda width: (jnp.arange(SSD_HEADS * width, dtype=jnp.int32)[None, :] // width == head).astype(F32)
    o, s_new = pl.pallas_call(
        functools.partial(_ssd_kernel, c=c, mm=mm, bb=bb),
        grid=(bsz // bb, nt),
        in_specs=[
            pl.BlockSpec((bb, c, 1024), rows(2)),
            pl.BlockSpec((bb, c, SSD_CONV_CH), rows(2)),
            pl.BlockSpec((bb, c, LANES), rows(36)),
            pl.BlockSpec((bb, SUBLANES, SSD_CONV_CH), lambda b, j: (b, 0, 0)),
            pl.BlockSpec((SUBLANES, SSD_CONV_CH), const2),
            pl.BlockSpec((1, SSD_CONV_CH), const2),
            pl.BlockSpec((1, LANES), const2),
            pl.BlockSpec((1, LANES), const2),
            pl.BlockSpec((1, SSD_INNER), const2),
            pl.BlockSpec((1, SSD_INNER), const2),
            pl.BlockSpec((LANES, SSD_HEADS * c), const2),
            pl.BlockSpec((LANES, SSD_INNER), const2),
            st_spec,
        ],
        out_specs=[pl.BlockSpec((bb, c, SSD_INNER), rows(0)), st_spec],
        out_shape=[jax.ShapeDtypeStruct((bsz, t_len, SSD_INNER), BF16),
                   jax.ShapeDtypeStruct((bsz, SSD_HEADS, SSD_HEADDIM, SSD_STATE), F32)],
        scratch_shapes=[pltpu.VMEM((bb, SUBLANES, SSD_CONV_CH), F32),
                        pltpu.VMEM((bb, SSD_STATE, SSD_INNER), F32)],
        compiler_params=_cparams(("parallel", "arbitrary")),
        name="ssd",
    )(p3, p3, p3, cinit, cw, cb, dtb, alog, dd, nw, expand(c), expand(SSD_HEADDIM), s0)
    return o.reshape(bsz * t_len, SSD_INNER), s_new


def _pad_rows_to_8(x):
    return jnp.pad(x, ((0, 0), (SUBLANES - x.shape[1], 0), (0, 0)))


def _pad_lanes(x, width):
    return jnp.pad(x, ((0, 0), (0, width - x.shape[1])))


def _trunk(x, states, w):
    s_gla, s_rwkv, s_shift, s_lru, s_lru_conv, s_ssd, s_ssd_conv = states
    bsz, t_len, d = x.shape
    n = bsz * t_len
    x2 = x.reshape(n, d)

    p0 = _in_proj(x2, w["g_mix0"], w["w_in0"], P0_TN)
    o_gla, s_gla_new = _gla(p0, s_gla, w["gla_wa2"], w["gla_ba"], w["gla_gn"], bsz, t_len)
    if bsz * RWKV_HEADS == LANES and t_len % RELAYOUT_STEPS == 0:
        y_rwkv, g, s_rwkv_new = _rwkv_paired(p0, s_shift, s_rwkv, w, bsz, t_len)
    else:
        y_rwkv, g, s_rwkv_new = _rwkv_general(p0, s_shift, s_rwkv, w, bsz, t_len)
    shift_new = p0.reshape(bsz, t_len, P0_COLS)[:, -1, 3072:3072 + RWKV_COLS]
    x2 = _out_ffn(x2, o_gla, y_rwkv, g, w["w_out0"], w["g_ffn0"], w["wg0"], w["wu0"], w["wd0"], None)

    p1 = _in_proj(x2, w["g_mix1"], w["w_in1"], P1_TN)
    lru_out, h_lru = _lru(p1, _pad_rows_to_8(s_lru_conv), w["lru_cw"], w["lru_cb"], w["lru_wri"], w["lru_br"],
                          w["lru_bi"], w["lru_lam"], s_lru.reshape(bsz, 1, LRU_WIDTH), bsz, t_len)
    y_ssd, s_ssd_new = _ssd(p1, _pad_rows_to_8(s_ssd_conv), w["ssd_cw"], w["ssd_cb"], w["ssd_dtb"], w["ssd_alog"],
                            w["ssd_dd"], w["ssd_nw"], s_ssd, bsz, t_len)
    keep = CONV_W - 1
    p1_tail = p1.reshape(bsz, t_len, P1_COLS)[:, -keep:]
    lru_conv_new = p1_tail[:, :, 1024:2048]
    ssd_conv_new = p1_tail[:, :, 3072:3072 + SSD_CONV_CH]
    y = _out_ffn(x2, lru_out, y_ssd, None, w["w_out1"], w["g_ffn1"], w["wg1"], w["wu1"], w["wd1"], w["g_final"])
    return y.reshape(bsz, t_len, d), (s_gla_new, s_rwkv_new, shift_new, h_lru.reshape(bsz, LRU_WIDTH),
                                      lru_conv_new, s_ssd_new, ssd_conv_new)


def kernel(x_prompt, x_sample, state_gla, state_rwkv, state_rwkv_shift, state_lru, state_lru_conv, state_ssd, state_ssd_conv, w_in0, gla_w_a2, gla_b_a, gla_g_norm, rwkv_mu, rwkv_w0, rwkv_w2, rwkv_a0, rwkv_a2, rwkv_g2, rwkv_k_k, rwkv_k_a, rwkv_r_k, rwkv_ln_w, rwkv_ln_b, w_out0, w_in1, lru_conv_w, lru_conv_b, lru_w_r, lru_b_r, lru_w_i, lru_b_i, lru_lambda, ssd_conv_w, ssd_conv_b, ssd_dt_bias, ssd_a_log, ssd_d, ssd_norm_w, w_out1, g_mix, g_ffn, w_ffn_gate, w_ffn_up, w_ffn_down, g_final):
    row = lambda p: p.reshape(1, -1).astype(F32)
    w0 = jnp.concatenate([w_in0[:, :2048], w_in0[:, 2064:3088], w_in0[:, 3088:],
                          _pad_lanes(w_in0[:, 2048:2064], LANES)], axis=1)
    zeros64 = jnp.zeros((64, RWKV_WIDTH), F32)
    rwkv_wa = jnp.concatenate([jnp.concatenate([rwkv_w2, zeros64], axis=1),
                               jnp.concatenate([zeros64, rwkv_a2], axis=1)], axis=0)
    w = {
        "w_in0": w0.astype(BF16),
        "w_in1": _pad_lanes(w_in1, P1_COLS).astype(BF16),
        "g_mix0": g_mix[0], "g_mix1": g_mix[1], "g_ffn0": g_ffn[0], "g_ffn1": g_ffn[1], "g_final": g_final,
        "gla_wa2": jnp.pad(gla_w_a2, ((0, LANES - GLA_GATE_RANK), (0, 0))).astype(BF16),
        "gla_ba": row(gla_b_a), "gla_gn": row(gla_g_norm),
        "rwkv_mu": row(rwkv_mu), "rwkv_w0": row(rwkv_w0), "rwkv_a0": row(rwkv_a0),
        "rwkv_wa": rwkv_wa.astype(BF16), "rwkv_g2": rwkv_g2.astype(BF16),
        "rwkv_tiles": [_chain_tile(p) for p in (rwkv_k_k, rwkv_k_a, rwkv_r_k, rwkv_ln_w, rwkv_ln_b)],
        "rwkv_tiles_paired": [_chain_tile_paired(p)
                              for p in (rwkv_k_k, rwkv_k_a, rwkv_r_k, rwkv_ln_w, rwkv_ln_b)],
        "w_out0": w_out0.astype(BF16), "w_out1": w_out1.astype(BF16),
        "wg0": w_ffn_gate[0].astype(BF16), "wu0": w_ffn_up[0].astype(BF16), "wd0": w_ffn_down[0].astype(BF16),
        "wg1": w_ffn_gate[1].astype(BF16), "wu1": w_ffn_up[1].astype(BF16), "wd1": w_ffn_down[1].astype(BF16),
        "lru_cw": jnp.pad(lru_conv_w, ((0, SUBLANES - CONV_W), (0, 0))), "lru_cb": row(lru_conv_b),
        "lru_wri": jnp.concatenate([lru_w_r, lru_w_i], axis=2).astype(BF16),
        "lru_br": row(lru_b_r), "lru_bi": row(lru_b_i), "lru_lam": row(lru_lambda),
        "ssd_cw": jnp.pad(ssd_conv_w, ((0, SUBLANES - CONV_W), (0, 0))), "ssd_cb": row(ssd_conv_b),
        "ssd_dtb": _pad_lanes(row(ssd_dt_bias), LANES), "ssd_alog": _pad_lanes(row(ssd_a_log), LANES),
        "ssd_dd": row(jnp.repeat(ssd_d, SSD_HEADDIM)), "ssd_nw": row(ssd_norm_w),
    }
    bp = x_prompt.shape[0]
    prompt_init = (
        jnp.zeros((bp, GLA_HEADS, GLA_DK, GLA_DV), F32),
        jnp.zeros((bp, RWKV_HEADS, RWKV_HEAD, RWKV_HEAD), F32),
        jnp.zeros((bp, RWKV_COLS), F32),
        jnp.zeros((bp, LRU_WIDTH), F32),
        jnp.zeros((bp, CONV_W - 1, LRU_WIDTH), F32),
        jnp.zeros((bp, SSD_HEADS, SSD_HEADDIM, SSD_STATE), F32),
        jnp.zeros((bp, CONV_W - 1, SSD_CONV_CH), F32),
    )
    sample_init = (state_gla, state_rwkv, state_rwkv_shift, state_lru, state_lru_conv, state_ssd, state_ssd_conv)
    y_prompt, p_states = _trunk(x_prompt, prompt_init, w)
    y_sample, s_states = _trunk(x_sample, sample_init, w)
    return (y_prompt, y_sample, *p_states, *s_states)
```

```python
import functools

import jax
import jax.numpy as jnp
from jax import lax
from jax.experimental import pallas as pl
from jax.experimental.pallas import tpu as pltpu

F32 = jnp.float32
BF16 = jnp.bfloat16

D_MODEL = 1024
NORM_EPS = 1e-6
GLA_HEADS = 4
GLA_DK = 128
GLA_DV = 256
GLA_GATE_RANK = 16
GLA_GATE_NORM = 16.0
GLA_CHUNK = 64
RWKV_HEAD = 64
RWKV_HEADS = 16
RWKV_WIDTH = 1024
RWKV_LORA = 256
RWKV_COLS = 3 * RWKV_WIDTH + RWKV_LORA
RWKV_GN_EPS = 64e-5
LRU_WIDTH = 1024
LRU_BLOCKS = 8
LRU_BLOCK = 128
LRU_C = 8.0
CONV_W = 4
SSD_INNER = 1024
SSD_HEADDIM = 64
SSD_HEADS = 16
SSD_GROUPS = 2
SSD_STATE = 128
SSD_CHUNK = 64
SSD_CONV_CH = SSD_INNER + 2 * SSD_GROUPS * SSD_STATE
D_FF = 2816

LANES = 128
SUBLANES = 8
VMEM_LIMIT_BYTES = 56 * 1024 * 1024

P0_COLS = 6528
P0_TN = 2176
P1_COLS = 4864
P1_TN = 2432


def _cparams(semantics):
    return pltpu.CompilerParams(dimension_semantics=semantics, vmem_limit_bytes=VMEM_LIMIT_BYTES)


def _dot(a, b):
    return jnp.dot(a, b, preferred_element_type=F32)


def _dot_nt(a, b):
    return lax.dot_general(a, b, (((1,), (1,)), ((), ())), preferred_element_type=F32)


def _dot_tn(a, b):
    return lax.dot_general(a, b, (((0,), (0,)), ((), ())), preferred_element_type=F32)


def _softplus(x):
    return jnp.maximum(x, 0.0) + jnp.log1p(jnp.exp(-jnp.abs(x)))


def _silu(x):
    return x * jax.nn.sigmoid(x)


def _gelu_tanh(x):
    return 0.5 * x * (1.0 + jnp.tanh(0.7978845608028654 * (x + 0.044715 * (x * x * x))))


def _rms(x, g, eps):
    return x * lax.rsqrt(jnp.mean(x * x, axis=-1, keepdims=True) + eps) * g


def _row_iota(shape):
    return lax.broadcasted_iota(jnp.int32, shape, 0)


def _cumsum_rows(x, seg):
    rows = _row_iota(x.shape) & (seg - 1)
    d = 1
    while d < seg:
        x = x + jnp.where(rows >= d, pltpu.roll(x, d, axis=0), 0.0)
        d *= 2
    return x


def _shifted(ext, k, rows):
    return pltpu.roll(ext, k, axis=0)[SUBLANES:SUBLANES + rows]


def _in_proj_kernel(x_ref, g_ref, w_ref, o_ref, h_scr):
    @pl.when(pl.program_id(1) == 0)
    def _():
        h_scr[...] = _rms(x_ref[...], g_ref[...], NORM_EPS).astype(BF16)

    o_ref[...] = _dot(h_scr[...], w_ref[...])


def _in_proj(x2d, g, w, tn):
    n, d = x2d.shape
    cols = w.shape[1]
    tm = min(n, 1024)
    return pl.pallas_call(
        _in_proj_kernel,
        grid=(n // tm, cols // tn),
        in_specs=[
            pl.BlockSpec((tm, d), lambda i, j: (i, 0)),
            pl.BlockSpec((1, d), lambda i, j: (0, 0)),
            pl.BlockSpec((d, tn), lambda i, j: (0, j)),
        ],
        out_specs=pl.BlockSpec((tm, tn), lambda i, j: (i, j)),
        out_shape=jax.ShapeDtypeStruct((n, cols), F32),
        scratch_shapes=[pltpu.VMEM((tm, d), BF16)],
        compiler_params=_cparams(("parallel", "arbitrary")),
        name="in_proj",
    )(x2d, g.reshape(1, d), w)


def _out_ffn_kernel(*refs, has_gate, final_norm):
    refs = list(refs)
    x_ref, oa_ref, ob_ref = refs[:3]
    pos = 3
    gate_ref = None
    if has_gate:
        gate_ref = refs[pos]
        pos += 1
    woa_ref, wob_ref, gffn_ref, wg_ref, wu_ref, wd_ref = refs[pos:pos + 6]
    pos += 6
    gfin_ref = None
    if final_norm:
        gfin_ref = refs[pos]
        pos += 1
    out_ref, h_scr = refs[pos:pos + 2]

    k = pl.program_id(1)

    @pl.when(k == 0)
    def _():
        ob = ob_ref[...]
        if has_gate:
            ob = (ob * gate_ref[...]).astype(BF16)
        x1 = x_ref[...] + (_dot(oa_ref[...], woa_ref[...]) + _dot(ob, wob_ref[...]))
        out_ref[...] = x1
        h_scr[...] = _rms(x1, gffn_ref[...], NORM_EPS).astype(BF16)

    h = h_scr[...]
    act = (_silu(_dot(h, wg_ref[...])) * _dot(h, wu_ref[...])).astype(BF16)
    out_ref[...] += _dot(act, wd_ref[...])

    if final_norm:
        @pl.when(k == pl.num_programs(1) - 1)
        def _():
            out_ref[...] = _rms(out_ref[...], gfin_ref[...], NORM_EPS)


def _out_ffn(x2d, oa, ob, gate, wo, g_ffn, wg, wu, wd, g_final):
    n, d = x2d.shape
    has_gate = gate is not None
    final_norm = g_final is not None
    tm = min(n, 512)
    tf = D_FF // 2
    row = lambda i, k: (i, 0)
    const = lambda i, k: (0, 0)
    args = [x2d, oa, ob]
    in_specs = [pl.BlockSpec((tm, d), row)] * 3
    if has_gate:
        args.append(gate)
        in_specs.append(pl.BlockSpec((tm, d), row))
    half = wo.shape[0] // 2
    args += [wo, wo, g_ffn.reshape(1, d), wg, wu, wd]
    in_specs += [
        pl.BlockSpec((half, d), const),
        pl.BlockSpec((half, d), lambda i, k: (1, 0)),
        pl.BlockSpec((1, d), const),
        pl.BlockSpec((d, tf), lambda i, k: (0, k)),
        pl.BlockSpec((d, tf), lambda i, k: (0, k)),
        pl.BlockSpec((tf, d), lambda i, k: (k, 0)),
    ]
    if final_norm:
        args.append(g_final.reshape(1, d))
        in_specs.append(pl.BlockSpec((1, d), const))
    return pl.pallas_call(
        functools.partial(_out_ffn_kernel, has_gate=has_gate, final_norm=final_norm),
        grid=(n // tm, D_FF // tf),
        in_specs=in_specs,
        out_specs=pl.BlockSpec((tm, d), row),
        out_shape=jax.ShapeDtypeStruct((n, d), F32),
        scratch_shapes=[pltpu.VMEM((tm, d), BF16)],
        compiler_params=_cparams(("parallel", "arbitrary")),
        name="out_ffn",
    )(*args)


def _gla_kernel(q_ref, k_ref, v_ref, og_ref, al_ref, wa2_ref, ba_ref, gn_ref, s0_ref,
                o_ref, sout_ref, st_scr, *, c, mm, bb):
    n = pl.program_id(1)

    @pl.when(n == 0)
    def _():
        for i in range(bb):
            for h in range(GLA_HEADS):
                st_scr[i * GLA_HEADS + h] = s0_ref[i, h].T

    mask = _row_iota((c, c)) >= lax.broadcasted_iota(jnp.int32, (c, c), 1)
    gn = gn_ref[...]
    for i in range(bb):
        la_all = _dot(al_ref[i].astype(BF16), wa2_ref[...]) + ba_ref[...]
        la_all = -_softplus(-la_all) * (1.0 / GLA_GATE_NORM)
        b_all = _cumsum_rows(la_all, c)
        for h in range(GLA_HEADS):
            ks = slice(h * GLA_DK, (h + 1) * GLA_DK)
            vs = slice(h * GLA_DV, (h + 1) * GLA_DV)
            b = b_all[:, ks]
            b_last = b[c - 1:c, :]
            q = q_ref[i, :, ks] * (GLA_DK ** -0.5)
            k = k_ref[i, :, ks]
            v = v_ref[i, :, vs].astype(mm)
            qd = (q * jnp.exp(b)).astype(mm)
            kd = (k * jnp.exp(-b)).astype(mm)
            kc = (k * jnp.exp(b_last - b)).astype(mm)
            scores = jnp.where(mask, _dot_nt(qd, kd), 0.0)
            st = st_scr[i * GLA_HEADS + h]
            o = _dot(scores.astype(mm), v) + _dot_nt(qd, st.astype(mm))
            st_scr[i * GLA_HEADS + h] = st * jnp.exp(b_last) + _dot_tn(v, kc)
            o = _rms(o, gn, 1e-5)
            o_ref[i, :, vs] = (o * _silu(og_ref[i, :, vs])).astype(BF16)

    @pl.when(n == pl.num_programs(1) - 1)
    def _():
        for i in range(bb):
            for h in range(GLA_HEADS):
                sout_ref[i, h] = st_scr[i * GLA_HEADS + h].T


def _batch_block(bsz, t_len):
    return min(bsz, 4) if t_len >= 64 else min(bsz, 8)


def _gla(p0, s0, wa2, ba, gn, bsz, t_len):
    c = min(GLA_CHUNK, t_len)
    nt = t_len // c
    bb = _batch_block(bsz, t_len)
    mm = BF16 if c % 16 == 0 else F32
    p3 = p0.reshape(bsz, t_len, P0_COLS)
    rows = lambda w: (lambda b, j: (b, j, w))
    const = lambda b, j: (0, 0)
    st_spec = pl.BlockSpec((bb, GLA_HEADS, GLA_DK, GLA_DV), lambda b, j: (b, 0, 0, 0))
    o, s_new = pl.pallas_call(
        functools.partial(_gla_kernel, c=c, mm=mm, bb=bb),
        grid=(bsz // bb, nt),
        in_specs=[
            pl.BlockSpec((bb, c, 512), rows(0)),
            pl.BlockSpec((bb, c, 512), rows(1)),
            pl.BlockSpec((bb, c, 1024), rows(1)),
            pl.BlockSpec((bb, c, 1024), rows(2)),
            pl.BlockSpec((bb, c, LANES), rows(50)),
            pl.BlockSpec((LANES, 512), const),
            pl.BlockSpec((1, 512), const),
            pl.BlockSpec((1, GLA_DV), const),
            st_spec,
        ],
        out_specs=[pl.BlockSpec((bb, c, 1024), rows(0)), st_spec],
        out_shape=[jax.ShapeDtypeStruct((bsz, t_len, 1024), BF16),
                   jax.ShapeDtypeStruct((bsz, GLA_HEADS, GLA_DK, GLA_DV), F32)],
        scratch_shapes=[pltpu.VMEM((bb * GLA_HEADS, GLA_DV, GLA_DK), F32)],
        compiler_params=_cparams(("parallel", "arbitrary")),
        name="gla",
    )(p3, p3, p3, p3, p3, wa2, ba, gn, s0)
    return o.reshape(bsz * t_len, 1024), s_new


def _rwkv_prep_kernel(r_ref, k_ref, v_ref, l_ref, ir_ref, ik_ref, iv_ref, il_ref,
                      mu_ref, w0_ref, a0_ref, wa_ref, g2_ref,
                      ro_ref, wo_ref, ko_ref, vo_ref, ao_ref, go_ref, carry,
                      *, tm, t_len, multi_seq):
    j = pl.program_id(1)
    cols = ((0, 1024), (1024, 2048), (2048, 3072), (3072, RWKV_COLS))

    if not multi_seq:
        @pl.when(j == 0)
        def _():
            for (lo, hi), iref in zip(cols, (ir_ref, ik_ref, iv_ref, il_ref)):
                carry[:, lo:hi] = iref[...]

    def mixed(x_ref, i_ref, lo, hi):
        x = x_ref[...]
        rolled = pltpu.roll(x, 1, axis=0)
        rows = _row_iota(x.shape)
        if multi_seq:
            prev = jnp.where((rows & (t_len - 1)) == 0, i_ref[...], rolled)
        else:
            prev = jnp.where(rows == 0, carry[SUBLANES - 1:SUBLANES, lo:hi], rolled)
            carry[:, lo:hi] = x[tm - SUBLANES:tm, :]
        return x + (prev - x) * mu_ref[:, lo:hi]

    r = mixed(r_ref, ir_ref, *cols[0])
    k = mixed(k_ref, ik_ref, *cols[1])
    v = mixed(v_ref, iv_ref, *cols[2])
    lo_rank = mixed(l_ref, il_ref, *cols[3])

    wa_in = lo_rank[:, :LANES]
    lane = lax.broadcasted_iota(jnp.int32, wa_in.shape, 1)
    wa_in = jnp.where(lane < 64, jnp.tanh(wa_in), wa_in).astype(BF16)
    wa = _dot(wa_in, wa_ref[...])
    g = _dot(jax.nn.sigmoid(lo_rank[:, LANES:]).astype(BF16), g2_ref[...])
    w = -_softplus(-(w0_ref[...] + wa[:, :RWKV_WIDTH])) - 0.5
    ro_ref[...] = r
    wo_ref[...] = -jnp.exp(w)
    ko_ref[...] = k
    vo_ref[...] = v
    ao_ref[...] = jax.nn.sigmoid(a0_ref[...] + wa[:, RWKV_WIDTH:])
    go_ref[...] = g


def _rwkv_prep(p0, shift8, mu, w0, a0, wa, g2, bsz, t_len):
    n = bsz * t_len
    multi_seq = t_len < 256
    tm = min(n, 512) if multi_seq else 256
    nt = 1 if multi_seq else t_len // tm
    nb = n // tm if multi_seq else bsz
    ti = tm if multi_seq else SUBLANES
    rows = lambda w: (lambda i, j: (i * nt + j, w))
    init = lambda w: (lambda i, j: (i, w))
    const = lambda i, j: (0, 0)
    out = jax.ShapeDtypeStruct((n, RWKV_WIDTH), F32)
    return pl.pallas_call(
        functools.partial(_rwkv_prep_kernel, tm=tm, t_len=t_len, multi_seq=multi_seq),
        grid=(nb, nt),
        in_specs=[
            pl.BlockSpec((tm, 1024), rows(3)),
            pl.BlockSpec((tm, 1024), rows(4)),
            pl.BlockSpec((tm, 1024), rows(5)),
            pl.BlockSpec((tm, RWKV_LORA), rows(24)),
            pl.BlockSpec((ti, 1024), init(0)),
            pl.BlockSpec((ti, 1024), init(1)),
            pl.BlockSpec((ti, 1024), init(2)),
            pl.BlockSpec((ti, RWKV_LORA), init(12)),
            pl.BlockSpec((1, RWKV_COLS), const),
            pl.BlockSpec((1, RWKV_WIDTH), const),
            pl.BlockSpec((1, RWKV_WIDTH), const),
            pl.BlockSpec((LANES, 2 * RWKV_WIDTH), const),
            pl.BlockSpec((LANES, RWKV_WIDTH), const),
        ],
        out_specs=[pl.BlockSpec((tm, RWKV_WIDTH), rows(0))] * 6,
        out_shape=[out] * 6,
        scratch_shapes=[pltpu.VMEM((SUBLANES, RWKV_COLS), F32)],
        compiler_params=_cparams(("parallel", "arbitrary")),
        name="rwkv_prep",
    )(p0, p0, p0, p0, shift8, shift8, shift8, shift8, mu, w0, a0, wa, g2)


def _rwkv_scan_kernel(r_ref, lw_ref, k_ref, a_ref, v_ref, kk_ref, ka_ref, rk_ref, lnw_ref, lnb_ref,
                      s0_ref, y_ref, s_ref, kb_scr, kp_scr, rs_scr, kn_scr, bonus_scr, grow_scr, *, tt):
    @pl.when(pl.program_id(1) == 0)
    def _():
        s_ref[...] = s0_ref[...]

    def tile_row(ref, j):
        return ref[pl.ds(j, 1), :]

    def bcast(x):
        return jnp.broadcast_to(x, (RWKV_HEAD, LANES))

    norm2 = jnp.zeros((tt, LANES), F32)
    for j in range(RWKV_HEAD):
        kj = k_ref[0, j] * tile_row(kk_ref, j)
        norm2 = norm2 + kj * kj
    norm = jnp.maximum(jnp.sqrt(norm2), 1e-12)
    bonus = jnp.zeros((tt, LANES), F32)
    s_kk = jnp.zeros((RWKV_HEAD, LANES), F32)
    for j in range(RWKV_HEAD):
        k = k_ref[0, j]
        a = a_ref[0, j]
        r = r_ref[0, j]
        kk = k * tile_row(kk_ref, j) / norm
        kp = k * (1.0 + (a - 1.0) * tile_row(ka_ref, j))
        log_g = _cumsum_rows(lw_ref[0, j], tt)
        grow = jnp.exp(log_g)
        shrink = jnp.exp(-log_g)
        kb_scr[j] = kk * a * shrink
        kp_scr[j] = kp * shrink
        rs_scr[j] = r * grow
        kn_scr[j] = pltpu.roll(kk, tt - 1, axis=0) * grow
        grow_scr[pl.ds(j, 1), :] = grow[tt - 1:tt, :]
        bonus = bonus + r * kp * tile_row(rk_ref, j)
        s_kk = s_kk + s_ref[j] * bcast(kk[0:1, :])
    bonus_scr[...] = bonus

    def row(ref, j, s):
        return bcast(ref[j, pl.ds(s, 1), :])

    def sweep(s, s_kk):
        sa = -s_kk
        v = v_ref[s]
        y = jnp.zeros_like(v)
        s_kk_next = jnp.zeros_like(v)
        for j in range(RWKV_HEAD):
            sn = s_ref[j] + sa * row(kb_scr, j, s) + v * row(kp_scr, j, s)
            s_ref[j] = sn
            y = y + sn * row(rs_scr, j, s)
            s_kk_next = s_kk_next + sn * row(kn_scr, j, s)
        return s_kk_next, y

    def emit(s, y):
        mu = jnp.mean(y, axis=0, keepdims=True)
        yc = y - mu
        var = jnp.mean(yc * yc, axis=0, keepdims=True)
        y_ref[s] = (yc * lax.rsqrt(var + RWKV_GN_EPS) * lnw_ref[...] + lnb_ref[...]
                    + bonus_scr[pl.ds(s, 1), :] * v_ref[s])

    def step(s, carry):
        s_kk, y_prev = carry
        emit(s - 1, y_prev)
        return sweep(s, s_kk)

    _, y_last = lax.fori_loop(1, tt, step, sweep(0, s_kk))
    emit(tt - 1, y_last)
    for j in range(RWKV_HEAD):
        s_ref[j] = s_ref[j] * bcast(tile_row(grow_scr, j))


def _rwkv_scan(r, lw, k, a, v, tiles, s0, tt):
    nb, _, _, nc = r.shape
    seq = pl.BlockSpec((1, RWKV_HEAD, tt, LANES), lambda g, t: (t, 0, 0, g))
    steps = pl.BlockSpec((tt, RWKV_HEAD, LANES), lambda g, t: (t, 0, g))
    shared = tiles[0].shape[0] == RWKV_HEAD
    tile = pl.BlockSpec((RWKV_HEAD, LANES), (lambda g, t: (0, 0)) if shared else (lambda g, t: (g, 0)))
    st = pl.BlockSpec((RWKV_HEAD, RWKV_HEAD, LANES), lambda g, t: (0, 0, g))
    return pl.pallas_call(
        functools.partial(_rwkv_scan_kernel, tt=tt),
        grid=(nc // LANES, nb),
        in_specs=[seq] * 4 + [steps] + [tile] * 5 + [st],
        out_specs=[steps, st],
        out_shape=[jax.ShapeDtypeStruct((nb * tt, RWKV_HEAD, nc), F32),
                   jax.ShapeDtypeStruct((RWKV_HEAD, RWKV_HEAD, nc), F32)],
        scratch_shapes=[pltpu.VMEM((RWKV_HEAD, tt, LANES), F32)] * 4
        + [pltpu.VMEM((tt, LANES), F32), pltpu.VMEM((RWKV_HEAD, LANES), F32)],
        compiler_params=_cparams(("parallel", "arbitrary")),
        name="rwkv_scan",
    )(r, lw, k, a, v, *tiles, s0)


RELAYOUT_PITCH = LANES + SUBLANES
RELAYOUT_STEPS = LANES
SCAN_STEPS = 32


def _chain_input_kernel(*refs, kind, step_major):
    x_ref, init_ref, mu_ref = refs[:3]
    bias_ref = wmat_ref = z_scr = None
    if kind == "plain":
        o_ref, carry, z_scr = refs[3:]
    elif kind == "gate":
        wmat_ref, o_ref, carry = refs[3:]
    else:
        bias_ref, wmat_ref, o_ref, carry, z_scr = refs[3:]

    @pl.when(pl.program_id(0) == 0)
    def _():
        carry[...] = init_ref[...]

    nseq, steps, width = x_ref.shape
    pairs = RWKV_HEADS // 2
    rows = _row_iota((steps, width))
    for b in range(nseq):
        x = x_ref[b]
        prev = jnp.where(rows == 0, carry[b, SUBLANES - 1:SUBLANES, :], pltpu.roll(x, 1, axis=0))
        carry[b] = x[steps - SUBLANES:, :]
        mixed = x + (prev - x) * mu_ref[...]
        if kind == "plain":
            val = mixed
        elif kind == "gate":
            o_ref[b] = _dot(jax.nn.sigmoid(mixed[:, LANES:]).astype(BF16), wmat_ref[...])
            continue
        else:
            low = mixed[:, :LANES]
            lane = lax.broadcasted_iota(jnp.int32, low.shape, 1)
            low = jnp.where(lane < 64, jnp.tanh(low), low).astype(BF16)
            pre = bias_ref[...] + _dot(low, wmat_ref[...])
            val = -jnp.exp(-_softplus(-pre) - 0.5) if kind == "decay" else jax.nn.sigmoid(pre)
        for hp in range(pairs):
            q = b * pairs + hp
            z_scr[q * RELAYOUT_PITCH:q * RELAYOUT_PITCH + LANES, :] = val[:, hp * LANES:(hp + 1) * LANES].T
    if kind == "gate":
        return
    half = nseq * pairs
    for j in range(RWKV_HEAD):
        m = jnp.concatenate([z_scr[pl.ds(j, half, stride=RELAYOUT_PITCH), :],
                             z_scr[pl.ds(RWKV_HEAD + j, half, stride=RELAYOUT_PITCH), :]], axis=0)
        mt = m.T
        if step_major:
            o_ref[pl.ds(j, steps, stride=RWKV_HEAD), :] = mt
        else:
            for tq in range(steps // SCAN_STEPS):
                o_ref[tq, j] = mt[tq * SCAN_STEPS:(tq + 1) * SCAN_STEPS]


def _chain_input(p3, shift3, mu, col, init_col, kind, step_major=False, bias=None, wmat=None):
    bsz, t_len, _ = p3.shape
    width = mu.shape[1]
    const = lambda t: (0, 0)
    args = [p3, shift3, mu]
    in_specs = [pl.BlockSpec((bsz, RELAYOUT_STEPS, width), lambda t: (0, t, col)),
                pl.BlockSpec((bsz, SUBLANES, width), lambda t: (0, 0, init_col)),
                pl.BlockSpec((1, width), const)]
    if bias is not None:
        args.append(bias)
        in_specs.append(pl.BlockSpec(bias.shape, const))
    if wmat is not None:
        args.append(wmat)
        in_specs.append(pl.BlockSpec(wmat.shape, const))
    scratch = [pltpu.VMEM((bsz, SUBLANES, width), F32)]
    if kind == "gate":
        out_spec = pl.BlockSpec((bsz, RELAYOUT_STEPS, RWKV_WIDTH), lambda t: (0, t, 0))
        out_shape = jax.ShapeDtypeStruct((bsz, t_len, RWKV_WIDTH), F32)
    else:
        scratch.append(pltpu.VMEM((bsz * (RWKV_HEADS // 2) * RELAYOUT_PITCH, LANES), F32))
        if step_major:
            out_spec = pl.BlockSpec((RELAYOUT_STEPS * RWKV_HEAD, LANES), lambda t: (t, 0))
            out_shape = jax.ShapeDtypeStruct((t_len * RWKV_HEAD, LANES), F32)
        else:
            per = RELAYOUT_STEPS // SCAN_STEPS
            out_spec = pl.BlockSpec((per, RWKV_HEAD, SCAN_STEPS, LANES), lambda t: (t, 0, 0, 0))
            out_shape = jax.ShapeDtypeStruct((t_len // SCAN_STEPS, RWKV_HEAD, SCAN_STEPS, LANES), F32)
    return pl.pallas_call(
        functools.partial(_chain_input_kernel, kind=kind, step_major=step_major),
        grid=(t_len // RELAYOUT_STEPS,),
        in_specs=in_specs,
        out_specs=out_spec,
        out_shape=out_shape,
        scratch_shapes=scratch,
        compiler_params=_cparams(("arbitrary",)),
        name="rwkv_" + kind,
    )(*args)


def _from_chains_kernel(y_ref, g_ref, o_ref, z_scr):
    nseq = o_ref.shape[0]
    pairs = RWKV_HEADS // 2
    half = nseq * pairs
    for i in range(RWKV_HEAD):
        mt = y_ref[pl.ds(i, RELAYOUT_STEPS, stride=RWKV_HEAD), :].T
        z_scr[pl.ds(i, half, stride=RELAYOUT_PITCH), :] = mt[:half]
        z_scr[pl.ds(RWKV_HEAD + i, half, stride=RELAYOUT_PITCH), :] = mt[half:]
    for b in range(nseq):
        for hp in range(pairs):
            q = b * pairs + hp
            lanes = slice(hp * LANES, (hp + 1) * LANES)
            y = z_scr[q * RELAYOUT_PITCH:q * RELAYOUT_PITCH + LANES, :].T
            o_ref[b, :, lanes] = (y * g_ref[b, :, lanes]).astype(BF16)


def _from_chains(y2, g3):
    bsz, t_len, _ = g3.shape
    natural = pl.BlockSpec((bsz, RELAYOUT_STEPS, RWKV_WIDTH), lambda t: (0, t, 0))
    return pl.pallas_call(
        _from_chains_kernel,
        grid=(t_len // RELAYOUT_STEPS,),
        in_specs=[pl.BlockSpec((RELAYOUT_STEPS * RWKV_HEAD, LANES), lambda t: (t, 0)), natural],
        out_specs=natural,
        out_shape=jax.ShapeDtypeStruct((bsz, t_len, RWKV_WIDTH), BF16),
        scratch_shapes=[pltpu.VMEM((bsz * (RWKV_HEADS // 2) * RELAYOUT_PITCH, LANES), F32)],
        compiler_params=_cparams(("arbitrary",)),
        name="from_chains",
    )(y2, g3)


def _rwkv_paired(p0, s_shift, s_rwkv, w, bsz, t_len):
    pairs = RWKV_HEADS // 2
    p3 = p0.reshape(bsz, t_len, P0_COLS)
    shift3 = jnp.broadcast_to(s_shift[:, None, :], (bsz, SUBLANES, RWKV_COLS))
    mu = w["rwkv_mu"]
    mu_main = [mu[:, i * RWKV_WIDTH:(i + 1) * RWKV_WIDTH] for i in range(3)]
    mu_low = mu[:, 3 * RWKV_WIDTH:]
    wa = w["rwkv_wa"]
    low_col = 3 * RWKV_WIDTH // RWKV_LORA
    r = _chain_input(p3, shift3, mu_main[0], 3, 0, "plain")
    k = _chain_input(p3, shift3, mu_main[1], 4, 1, "plain")
    v = _chain_input(p3, shift3, mu_main[2], 5, 2, "plain", step_major=True)
    lw = _chain_input(p3, shift3, mu_low, 24, low_col, "decay", bias=w["rwkv_w0"], wmat=wa[:, :RWKV_WIDTH])
    a = _chain_input(p3, shift3, mu_low, 24, low_col, "a", bias=w["rwkv_a0"], wmat=wa[:, RWKV_WIDTH:])
    g = _chain_input(p3, shift3, mu_low, 24, low_col, "gate", wmat=w["rwkv_g2"])
    s0 = s_rwkv.reshape(bsz, pairs, 2, RWKV_HEAD, RWKV_HEAD).transpose(4, 3, 2, 0, 1)
    y_c, s_c = _rwkv_scan(r, lw, k, a, v.reshape(t_len, RWKV_HEAD, LANES), w["rwkv_tiles_paired"],
                          s0.reshape(RWKV_HEAD, RWKV_HEAD, LANES), SCAN_STEPS)
    y = _from_chains(y_c.reshape(t_len * RWKV_HEAD, LANES), g)
    s_new = s_c.reshape(RWKV_HEAD, RWKV_HEAD, 2, bsz, pairs).transpose(3, 4, 2, 1, 0)
    return y.reshape(bsz * t_len, RWKV_WIDTH), None, s_new.reshape(bsz, RWKV_HEADS, RWKV_HEAD, RWKV_HEAD)


def _seq_lanes_kernel(x_ref, o_ref, *, step_major):
    for t in range(x_ref.shape[1]):
        tile_t = x_ref[:, t, :].T
        for hh in range(2):
            piece = tile_t[hh * RWKV_HEAD:(hh + 1) * RWKV_HEAD]
            lanes = slice(hh * LANES, (hh + 1) * LANES)
            if step_major:
                o_ref[t, :, lanes] = piece
            else:
                o_ref[0, :, t, lanes] = piece


def _seq_lanes(x2d, bsz, t_len, step_major):
    nc = RWKV_HEADS * bsz
    if step_major:
        out_spec = pl.BlockSpec((t_len, RWKV_HEAD, 2 * LANES), lambda hp: (0, 0, hp))
        out_shape = jax.ShapeDtypeStruct((t_len, RWKV_HEAD, nc), F32)
    else:
        out_spec = pl.BlockSpec((1, RWKV_HEAD, t_len, 2 * LANES), lambda hp: (0, 0, 0, hp))
        out_shape = jax.ShapeDtypeStruct((1, RWKV_HEAD, t_len, nc), F32)
    return pl.pallas_call(
        functools.partial(_seq_lanes_kernel, step_major=step_major),
        grid=(RWKV_HEADS // 2,),
        in_specs=[pl.BlockSpec((bsz, t_len, LANES), lambda hp: (0, 0, hp))],
        out_specs=out_spec,
        out_shape=out_shape,
        compiler_params=_cparams(("arbitrary",)),
        name="seq_lanes",
    )(x2d.reshape(bsz, t_len, RWKV_WIDTH))


def _seq_lanes_back_kernel(y_ref, o_ref):
    for t in range(y_ref.shape[0]):
        both = jnp.concatenate([y_ref[t, :, :LANES], y_ref[t, :, LANES:]], axis=0)
        o_ref[:, t, :] = both.T


def _seq_lanes_back(y_c, bsz):
    t_len = y_c.shape[0]
    return pl.pallas_call(
        _seq_lanes_back_kernel,
        grid=(RWKV_HEADS // 2,),
        in_specs=[pl.BlockSpec((t_len, RWKV_HEAD, 2 * LANES), lambda hp: (0, 0, hp))],
        out_specs=pl.BlockSpec((bsz, t_len, LANES), lambda hp: (0, 0, hp)),
        out_shape=jax.ShapeDtypeStruct((bsz, t_len, RWKV_WIDTH), F32),
        compiler_params=_cparams(("arbitrary",)),
        name="seq_lanes_back",
    )(y_c)


def _rwkv_seq_lanes(p0, s_shift, s_rwkv, w, bsz, t_len):
    shift8 = jnp.broadcast_to(s_shift[:, None, :], (bsz, SUBLANES, RWKV_COLS)).reshape(bsz * SUBLANES, RWKV_COLS)
    r, lw, k, v, a, g = _rwkv_prep(p0, shift8, w["rwkv_mu"], w["rwkv_w0"], w["rwkv_a0"], w["rwkv_wa"],
                                   w["rwkv_g2"], bsz, t_len)
    rows = [_seq_lanes(t, bsz, t_len, False) for t in (r, lw, k, a)]
    v_c = _seq_lanes(v, bsz, t_len, True)
    s0 = s_rwkv.transpose(3, 2, 1, 0).reshape(RWKV_HEAD, RWKV_HEAD, RWKV_HEADS * bsz)
    y_c, s_c = _rwkv_scan(*rows, v_c, w["rwkv_tiles_heads"], s0, t_len)
    y = _seq_lanes_back(y_c, bsz).reshape(bsz * t_len, RWKV_WIDTH)
    return y, g, s_c.reshape(RWKV_HEAD, RWKV_HEAD, RWKV_HEADS, bsz).transpose(3, 2, 1, 0)


def _chain_tile_heads(p):
    return jnp.broadcast_to(p.reshape(RWKV_HEADS * RWKV_HEAD, 1), (RWKV_HEADS * RWKV_HEAD, LANES))


def _chain_tile_paired(p):
    pairs = RWKV_HEADS // 2
    t = p.reshape(pairs, 2, RWKV_HEAD).transpose(2, 1, 0)[:, :, None, :]
    return jnp.broadcast_to(t, (RWKV_HEAD, 2, LANES // RWKV_HEADS, pairs)).reshape(RWKV_HEAD, LANES)


def _lru_kernel(gate_ref, x_ref, cinit_ref, cw_ref, cb_ref, wri_ref, br_ref, bi_ref, lam_ref, h0_ref,
                o_ref, hfin_ref, halo, hprev, *, tt, bb):
    j = pl.program_id(1)

    @pl.when(j == 0)
    def _():
        halo[...] = cinit_ref[...]
        for i in range(bb):
            hprev[i] = jnp.broadcast_to(h0_ref[i], (SUBLANES, LRU_WIDTH))

    rows = _row_iota((tt, LRU_BLOCK))
    for i in range(bb):
        u = x_ref[i]
        ext = jnp.concatenate([halo[i], u], axis=0)
        xc = cb_ref[...] + _shifted(ext, 3, tt) * cw_ref[0:1, :]
        xc = xc + _shifted(ext, 2, tt) * cw_ref[1:2, :]
        xc = xc + _shifted(ext, 1, tt) * cw_ref[2:3, :]
        xc = xc + u * cw_ref[3:4, :]
        halo[i] = u[tt - SUBLANES:tt, :]

        for n in range(LRU_BLOCKS):
            cs = slice(n * LRU_BLOCK, (n + 1) * LRU_BLOCK)
            xb = xc[:, cs]
            ri = _dot(xb.astype(BF16), wri_ref[n])
            rg = jax.nn.sigmoid(ri[:, :LRU_BLOCK] + br_ref[:, cs])
            ig = jax.nn.sigmoid(ri[:, LRU_BLOCK:] + bi_ref[:, cs])
            log_a = -LRU_C * rg * _softplus(-lam_ref[:, cs])
            a = jnp.exp(log_a)
            b = jnp.sqrt(-jnp.tanh(log_a) * (a * a + 1.0)) * ig * xb
            d = 1
            while d < SUBLANES:
                keep = (rows & (SUBLANES - 1)) >= d
                a_sh = jnp.where(keep, pltpu.roll(a, d, axis=0), 1.0)
                b_sh = jnp.where(keep, pltpu.roll(b, d, axis=0), 0.0)
                b = b + a * b_sh
                a = a * a_sh
                d *= 2
            h_in = hprev[i, 0:1, cs]
            groups = []
            for r0 in range(0, tt, SUBLANES):
                h_grp = a[r0:r0 + SUBLANES] * h_in + b[r0:r0 + SUBLANES]
                groups.append(h_grp)
                h_in = h_grp[SUBLANES - 1:SUBLANES, :]
            h = jnp.concatenate(groups, axis=0)
            o_ref[i, :, cs] = (h * _gelu_tanh(gate_ref[i, :, cs])).astype(BF16)
            h_last = h[tt - 1:tt, :]
            hprev[i, :, cs] = jnp.broadcast_to(h_last, (SUBLANES, LRU_BLOCK))
            hfin_ref[i, :, cs] = h_last


def _lru(p1, cinit, cw, cb, wri, br, bi, lam, h0, bsz, t_len):
    tt = min(t_len, 64)
    nt = t_len // tt
    bb = _batch_block(bsz, t_len)
    p3 = p1.reshape(bsz, t_len, P1_COLS)
    rows = lambda w: (lambda b, j: (b, j, w))
    per_seq = lambda b, j: (b, 0, 0)
    const2 = lambda b, j: (0, 0)
    o, h_fin = pl.pallas_call(
        functools.partial(_lru_kernel, tt=tt, bb=bb),
        grid=(bsz // bb, nt),
        in_specs=[
            pl.BlockSpec((bb, tt, 1024), rows(0)),
            pl.BlockSpec((bb, tt, 1024), rows(1)),
            pl.BlockSpec((bb, SUBLANES, LRU_WIDTH), per_seq),
            pl.BlockSpec((SUBLANES, LRU_WIDTH), const2),
            pl.BlockSpec((1, LRU_WIDTH), const2),
            pl.BlockSpec((LRU_BLOCKS, LRU_BLOCK, 2 * LRU_BLOCK), lambda b, j: (0, 0, 0)),
            pl.BlockSpec((1, LRU_WIDTH), const2),
            pl.BlockSpec((1, LRU_WIDTH), const2),
            pl.BlockSpec((1, LRU_WIDTH), const2),
            pl.BlockSpec((bb, 1, LRU_WIDTH), per_seq),
        ],
        out_specs=[pl.BlockSpec((bb, tt, LRU_WIDTH), rows(0)),
                   pl.BlockSpec((bb, 1, LRU_WIDTH), per_seq)],
        out_shape=[jax.ShapeDtypeStruct((bsz, t_len, LRU_WIDTH), BF16),
                   jax.ShapeDtypeStruct((bsz, 1, LRU_WIDTH), F32)],
        scratch_shapes=[pltpu.VMEM((bb, SUBLANES, LRU_WIDTH), F32), pltpu.VMEM((bb, SUBLANES, LRU_WIDTH), F32)],
        compiler_params=_cparams(("parallel", "arbitrary")),
        name="lru",
    )(p3, p3, cinit, cw, cb, wri, br, bi, lam, h0)
    return o.reshape(bsz * t_len, LRU_WIDTH), h_fin


def _expand_heads(x, e_ref, terms):
    parts = []
    rest = x
    for _ in range(terms):
        part = rest.astype(BF16).astype(F32)
        parts.append(part)
        rest = rest - part
    rows = x.shape[0]
    out = _dot(jnp.concatenate(parts, axis=0), e_ref[...])
    acc = out[:rows]
    for t in range(1, terms):
        acc = acc + out[t * rows:(t + 1) * rows]
    return acc


def _ssd_kernel(z_ref, xbc_ref, dt_ref, cinit_ref, cw_ref, cb_ref, dtb_ref, alog_ref, dd_ref, nw_ref,
                ej_ref, ep_ref, s0_ref, o_ref, s_ref, halo, st_t, *, c, mm, bb):
    j = pl.program_id(1)
    hg = SSD_HEADS // SSD_GROUPS
    gw = SSD_INNER // SSD_GROUPS
    pair = 2 * SSD_HEADDIM

    @pl.when(j == 0)
    def _():
        halo[...] = cinit_ref[...]
        for i in range(bb):
            for q in range(SSD_HEADS // 2):
                two = jnp.concatenate([s0_ref[i, 2 * q], s0_ref[i, 2 * q + 1]], axis=0)
                st_t[i, :, q * pair:(q + 1) * pair] = two.T

    wj = SSD_HEADS * c
    row_t = _row_iota((c, wj))
    lane_t = lax.broadcasted_iota(jnp.int32, (c, wj), 1) & (c - 1)
    causal = lane_t <= row_t
    diag = lane_t == row_t
    bd_rows = hg * c
    bd_mask = ((_row_iota((bd_rows, gw)) >> (c.bit_length() - 1))
               == (lax.broadcasted_iota(jnp.int32, (bd_rows, gw), 1) >> (SSD_HEADDIM.bit_length() - 1)))
    neg_a = -jnp.exp(alog_ref[...])
    gs = SSD_GROUPS * SSD_STATE
    for i in range(bb):
        u = xbc_ref[i]
        ext = jnp.concatenate([halo[i], u], axis=0)
        xc = cb_ref[...] + _shifted(ext, 3, c) * cw_ref[0:1, :]
        xc = xc + _shifted(ext, 2, c) * cw_ref[1:2, :]
        xc = xc + _shifted(ext, 1, c) * cw_ref[2:3, :]
        xc = _silu(xc + u * cw_ref[3:4, :])
        halo[i] = u[c - SUBLANES:c, :]

        xs = xc[:, :SSD_INNER]
        bm = xc[:, SSD_INNER:SSD_INNER + gs].astype(mm)
        cm = xc[:, SSD_INNER + gs:].astype(mm)
        dt = _softplus(dt_ref[i] + dtb_ref[...])
        cs = _cumsum_rows(dt * neg_a, c)
        cs_p = _expand_heads(cs, ep_ref, 3)
        dt_p = _expand_heads(dt, ep_ref, 2)
        cs_j = cs_p if c == SSD_HEADDIM else _expand_heads(cs, ej_ref, 3)
        cs_row = jnp.sum(jnp.where(diag, cs_j, 0.0), axis=0, keepdims=True)
        lmat = jnp.exp(jnp.where(causal, cs_j - cs_row, -jnp.inf))
        last_p = cs_p[c - 1:c, :]
        e_cs = jnp.exp(cs_p)
        e_last = jnp.exp(last_p)
        xdt = xs * dt_p
        xw = (xdt * jnp.exp(last_p - cs_p)).astype(mm)
        xdt = xdt.astype(mm)
        y_groups = []
        for g in range(SSD_GROUPS):
            gl = slice(g * gw, (g + 1) * gw)
            ss = slice(g * SSD_STATE, (g + 1) * SSD_STATE)
            cb_x = _dot_nt(cm[:, ss], jnp.concatenate([bm[:, ss]] * hg, axis=0))
            m_x = (cb_x * lmat[:, g * bd_rows:(g + 1) * bd_rows]).astype(mm)
            blocks = jnp.where(bd_mask, jnp.concatenate([xdt[:, gl]] * hg, axis=0), jnp.zeros((), mm))
            st = st_t[i, :, gl]
            y_groups.append(_dot(m_x, blocks) + e_cs[:, gl] * _dot(cm[:, ss], st.astype(mm)))
            st_t[i, :, gl] = st * e_last[:, gl] + _dot_tn(bm[:, ss], xw[:, gl])
        y = (jnp.concatenate(y_groups, axis=1) + dd_ref[...] * xs) * _silu(z_ref[i])
        for g in range(SSD_GROUPS):
            gl = slice(g * gw, (g + 1) * gw)
            o_ref[i, :, gl] = _rms(y[:, gl], nw_ref[:, gl], 1e-5).astype(BF16)

    @pl.when(j == pl.num_programs(1) - 1)
    def _():
        for i in range(bb):
            for q in range(SSD_HEADS // 2):
                two = st_t[i, :, q * pair:(q + 1) * pair].T
                s_ref[i, 2 * q] = two[:SSD_HEADDIM]
                s_ref[i, 2 * q + 1] = two[SSD_HEADDIM:]


def _ssd(p1, cinit, cw, cb, dtb, alog, dd, nw, s0, bsz, t_len):
    c = min(SSD_CHUNK, t_len)
    nt = t_len // c
    bb = _batch_block(bsz, t_len)
    mm = BF16 if c % 16 == 0 else F32
    p3 = p1.reshape(bsz, t_len, P1_COLS)
    rows = lambda w: (lambda b, j: (b, j, w))
    const2 = lambda b, j: (0, 0)
    st_spec = pl.BlockSpec((bb, SSD_HEADS, SSD_HEADDIM, SSD_STATE), lambda b, j: (b, 0, 0, 0))
    head = jnp.arange(LANES, dtype=jnp.int32)[:, None]
    expand = lambda width: (jnp.arange(SSD_HEADS * width, dtype=jnp.int32)[None, :] // width == head).astype(F32)
    o, s_new = pl.pallas_call(
        functools.partial(_ssd_kernel, c=c, mm=mm, bb=bb),
        grid=(bsz // bb, nt),
        in_specs=[
            pl.BlockSpec((bb, c, 1024), rows(2)),
            pl.BlockSpec((bb, c, SSD_CONV_CH), rows(2)),
            pl.BlockSpec((bb, c, LANES), rows(36)),
            pl.BlockSpec((bb, SUBLANES, SSD_CONV_CH), lambda b, j: (b, 0, 0)),
            pl.BlockSpec((SUBLANES, SSD_CONV_CH), const2),
            pl.BlockSpec((1, SSD_CONV_CH), const2),
            pl.BlockSpec((1, LANES), const2),
            pl.BlockSpec((1, LANES), const2),
            pl.BlockSpec((1, SSD_INNER), const2),
            pl.BlockSpec((1, SSD_INNER), const2),
            pl.BlockSpec((LANES, SSD_HEADS * c), const2),
            pl.BlockSpec((LANES, SSD_INNER), const2),
            st_spec,
        ],
        out_specs=[pl.BlockSpec((bb, c, SSD_INNER), rows(0)), st_spec],
        out_shape=[jax.ShapeDtypeStruct((bsz, t_len, SSD_INNER), BF16),
                   jax.ShapeDtypeStruct((bsz, SSD_HEADS, SSD_HEADDIM, SSD_STATE), F32)],
        scratch_shapes=[pltpu.VMEM((bb, SUBLANES, SSD_CONV_CH), F32),
                        pltpu.VMEM((bb, SSD_STATE, SSD_INNER), F32)],
        compiler_params=_cparams(("parallel", "arbitrary")),
        name="ssd",
    )(p3, p3, p3, cinit, cw, cb, dtb, alog, dd, nw, expand(c), expand(SSD_HEADDIM), s0)
    return o.reshape(bsz * t_len, SSD_INNER), s_new


def _pad_rows_to_8(x):
    return jnp.pad(x, ((0, 0), (SUBLANES - x.shape[1], 0), (0, 0)))


def _pad_lanes(x, width):
    return jnp.pad(x, ((0, 0), (0, width - x.shape[1])))


def _trunk(x, states, w):
    s_gla, s_rwkv, s_shift, s_lru, s_lru_conv, s_ssd, s_ssd_conv = states
    bsz, t_len, d = x.shape
    n = bsz * t_len
    x2 = x.reshape(n, d)

    p0 = _in_proj(x2, w["g_mix0"], w["w_in0"], P0_TN)
    o_gla, s_gla_new = _gla(p0, s_gla, w["gla_wa2"], w["gla_ba"], w["gla_gn"], bsz, t_len)
    if bsz * RWKV_HEADS == LANES and t_len % RELAYOUT_STEPS == 0:
        y_rwkv, g, s_rwkv_new = _rwkv_paired(p0, s_shift, s_rwkv, w, bsz, t_len)
    else:
        assert bsz == LANES and t_len == SUBLANES, "rwkv layouts exist for (8, 128k) and (128, 8) groups"
        y_rwkv, g, s_rwkv_new = _rwkv_seq_lanes(p0, s_shift, s_rwkv, w, bsz, t_len)
    shift_new = p0.reshape(bsz, t_len, P0_COLS)[:, -1, 3072:3072 + RWKV_COLS]
    x2 = _out_ffn(x2, o_gla, y_rwkv, g, w["w_out0"], w["g_ffn0"], w["wg0"], w["wu0"], w["wd0"], None)

    p1 = _in_proj(x2, w["g_mix1"], w["w_in1"], P1_TN)
    lru_out, h_lru = _lru(p1, _pad_rows_to_8(s_lru_conv), w["lru_cw"], w["lru_cb"], w["lru_wri"], w["lru_br"],
                          w["lru_bi"], w["lru_lam"], s_lru.reshape(bsz, 1, LRU_WIDTH), bsz, t_len)
    y_ssd, s_ssd_new = _ssd(p1, _pad_rows_to_8(s_ssd_conv), w["ssd_cw"], w["ssd_cb"], w["ssd_dtb"], w["ssd_alog"],
                            w["ssd_dd"], w["ssd_nw"], s_ssd, bsz, t_len)
    keep = CONV_W - 1
    p1_tail = p1.reshape(bsz, t_len, P1_COLS)[:, -keep:]
    lru_conv_new = p1_tail[:, :, 1024:2048]
    ssd_conv_new = p1_tail[:, :, 3072:3072 + SSD_CONV_CH]
    y = _out_ffn(x2, lru_out, y_ssd, None, w["w_out1"], w["g_ffn1"], w["wg1"], w["wu1"], w["wd1"], w["g_final"])
    return y.reshape(bsz, t_len, d), (s_gla_new, s_rwkv_new, shift_new, h_lru.reshape(bsz, LRU_WIDTH),
                                      lru_conv_new, s_ssd_new, ssd_conv_new)


def kernel(x_prompt, x_sample, state_gla, state_rwkv, state_rwkv_shift, state_lru, state_lru_conv, state_ssd, state_ssd_conv, w_in0, gla_w_a2, gla_b_a, gla_g_norm, rwkv_mu, rwkv_w0, rwkv_w2, rwkv_a0, rwkv_a2, rwkv_g2, rwkv_k_k, rwkv_k_a, rwkv_r_k, rwkv_ln_w, rwkv_ln_b, w_out0, w_in1, lru_conv_w, lru_conv_b, lru_w_r, lru_b_r, lru_w_i, lru_b_i, lru_lambda, ssd_conv_w, ssd_conv_b, ssd_dt_bias, ssd_a_log, ssd_d, ssd_norm_w, w_out1, g_mix, g_ffn, w_ffn_gate, w_ffn_up, w_ffn_down, g_final):
    row = lambda p: p.reshape(1, -1).astype(F32)
    w0 = jnp.concatenate([w_in0[:, :2048], w_in0[:, 2064:3088], w_in0[:, 3088:],
                          _pad_lanes(w_in0[:, 2048:2064], LANES)], axis=1)
    zeros64 = jnp.zeros((64, RWKV_WIDTH), F32)
    rwkv_wa = jnp.concatenate([jnp.concatenate([rwkv_w2, zeros64], axis=1),
                               jnp.concatenate([zeros64, rwkv_a2], axis=1)], axis=0)
    w = {
        "w_in0": w0.astype(BF16),
        "w_in1": _pad_lanes(w_in1, P1_COLS).astype(BF16),
        "g_mix0": g_mix[0], "g_mix1": g_mix[1], "g_ffn0": g_ffn[0], "g_ffn1": g_ffn[1], "g_final": g_final,
        "gla_wa2": jnp.pad(gla_w_a2, ((0, LANES - GLA_GATE_RANK), (0, 0))).astype(BF16),
        "gla_ba": row(gla_b_a), "gla_gn": row(gla_g_norm),
        "rwkv_mu": row(rwkv_mu), "rwkv_w0": row(rwkv_w0), "rwkv_a0": row(rwkv_a0),
        "rwkv_wa": rwkv_wa.astype(BF16), "rwkv_g2": rwkv_g2.astype(BF16),
        "rwkv_tiles_heads": [_chain_tile_heads(p) for p in (rwkv_k_k, rwkv_k_a, rwkv_r_k, rwkv_ln_w, rwkv_ln_b)],
        "rwkv_tiles_paired": [_chain_tile_paired(p)
                              for p in (rwkv_k_k, rwkv_k_a, rwkv_r_k, rwkv_ln_w, rwkv_ln_b)],
        "w_out0": w_out0.astype(BF16), "w_out1": w_out1.astype(BF16),
        "wg0": w_ffn_gate[0].astype(BF16), "wu0": w_ffn_up[0].astype(BF16), "wd0": w_ffn_down[0].astype(BF16),
        "wg1": w_ffn_gate[1].astype(BF16), "wu1": w_ffn_up[1].astype(BF16), "wd1": w_ffn_down[1].astype(BF16),
        "lru_cw": jnp.pad(lru_conv_w, ((0, SUBLANES - CONV_W), (0, 0))), "lru_cb": row(lru_conv_b),
        "lru_wri": jnp.concatenate([lru_w_r, lru_w_i], axis=2).astype(BF16),
        "lru_br": row(lru_b_r), "lru_bi": row(lru_b_i), "lru_lam": row(lru_lambda),
        "ssd_cw": jnp.pad(ssd_conv_w, ((0, SUBLANES - CONV_W), (0, 0))), "ssd_cb": row(ssd_conv_b),
        "ssd_dtb": _pad_lanes(row(ssd_dt_bias), LANES), "ssd_alog": _pad_lanes(row(ssd_a_log), LANES),
        "ssd_dd": row(jnp.repeat(ssd_d, SSD_HEADDIM)), "ssd_nw": row(ssd_norm_w),
    }
    bp = x_prompt.shape[0]
    prompt_init = (
        jnp.zeros((bp, GLA_HEADS, GLA_DK, GLA_DV), F32),
        jnp.zeros((bp, RWKV_HEADS, RWKV_HEAD, RWKV_HEAD), F32),
        jnp.zeros((bp, RWKV_COLS), F32),
        jnp.zeros((bp, LRU_WIDTH), F32),
        jnp.zeros((bp, CONV_W - 1, LRU_WIDTH), F32),
        jnp.zeros((bp, SSD_HEADS, SSD_HEADDIM, SSD_STATE), F32),
        jnp.zeros((bp, CONV_W - 1, SSD_CONV_CH), F32),
    )
    sample_init = (state_gla, state_rwkv, state_rwkv_shift, state_lru, state_lru_conv, state_ssd, state_ssd_conv)
    y_prompt, p_states = _trunk(x_prompt, prompt_init, w)
    y_sample, s_states = _trunk(x_sample, sample_init, w)
    return (y_prompt, y_sample, *p_states, *s_states)
```

```python
import functools

import jax
import jax.numpy as jnp
from jax import lax
from jax.experimental import pallas as pl
from jax.experimental.pallas import tpu as pltpu

F32 = jnp.float32
BF16 = jnp.bfloat16

D_MODEL = 1024
NORM_EPS = 1e-6
GLA_HEADS = 4
GLA_DK = 128
GLA_DV = 256
GLA_GATE_RANK = 16
GLA_GATE_NORM = 16.0
GLA_CHUNK = 64
RWKV_HEAD = 64
RWKV_HEADS = 16
RWKV_WIDTH = 1024
RWKV_LORA = 256
RWKV_COLS = 3 * RWKV_WIDTH + RWKV_LORA
RWKV_GN_EPS = 64e-5
LRU_WIDTH = 1024
LRU_BLOCKS = 8
LRU_BLOCK = 128
LRU_C = 8.0
CONV_W = 4
SSD_INNER = 1024
SSD_HEADDIM = 64
SSD_HEADS = 16
SSD_GROUPS = 2
SSD_STATE = 128
SSD_CHUNK = 64
SSD_CONV_CH = SSD_INNER + 2 * SSD_GROUPS * SSD_STATE
D_FF = 2816

LANES = 128
SUBLANES = 8
VMEM_LIMIT_BYTES = 56 * 1024 * 1024

P0_COLS = 6528
P0_TN = 2176
P0_Q, P0_K, P0_V, P0_GATE, P0_RWKV, P0_RWKV_LOW, P0_GLA_LOW = 0, 512, 1024, 2048, 3072, 6144, 6400
P1_LRU_GATE, P1_LRU_X, P1_SSD_Z, P1_SSD_XBC, P1_SSD_DT = 0, 1024, 2048, 3072, 4608
P1_COLS = 4864
P1_TN = 2432


def _cparams(semantics):
    return pltpu.CompilerParams(dimension_semantics=semantics, vmem_limit_bytes=VMEM_LIMIT_BYTES)


def _dot(a, b):
    return jnp.dot(a, b, preferred_element_type=F32)


def _dot_nt(a, b):
    return lax.dot_general(a, b, (((1,), (1,)), ((), ())), preferred_element_type=F32)


def _dot_tn(a, b):
    return lax.dot_general(a, b, (((0,), (0,)), ((), ())), preferred_element_type=F32)


def _softplus(x):
    return jnp.maximum(x, 0.0) + jnp.log1p(jnp.exp(-jnp.abs(x)))


def _silu(x):
    return x * jax.nn.sigmoid(x)


def _gelu_tanh(x):
    return 0.5 * x * (1.0 + jnp.tanh(0.7978845608028654 * (x + 0.044715 * (x * x * x))))


def _rms(x, g, eps):
    return x * lax.rsqrt(jnp.mean(x * x, axis=-1, keepdims=True) + eps) * g


def _row_iota(shape):
    return lax.broadcasted_iota(jnp.int32, shape, 0)


def _cumsum_rows(x, seg):
    rows = _row_iota(x.shape) & (seg - 1)
    d = 1
    while d < seg:
        x = x + jnp.where(rows >= d, pltpu.roll(x, d, axis=0), 0.0)
        d *= 2
    return x


def _shifted(ext, k, rows):
    return pltpu.roll(ext, k, axis=0)[SUBLANES:SUBLANES + rows]


def _in_proj_kernel(x_ref, g_ref, w_ref, o_ref, h_scr):
    @pl.when(pl.program_id(1) == 0)
    def _():
        h_scr[...] = _rms(x_ref[...], g_ref[...], NORM_EPS).astype(BF16)

    o_ref[...] = _dot(h_scr[...], w_ref[...])


def _in_proj(x2d, g, w, tn):
    n, d = x2d.shape
    cols = w.shape[1]
    tm = min(n, 1024)
    return pl.pallas_call(
        _in_proj_kernel,
        grid=(n // tm, cols // tn),
        in_specs=[
            pl.BlockSpec((tm, d), lambda i, j: (i, 0)),
            pl.BlockSpec((1, d), lambda i, j: (0, 0)),
            pl.BlockSpec((d, tn), lambda i, j: (0, j)),
        ],
        out_specs=pl.BlockSpec((tm, tn), lambda i, j: (i, j)),
        out_shape=jax.ShapeDtypeStruct((n, cols), F32),
        scratch_shapes=[pltpu.VMEM((tm, d), BF16)],
        compiler_params=_cparams(("parallel", "arbitrary")),
        name="in_proj",
    )(x2d, g.reshape(1, d), w)


def _out_ffn_kernel(*refs, has_gate, final_norm):
    refs = list(refs)
    x_ref, oa_ref, ob_ref = refs[:3]
    pos = 3
    gate_ref = None
    if has_gate:
        gate_ref = refs[pos]
        pos += 1
    woa_ref, wob_ref, gffn_ref, wg_ref, wu_ref, wd_ref = refs[pos:pos + 6]
    pos += 6
    gfin_ref = None
    if final_norm:
        gfin_ref = refs[pos]
        pos += 1
    out_ref, h_scr = refs[pos:pos + 2]

    k = pl.program_id(1)

    @pl.when(k == 0)
    def _():
        ob = ob_ref[...]
        if has_gate:
            ob = (ob * gate_ref[...]).astype(BF16)
        x1 = x_ref[...] + (_dot(oa_ref[...], woa_ref[...]) + _dot(ob, wob_ref[...]))
        out_ref[...] = x1
        h_scr[...] = _rms(x1, gffn_ref[...], NORM_EPS).astype(BF16)

    h = h_scr[...]
    act = (_silu(_dot(h, wg_ref[...])) * _dot(h, wu_ref[...])).astype(BF16)
    out_ref[...] += _dot(act, wd_ref[...])

    if final_norm:
        @pl.when(k == pl.num_programs(1) - 1)
        def _():
            out_ref[...] = _rms(out_ref[...], gfin_ref[...], NORM_EPS)


def _out_ffn(x2d, oa, ob, gate, wo, g_ffn, wg, wu, wd, g_final):
    n, d = x2d.shape
    has_gate = gate is not None
    final_norm = g_final is not None
    tm = min(n, 512)
    tf = D_FF // 2
    row = lambda i, k: (i, 0)
    const = lambda i, k: (0, 0)
    args = [x2d, oa, ob]
    in_specs = [pl.BlockSpec((tm, d), row)] * 3
    if has_gate:
        args.append(gate)
        in_specs.append(pl.BlockSpec((tm, d), row))
    half = wo.shape[0] // 2
    args += [wo, wo, g_ffn.reshape(1, d), wg, wu, wd]
    in_specs += [
        pl.BlockSpec((half, d), const),
        pl.BlockSpec((half, d), lambda i, k: (1, 0)),
        pl.BlockSpec((1, d), const),
        pl.BlockSpec((d, tf), lambda i, k: (0, k)),
        pl.BlockSpec((d, tf), lambda i, k: (0, k)),
        pl.BlockSpec((tf, d), lambda i, k: (k, 0)),
    ]
    if final_norm:
        args.append(g_final.reshape(1, d))
        in_specs.append(pl.BlockSpec((1, d), const))
    return pl.pallas_call(
        functools.partial(_out_ffn_kernel, has_gate=has_gate, final_norm=final_norm),
        grid=(n // tm, D_FF // tf),
        in_specs=in_specs,
        out_specs=pl.BlockSpec((tm, d), row),
        out_shape=jax.ShapeDtypeStruct((n, d), F32),
        scratch_shapes=[pltpu.VMEM((tm, d), BF16)],
        compiler_params=_cparams(("parallel", "arbitrary")),
        name="out_ffn",
    )(*args)


def _gla_kernel(q_ref, k_ref, v_ref, og_ref, al_ref, wa2_ref, ba_ref, gn_ref, s0_ref,
                o_ref, sout_ref, st_scr, *, c, mm, bb):
    n = pl.program_id(1)

    @pl.when(n == 0)
    def _():
        for i in range(bb):
            for h in range(GLA_HEADS):
                st_scr[i * GLA_HEADS + h] = s0_ref[i, h].T

    mask = _row_iota((c, c)) >= lax.broadcasted_iota(jnp.int32, (c, c), 1)
    gn = gn_ref[...]
    for i in range(bb):
        la_all = _dot(al_ref[i].astype(BF16), wa2_ref[...]) + ba_ref[...]
        la_all = -_softplus(-la_all) * (1.0 / GLA_GATE_NORM)
        b_all = _cumsum_rows(la_all, c)
        for h in range(GLA_HEADS):
            ks = slice(h * GLA_DK, (h + 1) * GLA_DK)
            vs = slice(h * GLA_DV, (h + 1) * GLA_DV)
            b = b_all[:, ks]
            b_last = b[c - 1:c, :]
            q = q_ref[i, :, ks] * (GLA_DK ** -0.5)
            k = k_ref[i, :, ks]
            v = v_ref[i, :, vs].astype(mm)
            qd = (q * jnp.exp(b)).astype(mm)
            kd = (k * jnp.exp(-b)).astype(mm)
            kc = (k * jnp.exp(b_last - b)).astype(mm)
            scores = jnp.where(mask, _dot_nt(qd, kd), 0.0)
            st = st_scr[i * GLA_HEADS + h]
            o = _dot(scores.astype(mm), v) + _dot_nt(qd, st.astype(mm))
            st_scr[i * GLA_HEADS + h] = st * jnp.exp(b_last) + _dot_tn(v, kc)
            o = _rms(o, gn, 1e-5)
            o_ref[i, :, vs] = (o * _silu(og_ref[i, :, vs])).astype(BF16)

    @pl.when(n == pl.num_programs(1) - 1)
    def _():
        for i in range(bb):
            for h in range(GLA_HEADS):
                sout_ref[i, h] = st_scr[i * GLA_HEADS + h].T


def _batch_block(bsz, t_len):
    return min(bsz, 4) if t_len >= 64 else min(bsz, 8)


def _gla(p0, s0, wa2, ba, gn, bsz, t_len):
    c = min(GLA_CHUNK, t_len)
    nt = t_len // c
    bb = _batch_block(bsz, t_len)
    mm = BF16 if c % 16 == 0 else F32
    p3 = p0.reshape(bsz, t_len, P0_COLS)
    rows = lambda w: (lambda b, j: (b, j, w))
    const = lambda b, j: (0, 0)
    st_spec = pl.BlockSpec((bb, GLA_HEADS, GLA_DK, GLA_DV), lambda b, j: (b, 0, 0, 0))
    o, s_new = pl.pallas_call(
        functools.partial(_gla_kernel, c=c, mm=mm, bb=bb),
        grid=(bsz // bb, nt),
        in_specs=[
            pl.BlockSpec((bb, c, 512), rows(P0_Q // 512)),
            pl.BlockSpec((bb, c, 512), rows(P0_K // 512)),
            pl.BlockSpec((bb, c, 1024), rows(P0_V // 1024)),
            pl.BlockSpec((bb, c, 1024), rows(P0_GATE // 1024)),
            pl.BlockSpec((bb, c, LANES), rows(P0_GLA_LOW // LANES)),
            pl.BlockSpec((LANES, 512), const),
            pl.BlockSpec((1, 512), const),
            pl.BlockSpec((1, GLA_DV), const),
            st_spec,
        ],
        out_specs=[pl.BlockSpec((bb, c, 1024), rows(0)), st_spec],
        out_shape=[jax.ShapeDtypeStruct((bsz, t_len, 1024), BF16),
                   jax.ShapeDtypeStruct((bsz, GLA_HEADS, GLA_DK, GLA_DV), F32)],
        scratch_shapes=[pltpu.VMEM((bb * GLA_HEADS, GLA_DV, GLA_DK), F32)],
        compiler_params=_cparams(("parallel", "arbitrary")),
        name="gla",
    )(p3, p3, p3, p3, p3, wa2, ba, gn, s0)
    return o.reshape(bsz * t_len, 1024), s_new


def _rwkv_prep_kernel(r_ref, k_ref, v_ref, l_ref, ir_ref, ik_ref, iv_ref, il_ref,
                      mu_ref, w0_ref, a0_ref, wa_ref, g2_ref,
                      ro_ref, wo_ref, ko_ref, vo_ref, ao_ref, go_ref, carry,
                      *, tm, t_len, multi_seq):
    j = pl.program_id(1)
    cols = ((0, 1024), (1024, 2048), (2048, 3072), (3072, RWKV_COLS))

    if not multi_seq:
        @pl.when(j == 0)
        def _():
            for (lo, hi), iref in zip(cols, (ir_ref, ik_ref, iv_ref, il_ref)):
                carry[:, lo:hi] = iref[...]

    def mixed(x_ref, i_ref, lo, hi):
        x = x_ref[...]
        rolled = pltpu.roll(x, 1, axis=0)
        rows = _row_iota(x.shape)
        if multi_seq:
            prev = jnp.where((rows & (t_len - 1)) == 0, i_ref[...], rolled)
        else:
            prev = jnp.where(rows == 0, carry[SUBLANES - 1:SUBLANES, lo:hi], rolled)
            carry[:, lo:hi] = x[tm - SUBLANES:tm, :]
        return x + (prev - x) * mu_ref[:, lo:hi]

    r = mixed(r_ref, ir_ref, *cols[0])
    k = mixed(k_ref, ik_ref, *cols[1])
    v = mixed(v_ref, iv_ref, *cols[2])
    lo_rank = mixed(l_ref, il_ref, *cols[3])

    wa_in = lo_rank[:, :LANES]
    lane = lax.broadcasted_iota(jnp.int32, wa_in.shape, 1)
    wa_in = jnp.where(lane < 64, jnp.tanh(wa_in), wa_in).astype(BF16)
    wa = _dot(wa_in, wa_ref[...])
    g = _dot(jax.nn.sigmoid(lo_rank[:, LANES:]).astype(BF16), g2_ref[...])
    w = -_softplus(-(w0_ref[...] + wa[:, :RWKV_WIDTH])) - 0.5
    ro_ref[...] = r
    wo_ref[...] = -jnp.exp(w)
    ko_ref[...] = k
    vo_ref[...] = v
    ao_ref[...] = jax.nn.sigmoid(a0_ref[...] + wa[:, RWKV_WIDTH:])
    go_ref[...] = g


def _rwkv_prep(p0, shift8, mu, w0, a0, wa, g2, bsz, t_len):
    n = bsz * t_len
    multi_seq = t_len < 256
    tm = min(n, 512) if multi_seq else 256
    nt = 1 if multi_seq else t_len // tm
    nb = n // tm if multi_seq else bsz
    ti = tm if multi_seq else SUBLANES
    rows = lambda w: (lambda i, j: (i * nt + j, w))
    init = lambda w: (lambda i, j: (i, w))
    const = lambda i, j: (0, 0)
    out = jax.ShapeDtypeStruct((n, RWKV_WIDTH), F32)
    return pl.pallas_call(
        functools.partial(_rwkv_prep_kernel, tm=tm, t_len=t_len, multi_seq=multi_seq),
        grid=(nb, nt),
        in_specs=[
            pl.BlockSpec((tm, RWKV_WIDTH), rows(P0_RWKV // RWKV_WIDTH)),
            pl.BlockSpec((tm, RWKV_WIDTH), rows(P0_RWKV // RWKV_WIDTH + 1)),
            pl.BlockSpec((tm, RWKV_WIDTH), rows(P0_RWKV // RWKV_WIDTH + 2)),
            pl.BlockSpec((tm, RWKV_LORA), rows(P0_RWKV_LOW // RWKV_LORA)),
            pl.BlockSpec((ti, RWKV_WIDTH), init(0)),
            pl.BlockSpec((ti, RWKV_WIDTH), init(1)),
            pl.BlockSpec((ti, RWKV_WIDTH), init(2)),
            pl.BlockSpec((ti, RWKV_LORA), init(3 * RWKV_WIDTH // RWKV_LORA)),
            pl.BlockSpec((1, RWKV_COLS), const),
            pl.BlockSpec((1, RWKV_WIDTH), const),
            pl.BlockSpec((1, RWKV_WIDTH), const),
            pl.BlockSpec((LANES, 2 * RWKV_WIDTH), const),
            pl.BlockSpec((LANES, RWKV_WIDTH), const),
        ],
        out_specs=[pl.BlockSpec((tm, RWKV_WIDTH), rows(0))] * 6,
        out_shape=[out] * 6,
        scratch_shapes=[pltpu.VMEM((SUBLANES, RWKV_COLS), F32)],
        compiler_params=_cparams(("parallel", "arbitrary")),
        name="rwkv_prep",
    )(p0, p0, p0, p0, shift8, shift8, shift8, shift8, mu, w0, a0, wa, g2)


def _rwkv_scan_kernel(r_ref, lw_ref, k_ref, a_ref, v_ref, kk_ref, ka_ref, rk_ref, lnw_ref, lnb_ref,
                      s0_ref, y_ref, s_ref, kb_scr, kp_scr, rs_scr, kn_scr, bonus_scr, grow_scr, *, tt):
    @pl.when(pl.program_id(1) == 0)
    def _():
        s_ref[...] = s0_ref[...]

    def tile_row(ref, j):
        return ref[pl.ds(j, 1), :]

    def bcast(x):
        return jnp.broadcast_to(x, (RWKV_HEAD, LANES))

    norm2 = jnp.zeros((tt, LANES), F32)
    for j in range(RWKV_HEAD):
        kj = k_ref[0, j] * tile_row(kk_ref, j)
        norm2 = norm2 + kj * kj
    norm = jnp.maximum(jnp.sqrt(norm2), 1e-12)
    bonus = jnp.zeros((tt, LANES), F32)
    s_kk = jnp.zeros((RWKV_HEAD, LANES), F32)
    for j in range(RWKV_HEAD):
        k = k_ref[0, j]
        a = a_ref[0, j]
        r = r_ref[0, j]
        kk = k * tile_row(kk_ref, j) / norm
        kp = k * (1.0 + (a - 1.0) * tile_row(ka_ref, j))
        log_g = _cumsum_rows(lw_ref[0, j], tt)
        grow = jnp.exp(log_g)
        shrink = jnp.exp(-log_g)
        kb_scr[j] = kk * a * shrink
        kp_scr[j] = kp * shrink
        rs_scr[j] = r * grow
        kn_scr[j] = pltpu.roll(kk, tt - 1, axis=0) * grow
        grow_scr[pl.ds(j, 1), :] = grow[tt - 1:tt, :]
        bonus = bonus + r * kp * tile_row(rk_ref, j)
        s_kk = s_kk + s_ref[j] * bcast(kk[0:1, :])
    bonus_scr[...] = bonus

    def row(ref, j, s):
        return bcast(ref[j, pl.ds(s, 1), :])

    def sweep(s, s_kk):
        sa = -s_kk
        v = v_ref[s]
        y = jnp.zeros_like(v)
        s_kk_next = jnp.zeros_like(v)
        for j in range(RWKV_HEAD):
            sn = s_ref[j] + sa * row(kb_scr, j, s) + v * row(kp_scr, j, s)
            s_ref[j] = sn
            y = y + sn * row(rs_scr, j, s)
            s_kk_next = s_kk_next + sn * row(kn_scr, j, s)
        return s_kk_next, y

    def emit(s, y):
        mu = jnp.mean(y, axis=0, keepdims=True)
        yc = y - mu
        var = jnp.mean(yc * yc, axis=0, keepdims=True)
        y_ref[s] = (yc * lax.rsqrt(var + RWKV_GN_EPS) * lnw_ref[...] + lnb_ref[...]
                    + bonus_scr[pl.ds(s, 1), :] * v_ref[s])

    def step(s, carry):
        s_kk, y_prev = carry
        emit(s - 1, y_prev)
        return sweep(s, s_kk)

    _, y_last = lax.fori_loop(1, tt, step, sweep(0, s_kk))
    emit(tt - 1, y_last)
    for j in range(RWKV_HEAD):
        s_ref[j] = s_ref[j] * bcast(tile_row(grow_scr, j))


def _rwkv_scan(r, lw, k, a, v, tiles, s0, tt):
    nb, _, _, nc = r.shape
    seq = pl.BlockSpec((1, RWKV_HEAD, tt, LANES), lambda g, t: (t, 0, 0, g))
    steps = pl.BlockSpec((tt, RWKV_HEAD, LANES), lambda g, t: (t, 0, g))
    shared = tiles[0].shape[0] == RWKV_HEAD
    tile = pl.BlockSpec((RWKV_HEAD, LANES), (lambda g, t: (0, 0)) if shared else (lambda g, t: (g, 0)))
    st = pl.BlockSpec((RWKV_HEAD, RWKV_HEAD, LANES), lambda g, t: (0, 0, g))
    return pl.pallas_call(
        functools.partial(_rwkv_scan_kernel, tt=tt),
        grid=(nc // LANES, nb),
        in_specs=[seq] * 4 + [steps] + [tile] * 5 + [st],
        out_specs=[steps, st],
        out_shape=[jax.ShapeDtypeStruct((nb * tt, RWKV_HEAD, nc), F32),
                   jax.ShapeDtypeStruct((RWKV_HEAD, RWKV_HEAD, nc), F32)],
        scratch_shapes=[pltpu.VMEM((RWKV_HEAD, tt, LANES), F32)] * 4
        + [pltpu.VMEM((tt, LANES), F32), pltpu.VMEM((RWKV_HEAD, LANES), F32)],
        compiler_params=_cparams(("parallel", "arbitrary")),
        name="rwkv_scan",
    )(r, lw, k, a, v, *tiles, s0)


RELAYOUT_PITCH = LANES + SUBLANES
RELAYOUT_STEPS = LANES
SCAN_STEPS = 32


def _stash_transposed(z_scr, b, val):
    pairs = RWKV_HEADS // 2
    for hp in range(pairs):
        q = b * pairs + hp
        z_scr[q * RELAYOUT_PITCH:q * RELAYOUT_PITCH + LANES, :] = val[:, hp * LANES:(hp + 1) * LANES].T


def _emit_chains(z_scr, o_ref, nseq, steps, step_major):
    half = nseq * (RWKV_HEADS // 2)
    for j in range(RWKV_HEAD):
        m = jnp.concatenate([z_scr[pl.ds(j, half, stride=RELAYOUT_PITCH), :],
                             z_scr[pl.ds(RWKV_HEAD + j, half, stride=RELAYOUT_PITCH), :]], axis=0)
        mt = m.T
        if step_major:
            o_ref[pl.ds(j, steps, stride=RWKV_HEAD), :] = mt
        else:
            for tq in range(steps // SCAN_STEPS):
                o_ref[tq, j] = mt[tq * SCAN_STEPS:(tq + 1) * SCAN_STEPS]


def _token_shift_mix(x_ref, mu_ref, carry, b, rows):
    x = x_ref[b]
    prev = jnp.where(rows == 0, carry[b, SUBLANES - 1:SUBLANES, :], pltpu.roll(x, 1, axis=0))
    carry[b] = x[x.shape[0] - SUBLANES:, :]
    return x + (prev - x) * mu_ref[...]


def _chain_plain_kernel(x_ref, init_ref, mu_ref, o_ref, carry, z_scr, *, step_major):
    @pl.when(pl.program_id(0) == 0)
    def _():
        carry[...] = init_ref[...]

    nseq, steps, width = x_ref.shape
    rows = _row_iota((steps, width))
    for b in range(nseq):
        _stash_transposed(z_scr, b, _token_shift_mix(x_ref, mu_ref, carry, b, rows))
    _emit_chains(z_scr, o_ref, nseq, steps, step_major)


def _chain_low_rank_kernel(x_ref, init_ref, mu_ref, w0_ref, a0_ref, wa_ref, g2_ref,
                           lw_ref, a_ref, g_ref, carry, z_lw, z_a):
    @pl.when(pl.program_id(0) == 0)
    def _():
        carry[...] = init_ref[...]

    nseq, steps, width = x_ref.shape
    rows = _row_iota((steps, width))
    for b in range(nseq):
        mixed = _token_shift_mix(x_ref, mu_ref, carry, b, rows)
        low = mixed[:, :LANES]
        lane = lax.broadcasted_iota(jnp.int32, low.shape, 1)
        low = jnp.where(lane < 64, jnp.tanh(low), low).astype(BF16)
        pre = _dot(low, wa_ref[...])
        decay_log = -jnp.exp(-_softplus(-(w0_ref[...] + pre[:, :RWKV_WIDTH])) - 0.5)
        _stash_transposed(z_lw, b, decay_log)
        _stash_transposed(z_a, b, jax.nn.sigmoid(a0_ref[...] + pre[:, RWKV_WIDTH:]))
        g_ref[b] = _dot(jax.nn.sigmoid(mixed[:, LANES:]).astype(BF16), g2_ref[...])
    _emit_chains(z_lw, lw_ref, nseq, steps, False)
    _emit_chains(z_a, a_ref, nseq, steps, False)


def _chain_specs(bsz, t_len, step_major):
    if step_major:
        return (pl.BlockSpec((RELAYOUT_STEPS * RWKV_HEAD, LANES), lambda t: (t, 0)),
                jax.ShapeDtypeStruct((t_len * RWKV_HEAD, LANES), F32))
    per = RELAYOUT_STEPS // SCAN_STEPS
    return (pl.BlockSpec((per, RWKV_HEAD, SCAN_STEPS, LANES), lambda t: (t, 0, 0, 0)),
            jax.ShapeDtypeStruct((t_len // SCAN_STEPS, RWKV_HEAD, SCAN_STEPS, LANES), F32))


def _relayout_scratch(bsz):
    return pltpu.VMEM((bsz * (RWKV_HEADS // 2) * RELAYOUT_PITCH, LANES), F32)


def _chain_plain(p3, shift3, mu, col, step_major=False):
    bsz, t_len, _ = p3.shape
    width = RWKV_WIDTH
    out_spec, out_shape = _chain_specs(bsz, t_len, step_major)
    return pl.pallas_call(
        functools.partial(_chain_plain_kernel, step_major=step_major),
        grid=(t_len // RELAYOUT_STEPS,),
        in_specs=[pl.BlockSpec((bsz, RELAYOUT_STEPS, width), lambda t: (0, t, P0_RWKV // width + col)),
                  pl.BlockSpec((bsz, SUBLANES, width), lambda t: (0, 0, col)),
                  pl.BlockSpec((1, width), lambda t: (0, 0))],
        out_specs=out_spec,
        out_shape=out_shape,
        scratch_shapes=[pltpu.VMEM((bsz, SUBLANES, width), F32), _relayout_scratch(bsz)],
        compiler_params=_cparams(("arbitrary",)),
        name="rwkv_plain",
    )(p3, shift3, mu)


def _chain_low_rank(p3, shift3, mu, w0, a0, wa, g2):
    bsz, t_len, _ = p3.shape
    width = RWKV_LORA
    const = lambda t: (0, 0)
    rows_spec, rows_shape = _chain_specs(bsz, t_len, False)
    natural = pl.BlockSpec((bsz, RELAYOUT_STEPS, RWKV_WIDTH), lambda t: (0, t, 0))
    return pl.pallas_call(
        _chain_low_rank_kernel,
        grid=(t_len // RELAYOUT_STEPS,),
        in_specs=[pl.BlockSpec((bsz, RELAYOUT_STEPS, width), lambda t: (0, t, P0_RWKV_LOW // width)),
                  pl.BlockSpec((bsz, SUBLANES, width), lambda t: (0, 0, 3 * RWKV_WIDTH // width)),
                  pl.BlockSpec((1, width), const),
                  pl.BlockSpec((1, RWKV_WIDTH), const),
                  pl.BlockSpec((1, RWKV_WIDTH), const),
                  pl.BlockSpec(wa.shape, const),
                  pl.BlockSpec(g2.shape, const)],
        out_specs=[rows_spec, rows_spec, natural],
        out_shape=[rows_shape, rows_shape, jax.ShapeDtypeStruct((bsz, t_len, RWKV_WIDTH), F32)],
        scratch_shapes=[pltpu.VMEM((bsz, SUBLANES, width), F32), _relayout_scratch(bsz), _relayout_scratch(bsz)],
        compiler_params=_cparams(("arbitrary",)),
        name="rwkv_low_rank",
    )(p3, shift3, mu, w0, a0, wa, g2)


def _from_chains_kernel(y_ref, g_ref, o_ref, z_scr):
    nseq = o_ref.shape[0]
    pairs = RWKV_HEADS // 2
    half = nseq * pairs
    for i in range(RWKV_HEAD):
        mt = y_ref[pl.ds(i, RELAYOUT_STEPS, stride=RWKV_HEAD), :].T
        z_scr[pl.ds(i, half, stride=RELAYOUT_PITCH), :] = mt[:half]
        z_scr[pl.ds(RWKV_HEAD + i, half, stride=RELAYOUT_PITCH), :] = mt[half:]
    for b in range(nseq):
        for hp in range(pairs):
            q = b * pairs + hp
            lanes = slice(hp * LANES, (hp + 1) * LANES)
            y = z_scr[q * RELAYOUT_PITCH:q * RELAYOUT_PITCH + LANES, :].T
            o_ref[b, :, lanes] = (y * g_ref[b, :, lanes]).astype(BF16)


def _from_chains(y2, g3):
    bsz, t_len, _ = g3.shape
    natural = pl.BlockSpec((bsz, RELAYOUT_STEPS, RWKV_WIDTH), lambda t: (0, t, 0))
    return pl.pallas_call(
        _from_chains_kernel,
        grid=(t_len // RELAYOUT_STEPS,),
        in_specs=[pl.BlockSpec((RELAYOUT_STEPS * RWKV_HEAD, LANES), lambda t: (t, 0)), natural],
        out_specs=natural,
        out_shape=jax.ShapeDtypeStruct((bsz, t_len, RWKV_WIDTH), BF16),
        scratch_shapes=[pltpu.VMEM((bsz * (RWKV_HEADS // 2) * RELAYOUT_PITCH, LANES), F32)],
        compiler_params=_cparams(("arbitrary",)),
        name="from_chains",
    )(y2, g3)


def _rwkv_paired(p0, s_shift, s_rwkv, w, bsz, t_len):
    pairs = RWKV_HEADS // 2
    p3 = p0.reshape(bsz, t_len, P0_COLS)
    shift3 = jnp.broadcast_to(s_shift[:, None, :], (bsz, SUBLANES, RWKV_COLS))
    mu = w["rwkv_mu"]
    mu_main = [mu[:, i * RWKV_WIDTH:(i + 1) * RWKV_WIDTH] for i in range(3)]
    mu_low = mu[:, 3 * RWKV_WIDTH:]
    r = _chain_plain(p3, shift3, mu_main[0], 0)
    k = _chain_plain(p3, shift3, mu_main[1], 1)
    v = _chain_plain(p3, shift3, mu_main[2], 2, step_major=True)
    lw, a, g = _chain_low_rank(p3, shift3, mu_low, w["rwkv_w0"], w["rwkv_a0"], w["rwkv_wa"], w["rwkv_g2"])
    s0 = s_rwkv.reshape(bsz, pairs, 2, RWKV_HEAD, RWKV_HEAD).transpose(4, 3, 2, 0, 1)
    y_c, s_c = _rwkv_scan(r, lw, k, a, v.reshape(t_len, RWKV_HEAD, LANES), w["rwkv_tiles_paired"],
                          s0.reshape(RWKV_HEAD, RWKV_HEAD, LANES), SCAN_STEPS)
    y = _from_chains(y_c.reshape(t_len * RWKV_HEAD, LANES), g)
    s_new = s_c.reshape(RWKV_HEAD, RWKV_HEAD, 2, bsz, pairs).transpose(3, 4, 2, 1, 0)
    return y.reshape(bsz * t_len, RWKV_WIDTH), None, s_new.reshape(bsz, RWKV_HEADS, RWKV_HEAD, RWKV_HEAD)


def _seq_lanes_kernel(x_ref, o_ref, *, step_major):
    for t in range(x_ref.shape[1]):
        tile_t = x_ref[:, t, :].T
        for hh in range(2):
            piece = tile_t[hh * RWKV_HEAD:(hh + 1) * RWKV_HEAD]
            lanes = slice(hh * LANES, (hh + 1) * LANES)
            if step_major:
                o_ref[t, :, lanes] = piece
            else:
                o_ref[0, :, t, lanes] = piece


def _seq_lanes(x2d, bsz, t_len, step_major):
    nc = RWKV_HEADS * bsz
    if step_major:
        out_spec = pl.BlockSpec((t_len, RWKV_HEAD, 2 * LANES), lambda hp: (0, 0, hp))
        out_shape = jax.ShapeDtypeStruct((t_len, RWKV_HEAD, nc), F32)
    else:
        out_spec = pl.BlockSpec((1, RWKV_HEAD, t_len, 2 * LANES), lambda hp: (0, 0, 0, hp))
        out_shape = jax.ShapeDtypeStruct((1, RWKV_HEAD, t_len, nc), F32)
    return pl.pallas_call(
        functools.partial(_seq_lanes_kernel, step_major=step_major),
        grid=(RWKV_HEADS // 2,),
        in_specs=[pl.BlockSpec((bsz, t_len, LANES), lambda hp: (0, 0, hp))],
        out_specs=out_spec,
        out_shape=out_shape,
        compiler_params=_cparams(("arbitrary",)),
        name="seq_lanes",
    )(x2d.reshape(bsz, t_len, RWKV_WIDTH))


def _seq_lanes_back_kernel(y_ref, o_ref):
    for t in range(y_ref.shape[0]):
        both = jnp.concatenate([y_ref[t, :, :LANES], y_ref[t, :, LANES:]], axis=0)
        o_ref[:, t, :] = both.T


def _seq_lanes_back(y_c, bsz):
    t_len = y_c.shape[0]
    return pl.pallas_call(
        _seq_lanes_back_kernel,
        grid=(RWKV_HEADS // 2,),
        in_specs=[pl.BlockSpec((t_len, RWKV_HEAD, 2 * LANES), lambda hp: (0, 0, hp))],
        out_specs=pl.BlockSpec((bsz, t_len, LANES), lambda hp: (0, 0, hp)),
        out_shape=jax.ShapeDtypeStruct((bsz, t_len, RWKV_WIDTH), F32),
        compiler_params=_cparams(("arbitrary",)),
        name="seq_lanes_back",
    )(y_c)


def _rwkv_seq_lanes(p0, s_shift, s_rwkv, w, bsz, t_len):
    shift8 = jnp.broadcast_to(s_shift[:, None, :], (bsz, SUBLANES, RWKV_COLS)).reshape(bsz * SUBLANES, RWKV_COLS)
    r, lw, k, v, a, g = _rwkv_prep(p0, shift8, w["rwkv_mu"], w["rwkv_w0"], w["rwkv_a0"], w["rwkv_wa"],
                                   w["rwkv_g2"], bsz, t_len)
    rows = [_seq_lanes(t, bsz, t_len, False) for t in (r, lw, k, a)]
    v_c = _seq_lanes(v, bsz, t_len, True)
    s0 = s_rwkv.transpose(3, 2, 1, 0).reshape(RWKV_HEAD, RWKV_HEAD, RWKV_HEADS * bsz)
    y_c, s_c = _rwkv_scan(*rows, v_c, w["rwkv_tiles_heads"], s0, t_len)
    y = _seq_lanes_back(y_c, bsz).reshape(bsz * t_len, RWKV_WIDTH)
    return y, g, s_c.reshape(RWKV_HEAD, RWKV_HEAD, RWKV_HEADS, bsz).transpose(3, 2, 1, 0)


def _chain_tile_heads(p):
    return jnp.broadcast_to(p.reshape(RWKV_HEADS * RWKV_HEAD, 1), (RWKV_HEADS * RWKV_HEAD, LANES))


def _chain_tile_paired(p):
    pairs = RWKV_HEADS // 2
    t = p.reshape(pairs, 2, RWKV_HEAD).transpose(2, 1, 0)[:, :, None, :]
    return jnp.broadcast_to(t, (RWKV_HEAD, 2, LANES // RWKV_HEADS, pairs)).reshape(RWKV_HEAD, LANES)


def _lru_kernel(gate_ref, x_ref, cinit_ref, cw_ref, cb_ref, wri_ref, br_ref, bi_ref, lam_ref, h0_ref,
                o_ref, hfin_ref, halo, hprev, *, tt, bb):
    j = pl.program_id(1)

    @pl.when(j == 0)
    def _():
        halo[...] = cinit_ref[...]
        for i in range(bb):
            hprev[i] = jnp.broadcast_to(h0_ref[i], (SUBLANES, LRU_WIDTH))

    rows = _row_iota((tt, LRU_BLOCK))
    for i in range(bb):
        u = x_ref[i]
        ext = jnp.concatenate([halo[i], u], axis=0)
        xc = cb_ref[...] + _shifted(ext, 3, tt) * cw_ref[0:1, :]
        xc = xc + _shifted(ext, 2, tt) * cw_ref[1:2, :]
        xc = xc + _shifted(ext, 1, tt) * cw_ref[2:3, :]
        xc = xc + u * cw_ref[3:4, :]
        halo[i] = u[tt - SUBLANES:tt, :]

        for n in range(LRU_BLOCKS):
            cs = slice(n * LRU_BLOCK, (n + 1) * LRU_BLOCK)
            xb = xc[:, cs]
            ri = _dot(xb.astype(BF16), wri_ref[n])
            rg = jax.nn.sigmoid(ri[:, :LRU_BLOCK] + br_ref[:, cs])
            ig = jax.nn.sigmoid(ri[:, LRU_BLOCK:] + bi_ref[:, cs])
            log_a = -LRU_C * rg * _softplus(-lam_ref[:, cs])
            a = jnp.exp(log_a)
            b = jnp.sqrt(-jnp.tanh(log_a) * (a * a + 1.0)) * ig * xb
            d = 1
            while d < SUBLANES:
                keep = (rows & (SUBLANES - 1)) >= d
                a_sh = jnp.where(keep, pltpu.roll(a, d, axis=0), 1.0)
                b_sh = jnp.where(keep, pltpu.roll(b, d, axis=0), 0.0)
                b = b + a * b_sh
                a = a * a_sh
                d *= 2
            h_in = hprev[i, 0:1, cs]
            groups = []
            for r0 in range(0, tt, SUBLANES):
                h_grp = a[r0:r0 + SUBLANES] * h_in + b[r0:r0 + SUBLANES]
                groups.append(h_grp)
                h_in = h_grp[SUBLANES - 1:SUBLANES, :]
            h = jnp.concatenate(groups, axis=0)
            o_ref[i, :, cs] = (h * _gelu_tanh(gate_ref[i, :, cs])).astype(BF16)
            h_last = h[tt - 1:tt, :]
            hprev[i, :, cs] = jnp.broadcast_to(h_last, (SUBLANES, LRU_BLOCK))
            hfin_ref[i, :, cs] = h_last


def _lru(p1, cinit, cw, cb, wri, br, bi, lam, h0, bsz, t_len):
    tt = min(t_len, 64)
    nt = t_len // tt
    bb = _batch_block(bsz, t_len)
    p3 = p1.reshape(bsz, t_len, P1_COLS)
    rows = lambda w: (lambda b, j: (b, j, w))
    per_seq = lambda b, j: (b, 0, 0)
    const2 = lambda b, j: (0, 0)
    o, h_fin = pl.pallas_call(
        functools.partial(_lru_kernel, tt=tt, bb=bb),
        grid=(bsz // bb, nt),
        in_specs=[
            pl.BlockSpec((bb, tt, LRU_WIDTH), rows(P1_LRU_GATE // LRU_WIDTH)),
            pl.BlockSpec((bb, tt, LRU_WIDTH), rows(P1_LRU_X // LRU_WIDTH)),
            pl.BlockSpec((bb, SUBLANES, LRU_WIDTH), per_seq),
            pl.BlockSpec((SUBLANES, LRU_WIDTH), const2),
            pl.BlockSpec((1, LRU_WIDTH), const2),
            pl.BlockSpec((LRU_BLOCKS, LRU_BLOCK, 2 * LRU_BLOCK), lambda b, j: (0, 0, 0)),
            pl.BlockSpec((1, LRU_WIDTH), const2),
            pl.BlockSpec((1, LRU_WIDTH), const2),
            pl.BlockSpec((1, LRU_WIDTH), const2),
            pl.BlockSpec((bb, 1, LRU_WIDTH), per_seq),
        ],
        out_specs=[pl.BlockSpec((bb, tt, LRU_WIDTH), rows(0)),
                   pl.BlockSpec((bb, 1, LRU_WIDTH), per_seq)],
        out_shape=[jax.ShapeDtypeStruct((bsz, t_len, LRU_WIDTH), BF16),
                   jax.ShapeDtypeStruct((bsz, 1, LRU_WIDTH), F32)],
        scratch_shapes=[pltpu.VMEM((bb, SUBLANES, LRU_WIDTH), F32), pltpu.VMEM((bb, SUBLANES, LRU_WIDTH), F32)],
        compiler_params=_cparams(("parallel", "arbitrary")),
        name="lru",
    )(p3, p3, cinit, cw, cb, wri, br, bi, lam, h0)
    return o.reshape(bsz * t_len, LRU_WIDTH), h_fin


def _expand_heads(x, e_ref, terms):
    parts = []
    rest = x
    for _ in range(terms):
        part = rest.astype(BF16).astype(F32)
        parts.append(part)
        rest = rest - part
    rows = x.shape[0]
    out = _dot(jnp.concatenate(parts, axis=0), e_ref[...])
    acc = out[:rows]
    for t in range(1, terms):
        acc = acc + out[t * rows:(t + 1) * rows]
    return acc


def _ssd_kernel(z_ref, xbc_ref, dt_ref, cinit_ref, cw_ref, cb_ref, dtb_ref, alog_ref, dd_ref, nw_ref,
                ej_ref, ep_ref, s0_ref, o_ref, s_ref, halo, st_t, *, c, mm, bb):
    j = pl.program_id(1)
    hg = SSD_HEADS // SSD_GROUPS
    gw = SSD_INNER // SSD_GROUPS
    pair = 2 * SSD_HEADDIM

    @pl.when(j == 0)
    def _():
        halo[...] = cinit_ref[...]
        for i in range(bb):
            for q in range(SSD_HEADS // 2):
                two = jnp.concatenate([s0_ref[i, 2 * q], s0_ref[i, 2 * q + 1]], axis=0)
                st_t[i, :, q * pair:(q + 1) * pair] = two.T

    wj = SSD_HEADS * c
    row_t = _row_iota((c, wj))
    lane_t = lax.broadcasted_iota(jnp.int32, (c, wj), 1) & (c - 1)
    causal = lane_t <= row_t
    diag = lane_t == row_t
    bd_rows = hg * c
    bd_mask = ((_row_iota((bd_rows, gw)) >> (c.bit_length() - 1))
               == (lax.broadcasted_iota(jnp.int32, (bd_rows, gw), 1) >> (SSD_HEADDIM.bit_length() - 1)))
    neg_a = -jnp.exp(alog_ref[...])
    gs = SSD_GROUPS * SSD_STATE
    for i in range(bb):
        u = xbc_ref[i]
        ext = jnp.concatenate([halo[i], u], axis=0)
        xc = cb_ref[...] + _shifted(ext, 3, c) * cw_ref[0:1, :]
        xc = xc + _shifted(ext, 2, c) * cw_ref[1:2, :]
        xc = xc + _shifted(ext, 1, c) * cw_ref[2:3, :]
        xc = _silu(xc + u * cw_ref[3:4, :])
        halo[i] = u[c - SUBLANES:c, :]

        xs = xc[:, :SSD_INNER]
        bm = xc[:, SSD_INNER:SSD_INNER + gs].astype(mm)
        cm = xc[:, SSD_INNER + gs:].astype(mm)
        dt = _softplus(dt_ref[i] + dtb_ref[...])
        cs = _cumsum_rows(dt * neg_a, c)
        cs_p = _expand_heads(cs, ep_ref, 3)
        dt_p = _expand_heads(dt, ep_ref, 2)
        cs_j = cs_p if c == SSD_HEADDIM else _expand_heads(cs, ej_ref, 3)
        cs_row = jnp.sum(jnp.where(diag, cs_j, 0.0), axis=0, keepdims=True)
        lmat = jnp.exp(jnp.where(causal, cs_j - cs_row, -jnp.inf))
        last_p = cs_p[c - 1:c, :]
        e_cs = jnp.exp(cs_p)
        e_last = jnp.exp(last_p)
        xdt = xs * dt_p
        xw = (xdt * jnp.exp(last_p - cs_p)).astype(mm)
        xdt = xdt.astype(mm)
        y_groups = []
        for g in range(SSD_GROUPS):
            gl = slice(g * gw, (g + 1) * gw)
            ss = slice(g * SSD_STATE, (g + 1) * SSD_STATE)
            cb_x = _dot_nt(cm[:, ss], jnp.concatenate([bm[:, ss]] * hg, axis=0))
            m_x = (cb_x * lmat[:, g * bd_rows:(g + 1) * bd_rows]).astype(mm)
            blocks = jnp.where(bd_mask, jnp.concatenate([xdt[:, gl]] * hg, axis=0), jnp.zeros((), mm))
            st = st_t[i, :, gl]
            y_groups.append(_dot(m_x, blocks) + e_cs[:, gl] * _dot(cm[:, ss], st.astype(mm)))
            st_t[i, :, gl] = st * e_last[:, gl] + _dot_tn(bm[:, ss], xw[:, gl])
        y = (jnp.concatenate(y_groups, axis=1) + dd_ref[...] * xs) * _silu(z_ref[i])
        for g in range(SSD_GROUPS):
            gl = slice(g * gw, (g + 1) * gw)
            o_ref[i, :, gl] = _rms(y[:, gl], nw_ref[:, gl], 1e-5).astype(BF16)

    @pl.when(j == pl.num_programs(1) - 1)
    def _():
        for i in range(bb):
            for q in range(SSD_HEADS // 2):
                two = st_t[i, :, q * pair:(q + 1) * pair].T
                s_ref[i, 2 * q] = two[:SSD_HEADDIM]
                s_ref[i, 2 * q + 1] = two[SSD_HEADDIM:]


def _ssd(p1, cinit, cw, cb, dtb, alog, dd, nw, s0, bsz, t_len):
    c = min(SSD_CHUNK, t_len)
    nt = t_len // c
    bb = _batch_block(bsz, t_len)
    mm = BF16 if c % 16 == 0 else F32
    p3 = p1.reshape(bsz, t_len, P1_COLS)
    rows = lambda w: (lambda b, j: (b, j, w))
    const2 = lambda b, j: (0, 0)
    st_spec = pl.BlockSpec((bb, SSD_HEADS, SSD_HEADDIM, SSD_STATE), lambda b, j: (b, 0, 0, 0))
    head = jnp.arange(LANES, dtype=jnp.int32)[:, None]
    expand = lambda width: (jnp.arange(SSD_HEADS * width, dtype=jnp.int32)[None, :] // width == head).astype(F32)
    o, s_new = pl.pallas_call(
        functools.partial(_ssd_kernel, c=c, mm=mm, bb=bb),
        grid=(bsz // bb, nt),
        in_specs=[
            pl.BlockSpec((bb, c, SSD_INNER), rows(P1_SSD_Z // SSD_INNER)),
            pl.BlockSpec((bb, c, SSD_CONV_CH), rows(P1_SSD_XBC // SSD_CONV_CH)),
            pl.BlockSpec((bb, c, LANES), rows(P1_SSD_DT // LANES)),
            pl.BlockSpec((bb, SUBLANES, SSD_CONV_CH), lambda b, j: (b, 0, 0)),
            pl.BlockSpec((SUBLANES, SSD_CONV_CH), const2),
            pl.BlockSpec((1, SSD_CONV_CH), const2),
            pl.BlockSpec((1, LANES), const2),
            pl.BlockSpec((1, LANES), const2),
            pl.BlockSpec((1, SSD_INNER), const2),
            pl.BlockSpec((1, SSD_INNER), const2),
            pl.BlockSpec((LANES, SSD_HEADS * c), const2),
            pl.BlockSpec((LANES, SSD_INNER), const2),
            st_spec,
        ],
        out_specs=[pl.BlockSpec((bb, c, SSD_INNER), rows(0)), st_spec],
        out_shape=[jax.ShapeDtypeStruct((bsz, t_len, SSD_INNER), BF16),
                   jax.ShapeDtypeStruct((bsz, SSD_HEADS, SSD_HEADDIM, SSD_STATE), F32)],
        scratch_shapes=[pltpu.VMEM((bb, SUBLANES, SSD_CONV_CH), F32),
                        pltpu.VMEM((bb, SSD_STATE, SSD_INNER), F32)],
        compiler_params=_cparams(("parallel", "arbitrary")),
        name="ssd",
    )(p3, p3, p3, cinit, cw, cb, dtb, alog, dd, nw, expand(c), expand(SSD_HEADDIM), s0)
    return o.reshape(bsz * t_len, SSD_INNER), s_new


def _pad_rows_to_8(x):
    return jnp.pad(x, ((0, 0), (SUBLANES - x.shape[1], 0), (0, 0)))


def _pad_lanes(x, width):
    return jnp.pad(x, ((0, 0), (0, width - x.shape[1])))


def _trunk(x, states, w):
    s_gla, s_rwkv, s_shift, s_lru, s_lru_conv, s_ssd, s_ssd_conv = states
    bsz, t_len, d = x.shape
    n = bsz * t_len
    x2 = x.reshape(n, d)

    p0 = _in_proj(x2, w["g_mix0"], w["w_in0"], P0_TN)
    o_gla, s_gla_new = _gla(p0, s_gla, w["gla_wa2"], w["gla_ba"], w["gla_gn"], bsz, t_len)
    if bsz * RWKV_HEADS == LANES and t_len % RELAYOUT_STEPS == 0:
        y_rwkv, g, s_rwkv_new = _rwkv_paired(p0, s_shift, s_rwkv, w, bsz, t_len)
    else:
        assert bsz == LANES and t_len == SUBLANES, "rwkv layouts exist for (8, 128k) and (128, 8) groups"
        y_rwkv, g, s_rwkv_new = _rwkv_seq_lanes(p0, s_shift, s_rwkv, w, bsz, t_len)
    shift_new = p0.reshape(bsz, t_len, P0_COLS)[:, -1, P0_RWKV:P0_RWKV + RWKV_COLS]
    x2 = _out_ffn(x2, o_gla, y_rwkv, g, w["w_out0"], w["g_ffn0"], w["wg0"], w["wu0"], w["wd0"], None)

    p1 = _in_proj(x2, w["g_mix1"], w["w_in1"], P1_TN)
    lru_out, h_lru = _lru(p1, _pad_rows_to_8(s_lru_conv), w["lru_cw"], w["lru_cb"], w["lru_wri"], w["lru_br"],
                          w["lru_bi"], w["lru_lam"], s_lru.reshape(bsz, 1, LRU_WIDTH), bsz, t_len)
    y_ssd, s_ssd_new = _ssd(p1, _pad_rows_to_8(s_ssd_conv), w["ssd_cw"], w["ssd_cb"], w["ssd_dtb"], w["ssd_alog"],
                            w["ssd_dd"], w["ssd_nw"], s_ssd, bsz, t_len)
    keep = CONV_W - 1
    p1_tail = p1.reshape(bsz, t_len, P1_COLS)[:, -keep:]
    lru_conv_new = p1_tail[:, :, P1_LRU_X:P1_LRU_X + LRU_WIDTH]
    ssd_conv_new = p1_tail[:, :, P1_SSD_XBC:P1_SSD_XBC + SSD_CONV_CH]
    y = _out_ffn(x2, lru_out, y_ssd, None, w["w_out1"], w["g_ffn1"], w["wg1"], w["wu1"], w["wd1"], w["g_final"])
    return y.reshape(bsz, t_len, d), (s_gla_new, s_rwkv_new, shift_new, h_lru.reshape(bsz, LRU_WIDTH),
                                      lru_conv_new, s_ssd_new, ssd_conv_new)


def kernel(x_prompt, x_sample, state_gla, state_rwkv, state_rwkv_shift, state_lru, state_lru_conv, state_ssd, state_ssd_conv, w_in0, gla_w_a2, gla_b_a, gla_g_norm, rwkv_mu, rwkv_w0, rwkv_w2, rwkv_a0, rwkv_a2, rwkv_g2, rwkv_k_k, rwkv_k_a, rwkv_r_k, rwkv_ln_w, rwkv_ln_b, w_out0, w_in1, lru_conv_w, lru_conv_b, lru_w_r, lru_b_r, lru_w_i, lru_b_i, lru_lambda, ssd_conv_w, ssd_conv_b, ssd_dt_bias, ssd_a_log, ssd_d, ssd_norm_w, w_out1, g_mix, g_ffn, w_ffn_gate, w_ffn_up, w_ffn_down, g_final):
    row = lambda p: p.reshape(1, -1).astype(F32)
    w0 = jnp.concatenate([w_in0[:, :2048], w_in0[:, 2064:3088], w_in0[:, 3088:],
                          _pad_lanes(w_in0[:, 2048:2064], LANES)], axis=1)
    zeros64 = jnp.zeros((64, RWKV_WIDTH), F32)
    rwkv_wa = jnp.concatenate([jnp.concatenate([rwkv_w2, zeros64], axis=1),
                               jnp.concatenate([zeros64, rwkv_a2], axis=1)], axis=0)
    w = {
        "w_in0": w0.astype(BF16),
        "w_in1": _pad_lanes(w_in1, P1_COLS).astype(BF16),
        "g_mix0": g_mix[0], "g_mix1": g_mix[1], "g_ffn0": g_ffn[0], "g_ffn1": g_ffn[1], "g_final": g_final,
        "gla_wa2": jnp.pad(gla_w_a2, ((0, LANES - GLA_GATE_RANK), (0, 0))).astype(BF16),
        "gla_ba": row(gla_b_a), "gla_gn": row(gla_g_norm),
        "rwkv_mu": row(rwkv_mu), "rwkv_w0": row(rwkv_w0), "rwkv_a0": row(rwkv_a0),
        "rwkv_wa": rwkv_wa.astype(BF16), "rwkv_g2": rwkv_g2.astype(BF16),
        "rwkv_tiles_heads": [_chain_tile_heads(p) for p in (rwkv_k_k, rwkv_k_a, rwkv_r_k, rwkv_ln_w, rwkv_ln_b)],
        "rwkv_tiles_paired": [_chain_tile_paired(p)
                              for p in (rwkv_k_k, rwkv_k_a, rwkv_r_k, rwkv_ln_w, rwkv_ln_b)],
        "w_out0": w_out0.astype(BF16), "w_out1": w_out1.astype(BF16),
        "wg0": w_ffn_gate[0].astype(BF16), "wu0": w_ffn_up[0].astype(BF16), "wd0": w_ffn_down[0].astype(BF16),
        "wg1": w_ffn_gate[1].astype(BF16), "wu1": w_ffn_up[1].astype(BF16), "wd1": w_ffn_down[1].astype(BF16),
        "lru_cw": jnp.pad(lru_conv_w, ((0, SUBLANES - CONV_W), (0, 0))), "lru_cb": row(lru_conv_b),
        "lru_wri": jnp.concatenate([lru_w_r, lru_w_i], axis=2).astype(BF16),
        "lru_br": row(lru_b_r), "lru_bi": row(lru_b_i), "lru_lam": row(lru_lambda),
        "ssd_cw": jnp.pad(ssd_conv_w, ((0, SUBLANES - CONV_W), (0, 0))), "ssd_cb": row(ssd_conv_b),
        "ssd_dtb": _pad_lanes(row(ssd_dt_bias), LANES), "ssd_alog": _pad_lanes(row(ssd_a_log), LANES),
        "ssd_dd": row(jnp.repeat(ssd_d, SSD_HEADDIM)), "ssd_nw": row(ssd_norm_w),
    }
    bp = x_prompt.shape[0]
    prompt_init = (
        jnp.zeros((bp, GLA_HEADS, GLA_DK, GLA_DV), F32),
        jnp.zeros((bp, RWKV_HEADS, RWKV_HEAD, RWKV_HEAD), F32),
        jnp.zeros((bp, RWKV_COLS), F32),
        jnp.zeros((bp, LRU_WIDTH), F32),
        jnp.zeros((bp, CONV_W - 1, LRU_WIDTH), F32),
        jnp.zeros((bp, SSD_HEADS, SSD_HEADDIM, SSD_STATE), F32),
        jnp.zeros((bp, CONV_W - 1, SSD_CONV_CH), F32),
    )
    sample_init = (state_gla, state_rwkv, state_rwkv_shift, state_lru, state_lru_conv, state_ssd, state_ssd_conv)
    y_prompt, p_states = _trunk(x_prompt, prompt_init, w)
    y_sample, s_states = _trunk(x_sample, sample_init, w)
    return (y_prompt, y_sample, *p_states, *s_states)
```

```python
import functools

import jax
import jax.numpy as jnp
from jax import lax
from jax.experimental import pallas as pl
from jax.experimental.pallas import tpu as pltpu

F32 = jnp.float32
BF16 = jnp.bfloat16

D_MODEL = 1024
NORM_EPS = 1e-6
GLA_HEADS = 4
GLA_DK = 128
GLA_DV = 256
GLA_GATE_RANK = 16
GLA_GATE_NORM = 16.0
GLA_CHUNK = 64
RWKV_HEAD = 64
RWKV_HEADS = 16
RWKV_WIDTH = 1024
RWKV_LORA = 256
RWKV_COLS = 3 * RWKV_WIDTH + RWKV_LORA
RWKV_GN_EPS = 64e-5
LRU_WIDTH = 1024
LRU_BLOCKS = 8
LRU_BLOCK = 128
LRU_C = 8.0
CONV_W = 4
SSD_INNER = 1024
SSD_HEADDIM = 64
SSD_HEADS = 16
SSD_GROUPS = 2
SSD_STATE = 128
SSD_CHUNK = 64
SSD_CONV_CH = SSD_INNER + 2 * SSD_GROUPS * SSD_STATE
D_FF = 2816

LANES = 128
SUBLANES = 8
VMEM_LIMIT_BYTES = 56 * 1024 * 1024

P0_COLS = 6528
P0_TN = 2176
P0_Q, P0_K, P0_V, P0_GATE, P0_RWKV, P0_RWKV_LOW, P0_GLA_LOW = 0, 512, 1024, 2048, 3072, 6144, 6400
P1_LRU_GATE, P1_LRU_X, P1_SSD_Z, P1_SSD_XBC, P1_SSD_DT = 0, 1024, 2048, 3072, 4608
P1_COLS = 4864
P1_TN = 2432


def _cparams(semantics):
    return pltpu.CompilerParams(dimension_semantics=semantics, vmem_limit_bytes=VMEM_LIMIT_BYTES)


def _dot(a, b):
    return jnp.dot(a, b, preferred_element_type=F32)


def _dot_nt(a, b):
    return lax.dot_general(a, b, (((1,), (1,)), ((), ())), preferred_element_type=F32)


def _dot_tn(a, b):
    return lax.dot_general(a, b, (((0,), (0,)), ((), ())), preferred_element_type=F32)


def _softplus(x):
    return jnp.maximum(x, 0.0) + jnp.log1p(jnp.exp(-jnp.abs(x)))


def _silu(x):
    return x * jax.nn.sigmoid(x)


def _gelu_tanh(x):
    return 0.5 * x * (1.0 + jnp.tanh(0.7978845608028654 * (x + 0.044715 * (x * x * x))))


def _rms(x, g, eps):
    return x * lax.rsqrt(jnp.mean(x * x, axis=-1, keepdims=True) + eps) * g


def _row_iota(shape):
    return lax.broadcasted_iota(jnp.int32, shape, 0)


def _cumsum_rows(x, seg):
    rows = _row_iota(x.shape) & (seg - 1)
    d = 1
    while d < seg:
        x = x + jnp.where(rows >= d, pltpu.roll(x, d, axis=0), 0.0)
        d *= 2
    return x


def _shifted(ext, k, rows):
    return pltpu.roll(ext, k, axis=0)[SUBLANES:SUBLANES + rows]


def _in_proj_kernel(x_ref, g_ref, w_ref, o_ref, h_scr):
    @pl.when(pl.program_id(1) == 0)
    def _():
        h_scr[...] = _rms(x_ref[...], g_ref[...], NORM_EPS).astype(BF16)

    o_ref[...] = _dot(h_scr[...], w_ref[...])


def _in_proj(x2d, g, w, tn):
    n, d = x2d.shape
    cols = w.shape[1]
    tm = min(n, 1024)
    return pl.pallas_call(
        _in_proj_kernel,
        grid=(n // tm, cols // tn),
        in_specs=[
            pl.BlockSpec((tm, d), lambda i, j: (i, 0)),
            pl.BlockSpec((1, d), lambda i, j: (0, 0)),
            pl.BlockSpec((d, tn), lambda i, j: (0, j)),
        ],
        out_specs=pl.BlockSpec((tm, tn), lambda i, j: (i, j)),
        out_shape=jax.ShapeDtypeStruct((n, cols), F32),
        scratch_shapes=[pltpu.VMEM((tm, d), BF16)],
        compiler_params=_cparams(("parallel", "arbitrary")),
        name="in_proj",
    )(x2d, g.reshape(1, d), w)


def _out_ffn_kernel(*refs, has_gate, final_norm):
    refs = list(refs)
    x_ref, oa_ref, ob_ref = refs[:3]
    pos = 3
    gate_ref = None
    if has_gate:
        gate_ref = refs[pos]
        pos += 1
    woa_ref, wob_ref, gffn_ref, wg_ref, wu_ref, wd_ref = refs[pos:pos + 6]
    pos += 6
    gfin_ref = None
    if final_norm:
        gfin_ref = refs[pos]
        pos += 1
    out_ref, h_scr = refs[pos:pos + 2]

    k = pl.program_id(1)

    @pl.when(k == 0)
    def _():
        ob = ob_ref[...]
        if has_gate:
            ob = (ob * gate_ref[...]).astype(BF16)
        x1 = x_ref[...] + (_dot(oa_ref[...], woa_ref[...]) + _dot(ob, wob_ref[...]))
        out_ref[...] = x1
        h_scr[...] = _rms(x1, gffn_ref[...], NORM_EPS).astype(BF16)

    h = h_scr[...]
    act = (_silu(_dot(h, wg_ref[...])) * _dot(h, wu_ref[...])).astype(BF16)
    out_ref[...] += _dot(act, wd_ref[...])

    if final_norm:
        @pl.when(k == pl.num_programs(1) - 1)
        def _():
            out_ref[...] = _rms(out_ref[...], gfin_ref[...], NORM_EPS)


def _out_ffn(x2d, oa, ob, gate, wo, g_ffn, wg, wu, wd, g_final):
    n, d = x2d.shape
    has_gate = gate is not None
    final_norm = g_final is not None
    tm = min(n, 512)
    tf = D_FF // 2
    row = lambda i, k: (i, 0)
    const = lambda i, k: (0, 0)
    args = [x2d, oa, ob]
    in_specs = [pl.BlockSpec((tm, d), row)] * 3
    if has_gate:
        args.append(gate)
        in_specs.append(pl.BlockSpec((tm, d), row))
    half = wo.shape[0] // 2
    args += [wo, wo, g_ffn.reshape(1, d), wg, wu, wd]
    in_specs += [
        pl.BlockSpec((half, d), const),
        pl.BlockSpec((half, d), lambda i, k: (1, 0)),
        pl.BlockSpec((1, d), const),
        pl.BlockSpec((d, tf), lambda i, k: (0, k)),
        pl.BlockSpec((d, tf), lambda i, k: (0, k)),
        pl.BlockSpec((tf, d), lambda i, k: (k, 0)),
    ]
    if final_norm:
        args.append(g_final.reshape(1, d))
        in_specs.append(pl.BlockSpec((1, d), const))
    return pl.pallas_call(
        functools.partial(_out_ffn_kernel, has_gate=has_gate, final_norm=final_norm),
        grid=(n // tm, D_FF // tf),
        in_specs=in_specs,
        out_specs=pl.BlockSpec((tm, d), row),
        out_shape=jax.ShapeDtypeStruct((n, d), F32),
        scratch_shapes=[pltpu.VMEM((tm, d), BF16)],
        compiler_params=_cparams(("parallel", "arbitrary")),
        name="out_ffn",
    )(*args)


def _gla_kernel(q_ref, k_ref, v_ref, og_ref, al_ref, wa2_ref, ba_ref, gn_ref, s0_ref,
                o_ref, sout_ref, st_scr, *, c, mm, bb):
    n = pl.program_id(1)

    @pl.when(n == 0)
    def _():
        for i in range(bb):
            for h in range(GLA_HEADS):
                st_scr[i * GLA_HEADS + h] = s0_ref[i, h].T

    mask = _row_iota((c, c)) >= lax.broadcasted_iota(jnp.int32, (c, c), 1)
    gn = gn_ref[...]
    for i in range(bb):
        la_all = _dot(al_ref[i].astype(BF16), wa2_ref[...]) + ba_ref[...]
        la_all = -_softplus(-la_all) * (1.0 / GLA_GATE_NORM)
        b_all = _cumsum_rows(la_all, c)
        for h in range(GLA_HEADS):
            ks = slice(h * GLA_DK, (h + 1) * GLA_DK)
            vs = slice(h * GLA_DV, (h + 1) * GLA_DV)
            b = b_all[:, ks]
            b_last = b[c - 1:c, :]
            q = q_ref[i, :, ks] * (GLA_DK ** -0.5)
            k = k_ref[i, :, ks]
            v = v_ref[i, :, vs].astype(mm)
            qd = (q * jnp.exp(b)).astype(mm)
            kd = (k * jnp.exp(-b)).astype(mm)
            kc = (k * jnp.exp(b_last - b)).astype(mm)
            scores = jnp.where(mask, _dot_nt(qd, kd), 0.0)
            st = st_scr[i * GLA_HEADS + h]
            o = _dot(scores.astype(mm), v) + _dot_nt(qd, st.astype(mm))
            st_scr[i * GLA_HEADS + h] = st * jnp.exp(b_last) + _dot_tn(v, kc)
            o = _rms(o, gn, 1e-5)
            o_ref[i, :, vs] = (o * _silu(og_ref[i, :, vs])).astype(BF16)

    @pl.when(n == pl.num_programs(1) - 1)
    def _():
        for i in range(bb):
            for h in range(GLA_HEADS):
                sout_ref[i, h] = st_scr[i * GLA_HEADS + h].T


def _batch_block(bsz, t_len):
    return min(bsz, 8)


def _gla(p0, s0, wa2, ba, gn, bsz, t_len):
    c = min(GLA_CHUNK, t_len)
    nt = t_len // c
    bb = _batch_block(bsz, t_len)
    mm = BF16 if c % 16 == 0 else F32
    p3 = p0.reshape(bsz, t_len, P0_COLS)
    rows = lambda w: (lambda b, j: (b, j, w))
    const = lambda b, j: (0, 0)
    st_spec = pl.BlockSpec((bb, GLA_HEADS, GLA_DK, GLA_DV), lambda b, j: (b, 0, 0, 0))
    o, s_new = pl.pallas_call(
        functools.partial(_gla_kernel, c=c, mm=mm, bb=bb),
        grid=(bsz // bb, nt),
        in_specs=[
            pl.BlockSpec((bb, c, 512), rows(P0_Q // 512)),
            pl.BlockSpec((bb, c, 512), rows(P0_K // 512)),
            pl.BlockSpec((bb, c, 1024), rows(P0_V // 1024)),
            pl.BlockSpec((bb, c, 1024), rows(P0_GATE // 1024)),
            pl.BlockSpec((bb, c, LANES), rows(P0_GLA_LOW // LANES)),
            pl.BlockSpec((LANES, 512), const),
            pl.BlockSpec((1, 512), const),
            pl.BlockSpec((1, GLA_DV), const),
            st_spec,
        ],
        out_specs=[pl.BlockSpec((bb, c, 1024), rows(0)), st_spec],
        out_shape=[jax.ShapeDtypeStruct((bsz, t_len, 1024), BF16),
                   jax.ShapeDtypeStruct((bsz, GLA_HEADS, GLA_DK, GLA_DV), F32)],
        scratch_shapes=[pltpu.VMEM((bb * GLA_HEADS, GLA_DV, GLA_DK), F32)],
        compiler_params=_cparams(("parallel", "arbitrary")),
        name="gla",
    )(p3, p3, p3, p3, p3, wa2, ba, gn, s0)
    return o.reshape(bsz * t_len, 1024), s_new


def _rwkv_prep_kernel(r_ref, k_ref, v_ref, l_ref, ir_ref, ik_ref, iv_ref, il_ref,
                      mu_ref, w0_ref, a0_ref, wa_ref, g2_ref,
                      ro_ref, wo_ref, ko_ref, vo_ref, ao_ref, go_ref, carry,
                      *, tm, t_len, multi_seq):
    j = pl.program_id(1)
    cols = ((0, 1024), (1024, 2048), (2048, 3072), (3072, RWKV_COLS))

    if not multi_seq:
        @pl.when(j == 0)
        def _():
            for (lo, hi), iref in zip(cols, (ir_ref, ik_ref, iv_ref, il_ref)):
                carry[:, lo:hi] = iref[...]

    def mixed(x_ref, i_ref, lo, hi):
        x = x_ref[...]
        rolled = pltpu.roll(x, 1, axis=0)
        rows = _row_iota(x.shape)
        if multi_seq:
            prev = jnp.where((rows & (t_len - 1)) == 0, i_ref[...], rolled)
        else:
            prev = jnp.where(rows == 0, carry[SUBLANES - 1:SUBLANES, lo:hi], rolled)
            carry[:, lo:hi] = x[tm - SUBLANES:tm, :]
        return x + (prev - x) * mu_ref[:, lo:hi]

    r = mixed(r_ref, ir_ref, *cols[0])
    k = mixed(k_ref, ik_ref, *cols[1])
    v = mixed(v_ref, iv_ref, *cols[2])
    lo_rank = mixed(l_ref, il_ref, *cols[3])

    wa_in = lo_rank[:, :LANES]
    lane = lax.broadcasted_iota(jnp.int32, wa_in.shape, 1)
    wa_in = jnp.where(lane < 64, jnp.tanh(wa_in), wa_in).astype(BF16)
    wa = _dot(wa_in, wa_ref[...])
    g = _dot(jax.nn.sigmoid(lo_rank[:, LANES:]).astype(BF16), g2_ref[...])
    w = -_softplus(-(w0_ref[...] + wa[:, :RWKV_WIDTH])) - 0.5
    ro_ref[...] = r
    wo_ref[...] = -jnp.exp(w)
    ko_ref[...] = k
    vo_ref[...] = v
    ao_ref[...] = jax.nn.sigmoid(a0_ref[...] + wa[:, RWKV_WIDTH:])
    go_ref[...] = g


def _rwkv_prep(p0, shift8, mu, w0, a0, wa, g2, bsz, t_len):
    n = bsz * t_len
    multi_seq = t_len < 256
    tm = min(n, 512) if multi_seq else 256
    nt = 1 if multi_seq else t_len // tm
    nb = n // tm if multi_seq else bsz
    ti = tm if multi_seq else SUBLANES
    rows = lambda w: (lambda i, j: (i * nt + j, w))
    init = lambda w: (lambda i, j: (i, w))
    const = lambda i, j: (0, 0)
    out = jax.ShapeDtypeStruct((n, RWKV_WIDTH), F32)
    return pl.pallas_call(
        functools.partial(_rwkv_prep_kernel, tm=tm, t_len=t_len, multi_seq=multi_seq),
        grid=(nb, nt),
        in_specs=[
            pl.BlockSpec((tm, RWKV_WIDTH), rows(P0_RWKV // RWKV_WIDTH)),
            pl.BlockSpec((tm, RWKV_WIDTH), rows(P0_RWKV // RWKV_WIDTH + 1)),
            pl.BlockSpec((tm, RWKV_WIDTH), rows(P0_RWKV // RWKV_WIDTH + 2)),
            pl.BlockSpec((tm, RWKV_LORA), rows(P0_RWKV_LOW // RWKV_LORA)),
            pl.BlockSpec((ti, RWKV_WIDTH), init(0)),
            pl.BlockSpec((ti, RWKV_WIDTH), init(1)),
            pl.BlockSpec((ti, RWKV_WIDTH), init(2)),
            pl.BlockSpec((ti, RWKV_LORA), init(3 * RWKV_WIDTH // RWKV_LORA)),
            pl.BlockSpec((1, RWKV_COLS), const),
            pl.BlockSpec((1, RWKV_WIDTH), const),
            pl.BlockSpec((1, RWKV_WIDTH), const),
            pl.BlockSpec((LANES, 2 * RWKV_WIDTH), const),
            pl.BlockSpec((LANES, RWKV_WIDTH), const),
        ],
        out_specs=[pl.BlockSpec((tm, RWKV_WIDTH), rows(0))] * 6,
        out_shape=[out] * 6,
        scratch_shapes=[pltpu.VMEM((SUBLANES, RWKV_COLS), F32)],
        compiler_params=_cparams(("parallel", "arbitrary")),
        name="rwkv_prep",
    )(p0, p0, p0, p0, shift8, shift8, shift8, shift8, mu, w0, a0, wa, g2)


def _rwkv_scan_kernel(r_ref, lw_ref, k_ref, a_ref, v_ref, kk_ref, ka_ref, rk_ref, lnw_ref, lnb_ref,
                      s0_ref, y_ref, s_ref, kb_scr, kp_scr, rs_scr, kn_scr, bonus_scr, grow_scr, *, tt):
    @pl.when(pl.program_id(1) == 0)
    def _():
        s_ref[...] = s0_ref[...]

    def tile_row(ref, j):
        return ref[pl.ds(j, 1), :]

    def bcast(x):
        return jnp.broadcast_to(x, (RWKV_HEAD, LANES))

    norm2 = jnp.zeros((tt, LANES), F32)
    for j in range(RWKV_HEAD):
        kj = k_ref[0, j] * tile_row(kk_ref, j)
        norm2 = norm2 + kj * kj
    norm = jnp.maximum(jnp.sqrt(norm2), 1e-12)
    bonus = jnp.zeros((tt, LANES), F32)
    s_kk = jnp.zeros((RWKV_HEAD, LANES), F32)
    for j in range(RWKV_HEAD):
        k = k_ref[0, j]
        a = a_ref[0, j]
        r = r_ref[0, j]
        kk = k * tile_row(kk_ref, j) / norm
        kp = k * (1.0 + (a - 1.0) * tile_row(ka_ref, j))
        log_g = _cumsum_rows(lw_ref[0, j], tt)
        grow = jnp.exp(log_g)
        shrink = jnp.exp(-log_g)
        kb_scr[j] = kk * a * shrink
        kp_scr[j] = kp * shrink
        rs_scr[j] = r * grow
        kn_scr[j] = pltpu.roll(kk, tt - 1, axis=0) * grow
        grow_scr[pl.ds(j, 1), :] = grow[tt - 1:tt, :]
        bonus = bonus + r * kp * tile_row(rk_ref, j)
        s_kk = s_kk + s_ref[j] * bcast(kk[0:1, :])
    bonus_scr[...] = bonus

    def row(ref, j, s):
        return bcast(ref[j, pl.ds(s, 1), :])

    def sweep(s, s_kk):
        sa = -s_kk
        v = v_ref[s]
        y = jnp.zeros_like(v)
        s_kk_next = jnp.zeros_like(v)
        for j in range(RWKV_HEAD):
            sn = s_ref[j] + sa * row(kb_scr, j, s) + v * row(kp_scr, j, s)
            s_ref[j] = sn
            y = y + sn * row(rs_scr, j, s)
            s_kk_next = s_kk_next + sn * row(kn_scr, j, s)
        return s_kk_next, y

    def emit(s, y):
        mu = jnp.mean(y, axis=0, keepdims=True)
        yc = y - mu
        var = jnp.mean(yc * yc, axis=0, keepdims=True)
        y_ref[s] = (yc * lax.rsqrt(var + RWKV_GN_EPS) * lnw_ref[...] + lnb_ref[...]
                    + bonus_scr[pl.ds(s, 1), :] * v_ref[s])

    def step(s, carry):
        s_kk, y_prev = carry
        emit(s - 1, y_prev)
        return sweep(s, s_kk)

    _, y_last = lax.fori_loop(1, tt, step, sweep(0, s_kk))
    emit(tt - 1, y_last)
    for j in range(RWKV_HEAD):
        s_ref[j] = s_ref[j] * bcast(tile_row(grow_scr, j))


def _rwkv_scan(r, lw, k, a, v, tiles, s0, tt):
    nb, _, _, nc = r.shape
    seq = pl.BlockSpec((1, RWKV_HEAD, tt, LANES), lambda g, t: (t, 0, 0, g))
    steps = pl.BlockSpec((tt, RWKV_HEAD, LANES), lambda g, t: (t, 0, g))
    shared = tiles[0].shape[0] == RWKV_HEAD
    tile = pl.BlockSpec((RWKV_HEAD, LANES), (lambda g, t: (0, 0)) if shared else (lambda g, t: (g, 0)))
    st = pl.BlockSpec((RWKV_HEAD, RWKV_HEAD, LANES), lambda g, t: (0, 0, g))
    return pl.pallas_call(
        functools.partial(_rwkv_scan_kernel, tt=tt),
        grid=(nc // LANES, nb),
        in_specs=[seq] * 4 + [steps] + [tile] * 5 + [st],
        out_specs=[steps, st],
        out_shape=[jax.ShapeDtypeStruct((nb * tt, RWKV_HEAD, nc), F32),
                   jax.ShapeDtypeStruct((RWKV_HEAD, RWKV_HEAD, nc), F32)],
        scratch_shapes=[pltpu.VMEM((RWKV_HEAD, tt, LANES), F32)] * 4
        + [pltpu.VMEM((tt, LANES), F32), pltpu.VMEM((RWKV_HEAD, LANES), F32)],
        compiler_params=_cparams(("parallel", "arbitrary")),
        name="rwkv_scan",
    )(r, lw, k, a, v, *tiles, s0)


RELAYOUT_PITCH = LANES + SUBLANES
RELAYOUT_STEPS = LANES
SCAN_STEPS = 32


def _stash_transposed(z_scr, b, val):
    pairs = RWKV_HEADS // 2
    for hp in range(pairs):
        q = b * pairs + hp
        z_scr[q * RELAYOUT_PITCH:q * RELAYOUT_PITCH + LANES, :] = val[:, hp * LANES:(hp + 1) * LANES].T


def _emit_chains(z_scr, o_ref, nseq, steps, step_major):
    half = nseq * (RWKV_HEADS // 2)
    for j in range(RWKV_HEAD):
        m = jnp.concatenate([z_scr[pl.ds(j, half, stride=RELAYOUT_PITCH), :],
                             z_scr[pl.ds(RWKV_HEAD + j, half, stride=RELAYOUT_PITCH), :]], axis=0)
        mt = m.T
        if step_major:
            o_ref[pl.ds(j, steps, stride=RWKV_HEAD), :] = mt
        else:
            for tq in range(steps // SCAN_STEPS):
                o_ref[tq, j] = mt[tq * SCAN_STEPS:(tq + 1) * SCAN_STEPS]


def _token_shift_mix(x_ref, mu_ref, carry, b, rows):
    x = x_ref[b]
    prev = jnp.where(rows == 0, carry[b, SUBLANES - 1:SUBLANES, :], pltpu.roll(x, 1, axis=0))
    carry[b] = x[x.shape[0] - SUBLANES:, :]
    return x + (prev - x) * mu_ref[...]


def _chain_plain_kernel(x_ref, init_ref, mu_ref, o_ref, carry, z_scr, *, step_major):
    @pl.when(pl.program_id(0) == 0)
    def _():
        carry[...] = init_ref[...]

    nseq, steps, width = x_ref.shape
    rows = _row_iota((steps, width))
    for b in range(nseq):
        _stash_transposed(z_scr, b, _token_shift_mix(x_ref, mu_ref, carry, b, rows))
    _emit_chains(z_scr, o_ref, nseq, steps, step_major)


def _chain_low_rank_kernel(x_ref, init_ref, mu_ref, w0_ref, a0_ref, wa_ref, g2_ref,
                           lw_ref, a_ref, g_ref, carry, z_lw, z_a):
    @pl.when(pl.program_id(0) == 0)
    def _():
        carry[...] = init_ref[...]

    nseq, steps, width = x_ref.shape
    rows = _row_iota((steps, width))
    for b in range(nseq):
        mixed = _token_shift_mix(x_ref, mu_ref, carry, b, rows)
        low = mixed[:, :LANES]
        lane = lax.broadcasted_iota(jnp.int32, low.shape, 1)
        low = jnp.where(lane < 64, jnp.tanh(low), low).astype(BF16)
        pre = _dot(low, wa_ref[...])
        decay_log = -jnp.exp(-_softplus(-(w0_ref[...] + pre[:, :RWKV_WIDTH])) - 0.5)
        _stash_transposed(z_lw, b, decay_log)
        _stash_transposed(z_a, b, jax.nn.sigmoid(a0_ref[...] + pre[:, RWKV_WIDTH:]))
        g_ref[b] = _dot(jax.nn.sigmoid(mixed[:, LANES:]).astype(BF16), g2_ref[...])
    _emit_chains(z_lw, lw_ref, nseq, steps, False)
    _emit_chains(z_a, a_ref, nseq, steps, False)


def _chain_specs(bsz, t_len, step_major):
    if step_major:
        return (pl.BlockSpec((RELAYOUT_STEPS * RWKV_HEAD, LANES), lambda t: (t, 0)),
                jax.ShapeDtypeStruct((t_len * RWKV_HEAD, LANES), F32))
    per = RELAYOUT_STEPS // SCAN_STEPS
    return (pl.BlockSpec((per, RWKV_HEAD, SCAN_STEPS, LANES), lambda t: (t, 0, 0, 0)),
            jax.ShapeDtypeStruct((t_len // SCAN_STEPS, RWKV_HEAD, SCAN_STEPS, LANES), F32))


def _relayout_scratch(bsz):
    return pltpu.VMEM((bsz * (RWKV_HEADS // 2) * RELAYOUT_PITCH, LANES), F32)


def _chain_plain(p3, shift3, mu, col, step_major=False):
    bsz, t_len, _ = p3.shape
    width = RWKV_WIDTH
    out_spec, out_shape = _chain_specs(bsz, t_len, step_major)
    return pl.pallas_call(
        functools.partial(_chain_plain_kernel, step_major=step_major),
        grid=(t_len // RELAYOUT_STEPS,),
        in_specs=[pl.BlockSpec((bsz, RELAYOUT_STEPS, width), lambda t: (0, t, P0_RWKV // width + col)),
                  pl.BlockSpec((bsz, SUBLANES, width), lambda t: (0, 0, col)),
                  pl.BlockSpec((1, width), lambda t: (0, 0))],
        out_specs=out_spec,
        out_shape=out_shape,
        scratch_shapes=[pltpu.VMEM((bsz, SUBLANES, width), F32), _relayout_scratch(bsz)],
        compiler_params=_cparams(("arbitrary",)),
        name="rwkv_plain",
    )(p3, shift3, mu)


def _chain_low_rank(p3, shift3, mu, w0, a0, wa, g2):
    bsz, t_len, _ = p3.shape
    width = RWKV_LORA
    const = lambda t: (0, 0)
    rows_spec, rows_shape = _chain_specs(bsz, t_len, False)
    natural = pl.BlockSpec((bsz, RELAYOUT_STEPS, RWKV_WIDTH), lambda t: (0, t, 0))
    return pl.pallas_call(
        _chain_low_rank_kernel,
        grid=(t_len // RELAYOUT_STEPS,),
        in_specs=[pl.BlockSpec((bsz, RELAYOUT_STEPS, width), lambda t: (0, t, P0_RWKV_LOW // width)),
                  pl.BlockSpec((bsz, SUBLANES, width), lambda t: (0, 0, 3 * RWKV_WIDTH // width)),
                  pl.BlockSpec((1, width), const),
                  pl.BlockSpec((1, RWKV_WIDTH), const),
                  pl.BlockSpec((1, RWKV_WIDTH), const),
                  pl.BlockSpec(wa.shape, const),
                  pl.BlockSpec(g2.shape, const)],
        out_specs=[rows_spec, rows_spec, natural],
        out_shape=[rows_shape, rows_shape, jax.ShapeDtypeStruct((bsz, t_len, RWKV_WIDTH), F32)],
        scratch_shapes=[pltpu.VMEM((bsz, SUBLANES, width), F32), _relayout_scratch(bsz), _relayout_scratch(bsz)],
        compiler_params=_cparams(("arbitrary",)),
        name="rwkv_low_rank",
    )(p3, shift3, mu, w0, a0, wa, g2)


def _from_chains_kernel(y_ref, g_ref, o_ref, z_scr):
    nseq = o_ref.shape[0]
    pairs = RWKV_HEADS // 2
    half = nseq * pairs
    for i in range(RWKV_HEAD):
        mt = y_ref[pl.ds(i, RELAYOUT_STEPS, stride=RWKV_HEAD), :].T
        z_scr[pl.ds(i, half, stride=RELAYOUT_PITCH), :] = mt[:half]
        z_scr[pl.ds(RWKV_HEAD + i, half, stride=RELAYOUT_PITCH), :] = mt[half:]
    for b in range(nseq):
        for hp in range(pairs):
            q = b * pairs + hp
            lanes = slice(hp * LANES, (hp + 1) * LANES)
            y = z_scr[q * RELAYOUT_PITCH:q * RELAYOUT_PITCH + LANES, :].T
            o_ref[b, :, lanes] = (y * g_ref[b, :, lanes]).astype(BF16)


def _from_chains(y2, g3):
    bsz, t_len, _ = g3.shape
    natural = pl.BlockSpec((bsz, RELAYOUT_STEPS, RWKV_WIDTH), lambda t: (0, t, 0))
    return pl.pallas_call(
        _from_chains_kernel,
        grid=(t_len // RELAYOUT_STEPS,),
        in_specs=[pl.BlockSpec((RELAYOUT_STEPS * RWKV_HEAD, LANES), lambda t: (t, 0)), natural],
        out_specs=natural,
        out_shape=jax.ShapeDtypeStruct((bsz, t_len, RWKV_WIDTH), BF16),
        scratch_shapes=[pltpu.VMEM((bsz * (RWKV_HEADS // 2) * RELAYOUT_PITCH, LANES), F32)],
        compiler_params=_cparams(("arbitrary",)),
        name="from_chains",
    )(y2, g3)


def _rwkv_paired(p0, s_shift, s_rwkv, w, bsz, t_len):
    pairs = RWKV_HEADS // 2
    p3 = p0.reshape(bsz, t_len, P0_COLS)
    shift3 = jnp.broadcast_to(s_shift[:, None, :], (bsz, SUBLANES, RWKV_COLS))
    mu = w["rwkv_mu"]
    mu_main = [mu[:, i * RWKV_WIDTH:(i + 1) * RWKV_WIDTH] for i in range(3)]
    mu_low = mu[:, 3 * RWKV_WIDTH:]
    r = _chain_plain(p3, shift3, mu_main[0], 0)
    k = _chain_plain(p3, shift3, mu_main[1], 1)
    v = _chain_plain(p3, shift3, mu_main[2], 2, step_major=True)
    lw, a, g = _chain_low_rank(p3, shift3, mu_low, w["rwkv_w0"], w["rwkv_a0"], w["rwkv_wa"], w["rwkv_g2"])
    s0 = s_rwkv.reshape(bsz, pairs, 2, RWKV_HEAD, RWKV_HEAD).transpose(4, 3, 2, 0, 1)
    y_c, s_c = _rwkv_scan(r, lw, k, a, v.reshape(t_len, RWKV_HEAD, LANES), w["rwkv_tiles_paired"],
                          s0.reshape(RWKV_HEAD, RWKV_HEAD, LANES), SCAN_STEPS)
    y = _from_chains(y_c.reshape(t_len * RWKV_HEAD, LANES), g)
    s_new = s_c.reshape(RWKV_HEAD, RWKV_HEAD, 2, bsz, pairs).transpose(3, 4, 2, 1, 0)
    return y.reshape(bsz * t_len, RWKV_WIDTH), None, s_new.reshape(bsz, RWKV_HEADS, RWKV_HEAD, RWKV_HEAD)


def _seq_lanes_kernel(x_ref, o_ref, *, step_major):
    for t in range(x_ref.shape[1]):
        tile_t = x_ref[:, t, :].T
        for hh in range(2):
            piece = tile_t[hh * RWKV_HEAD:(hh + 1) * RWKV_HEAD]
            lanes = slice(hh * LANES, (hh + 1) * LANES)
            if step_major:
                o_ref[t, :, lanes] = piece
            else:
                o_ref[0, :, t, lanes] = piece


def _seq_lanes(x2d, bsz, t_len, step_major):
    nc = RWKV_HEADS * bsz
    if step_major:
        out_spec = pl.BlockSpec((t_len, RWKV_HEAD, 2 * LANES), lambda hp: (0, 0, hp))
        out_shape = jax.ShapeDtypeStruct((t_len, RWKV_HEAD, nc), F32)
    else:
        out_spec = pl.BlockSpec((1, RWKV_HEAD, t_len, 2 * LANES), lambda hp: (0, 0, 0, hp))
        out_shape = jax.ShapeDtypeStruct((1, RWKV_HEAD, t_len, nc), F32)
    return pl.pallas_call(
        functools.partial(_seq_lanes_kernel, step_major=step_major),
        grid=(RWKV_HEADS // 2,),
        in_specs=[pl.BlockSpec((bsz, t_len, LANES), lambda hp: (0, 0, hp))],
        out_specs=out_spec,
        out_shape=out_shape,
        compiler_params=_cparams(("arbitrary",)),
        name="seq_lanes",
    )(x2d.reshape(bsz, t_len, RWKV_WIDTH))


def _seq_lanes_back_kernel(y_ref, o_ref):
    for t in range(y_ref.shape[0]):
        both = jnp.concatenate([y_ref[t, :, :LANES], y_ref[t, :, LANES:]], axis=0)
        o_ref[:, t, :] = both.T


def _seq_lanes_back(y_c, bsz):
    t_len = y_c.shape[0]
    return pl.pallas_call(
        _seq_lanes_back_kernel,
        grid=(RWKV_HEADS // 2,),
        in_specs=[pl.BlockSpec((t_len, RWKV_HEAD, 2 * LANES), lambda hp: (0, 0, hp))],
        out_specs=pl.BlockSpec((bsz, t_len, LANES), lambda hp: (0, 0, hp)),
        out_shape=jax.ShapeDtypeStruct((bsz, t_len, RWKV_WIDTH), F32),
        compiler_params=_cparams(("arbitrary",)),
        name="seq_lanes_back",
    )(y_c)


def _rwkv_seq_lanes(p0, s_shift, s_rwkv, w, bsz, t_len):
    shift8 = jnp.broadcast_to(s_shift[:, None, :], (bsz, SUBLANES, RWKV_COLS)).reshape(bsz * SUBLANES, RWKV_COLS)
    r, lw, k, v, a, g = _rwkv_prep(p0, shift8, w["rwkv_mu"], w["rwkv_w0"], w["rwkv_a0"], w["rwkv_wa"],
                                   w["rwkv_g2"], bsz, t_len)
    rows = [_seq_lanes(t, bsz, t_len, False) for t in (r, lw, k, a)]
    v_c = _seq_lanes(v, bsz, t_len, True)
    s0 = s_rwkv.transpose(3, 2, 1, 0).reshape(RWKV_HEAD, RWKV_HEAD, RWKV_HEADS * bsz)
    y_c, s_c = _rwkv_scan(*rows, v_c, w["rwkv_tiles_heads"], s0, t_len)
    y = _seq_lanes_back(y_c, bsz).reshape(bsz * t_len, RWKV_WIDTH)
    return y, g, s_c.reshape(RWKV_HEAD, RWKV_HEAD, RWKV_HEADS, bsz).transpose(3, 2, 1, 0)


def _chain_tile_heads(p):
    return jnp.broadcast_to(p.reshape(RWKV_HEADS * RWKV_HEAD, 1), (RWKV_HEADS * RWKV_HEAD, LANES))


def _chain_tile_paired(p):
    pairs = RWKV_HEADS // 2
    t = p.reshape(pairs, 2, RWKV_HEAD).transpose(2, 1, 0)[:, :, None, :]
    return jnp.broadcast_to(t, (RWKV_HEAD, 2, LANES // RWKV_HEADS, pairs)).reshape(RWKV_HEAD, LANES)


def _lru_kernel(gate_ref, x_ref, cinit_ref, cw_ref, cb_ref, wri_ref, br_ref, bi_ref, lam_ref, h0_ref,
                o_ref, hfin_ref, halo, hprev, *, tt, bb):
    j = pl.program_id(1)

    @pl.when(j == 0)
    def _():
        halo[...] = cinit_ref[...]
        for i in range(bb):
            hprev[i] = jnp.broadcast_to(h0_ref[i], (SUBLANES, LRU_WIDTH))

    rows = _row_iota((tt, LRU_BLOCK))
    for i in range(bb):
        u = x_ref[i]
        ext = jnp.concatenate([halo[i], u], axis=0)
        xc = cb_ref[...] + _shifted(ext, 3, tt) * cw_ref[0:1, :]
        xc = xc + _shifted(ext, 2, tt) * cw_ref[1:2, :]
        xc = xc + _shifted(ext, 1, tt) * cw_ref[2:3, :]
        xc = xc + u * cw_ref[3:4, :]
        halo[i] = u[tt - SUBLANES:tt, :]

        for n in range(LRU_BLOCKS):
            cs = slice(n * LRU_BLOCK, (n + 1) * LRU_BLOCK)
            xb = xc[:, cs]
            ri = _dot(xb.astype(BF16), wri_ref[n])
            rg = jax.nn.sigmoid(ri[:, :LRU_BLOCK] + br_ref[:, cs])
            ig = jax.nn.sigmoid(ri[:, LRU_BLOCK:] + bi_ref[:, cs])
            log_a = -LRU_C * rg * _softplus(-lam_ref[:, cs])
            a = jnp.exp(log_a)
            b = jnp.sqrt(-jnp.tanh(log_a) * (a * a + 1.0)) * ig * xb
            d = 1
            while d < SUBLANES:
                keep = (rows & (SUBLANES - 1)) >= d
                a_sh = jnp.where(keep, pltpu.roll(a, d, axis=0), 1.0)
                b_sh = jnp.where(keep, pltpu.roll(b, d, axis=0), 0.0)
                b = b + a * b_sh
                a = a * a_sh
                d *= 2
            h_in = hprev[i, 0:1, cs]
            groups = []
            for r0 in range(0, tt, SUBLANES):
                h_grp = a[r0:r0 + SUBLANES] * h_in + b[r0:r0 + SUBLANES]
                groups.append(h_grp)
                h_in = h_grp[SUBLANES - 1:SUBLANES, :]
            h = jnp.concatenate(groups, axis=0)
            o_ref[i, :, cs] = (h * _gelu_tanh(gate_ref[i, :, cs])).astype(BF16)
            h_last = h[tt - 1:tt, :]
            hprev[i, :, cs] = jnp.broadcast_to(h_last, (SUBLANES, LRU_BLOCK))
            hfin_ref[i, :, cs] = h_last


def _lru(p1, cinit, cw, cb, wri, br, bi, lam, h0, bsz, t_len):
    tt = min(t_len, 64)
    nt = t_len // tt
    bb = _batch_block(bsz, t_len)
    p3 = p1.reshape(bsz, t_len, P1_COLS)
    rows = lambda w: (lambda b, j: (b, j, w))
    per_seq = lambda b, j: (b, 0, 0)
    const2 = lambda b, j: (0, 0)
    o, h_fin = pl.pallas_call(
        functools.partial(_lru_kernel, tt=tt, bb=bb),
        grid=(bsz // bb, nt),
        in_specs=[
            pl.BlockSpec((bb, tt, LRU_WIDTH), rows(P1_LRU_GATE // LRU_WIDTH)),
            pl.BlockSpec((bb, tt, LRU_WIDTH), rows(P1_LRU_X // LRU_WIDTH)),
            pl.BlockSpec((bb, SUBLANES, LRU_WIDTH), per_seq),
            pl.BlockSpec((SUBLANES, LRU_WIDTH), const2),
            pl.BlockSpec((1, LRU_WIDTH), const2),
            pl.BlockSpec((LRU_BLOCKS, LRU_BLOCK, 2 * LRU_BLOCK), lambda b, j: (0, 0, 0)),
            pl.BlockSpec((1, LRU_WIDTH), const2),
            pl.BlockSpec((1, LRU_WIDTH), const2),
            pl.BlockSpec((1, LRU_WIDTH), const2),
            pl.BlockSpec((bb, 1, LRU_WIDTH), per_seq),
        ],
        out_specs=[pl.BlockSpec((bb, tt, LRU_WIDTH), rows(0)),
                   pl.BlockSpec((bb, 1, LRU_WIDTH), per_seq)],
        out_shape=[jax.ShapeDtypeStruct((bsz, t_len, LRU_WIDTH), BF16),
                   jax.ShapeDtypeStruct((bsz, 1, LRU_WIDTH), F32)],
        scratch_shapes=[pltpu.VMEM((bb, SUBLANES, LRU_WIDTH), F32), pltpu.VMEM((bb, SUBLANES, LRU_WIDTH), F32)],
        compiler_params=_cparams(("parallel", "arbitrary")),
        name="lru",
    )(p3, p3, cinit, cw, cb, wri, br, bi, lam, h0)
    return o.reshape(bsz * t_len, LRU_WIDTH), h_fin


def _expand_heads(x, e_ref, terms):
    parts = []
    rest = x
    for _ in range(terms):
        part = rest.astype(BF16).astype(F32)
        parts.append(part)
        rest = rest - part
    rows = x.shape[0]
    out = _dot(jnp.concatenate(parts, axis=0), e_ref[...])
    acc = out[:rows]
    for t in range(1, terms):
        acc = acc + out[t * rows:(t + 1) * rows]
    return acc


def _ssd_kernel(z_ref, xbc_ref, dt_ref, cinit_ref, cw_ref, cb_ref, dtb_ref, alog_ref, dd_ref, nw_ref,
                ej_ref, ep_ref, s0_ref, o_ref, s_ref, halo, st_t, *, c, mm, bb):
    j = pl.program_id(1)
    hg = SSD_HEADS // SSD_GROUPS
    gw = SSD_INNER // SSD_GROUPS
    pair = 2 * SSD_HEADDIM

    @pl.when(j == 0)
    def _():
        halo[...] = cinit_ref[...]
        for i in range(bb):
            for q in range(SSD_HEADS // 2):
                two = jnp.concatenate([s0_ref[i, 2 * q], s0_ref[i, 2 * q + 1]], axis=0)
                st_t[i, :, q * pair:(q + 1) * pair] = two.T

    wj = SSD_HEADS * c
    row_t = _row_iota((c, wj))
    lane_t = lax.broadcasted_iota(jnp.int32, (c, wj), 1) & (c - 1)
    causal = lane_t <= row_t
    diag = lane_t == row_t
    bd_rows = hg * c
    bd_mask = ((_row_iota((bd_rows, gw)) >> (c.bit_length() - 1))
               == (lax.broadcasted_iota(jnp.int32, (bd_rows, gw), 1) >> (SSD_HEADDIM.bit_length() - 1)))
    neg_a = -jnp.exp(alog_ref[...])
    gs = SSD_GROUPS * SSD_STATE
    for i in range(bb):
        u = xbc_ref[i]
        ext = jnp.concatenate([halo[i], u], axis=0)
        xc = cb_ref[...] + _shifted(ext, 3, c) * cw_ref[0:1, :]
        xc = xc + _shifted(ext, 2, c) * cw_ref[1:2, :]
        xc = xc + _shifted(ext, 1, c) * cw_ref[2:3, :]
        xc = _silu(xc + u * cw_ref[3:4, :])
        halo[i] = u[c - SUBLANES:c, :]

        xs = xc[:, :SSD_INNER]
        bm = xc[:, SSD_INNER:SSD_INNER + gs].astype(mm)
        cm = xc[:, SSD_INNER + gs:].astype(mm)
        dt = _softplus(dt_ref[i] + dtb_ref[...])
        cs = _cumsum_rows(dt * neg_a, c)
        cs_p = _expand_heads(cs, ep_ref, 3)
        dt_p = _expand_heads(dt, ep_ref, 2)
        cs_j = cs_p if c == SSD_HEADDIM else _expand_heads(cs, ej_ref, 3)
        cs_row = jnp.sum(jnp.where(diag, cs_j, 0.0), axis=0, keepdims=True)
        lmat = jnp.exp(jnp.where(causal, cs_j - cs_row, -jnp.inf))
        last_p = cs_p[c - 1:c, :]
        e_cs = jnp.exp(cs_p)
        e_last = jnp.exp(last_p)
        xdt = xs * dt_p
        xw = (xdt * jnp.exp(last_p - cs_p)).astype(mm)
        xdt = xdt.astype(mm)
        y_groups = []
        for g in range(SSD_GROUPS):
            gl = slice(g * gw, (g + 1) * gw)
            ss = slice(g * SSD_STATE, (g + 1) * SSD_STATE)
            cb_x = _dot_nt(cm[:, ss], jnp.concatenate([bm[:, ss]] * hg, axis=0))
            m_x = (cb_x * lmat[:, g * bd_rows:(g + 1) * bd_rows]).astype(mm)
            blocks = jnp.where(bd_mask, jnp.concatenate([xdt[:, gl]] * hg, axis=0), jnp.zeros((), mm))
            st = st_t[i, :, gl]
            y_groups.append(_dot(m_x, blocks) + e_cs[:, gl] * _dot(cm[:, ss], st.astype(mm)))
            st_t[i, :, gl] = st * e_last[:, gl] + _dot_tn(bm[:, ss], xw[:, gl])
        y = (jnp.concatenate(y_groups, axis=1) + dd_ref[...] * xs) * _silu(z_ref[i])
        for g in range(SSD_GROUPS):
            gl = slice(g * gw, (g + 1) * gw)
            o_ref[i, :, gl] = _rms(y[:, gl], nw_ref[:, gl], 1e-5).astype(BF16)

    @pl.when(j == pl.num_programs(1) - 1)
    def _():
        for i in range(bb):
            for q in range(SSD_HEADS // 2):
                two = st_t[i, :, q * pair:(q + 1) * pair].T
                s_ref[i, 2 * q] = two[:SSD_HEADDIM]
                s_ref[i, 2 * q + 1] = two[SSD_HEADDIM:]


def _ssd(p1, cinit, cw, cb, dtb, alog, dd, nw, s0, bsz, t_len):
    c = min(SSD_CHUNK, t_len)
    nt = t_len // c
    bb = _batch_block(bsz, t_len)
    mm = BF16 if c % 16 == 0 else F32
    p3 = p1.reshape(bsz, t_len, P1_COLS)
    rows = lambda w: (lambda b, j: (b, j, w))
    const2 = lambda b, j: (0, 0)
    st_spec = pl.BlockSpec((bb, SSD_HEADS, SSD_HEADDIM, SSD_STATE), lambda b, j: (b, 0, 0, 0))
    head = jnp.arange(LANES, dtype=jnp.int32)[:, None]
    expand = lambda width: (jnp.arange(SSD_HEADS * width, dtype=jnp.int32)[None, :] // width == head).astype(F32)
    o, s_new = pl.pallas_call(
        functools.partial(_ssd_kernel, c=c, mm=mm, bb=bb),
        grid=(bsz // bb, nt),
        in_specs=[
            pl.BlockSpec((bb, c, SSD_INNER), rows(P1_SSD_Z // SSD_INNER)),
            pl.BlockSpec((bb, c, SSD_CONV_CH), rows(P1_SSD_XBC // SSD_CONV_CH)),
            pl.BlockSpec((bb, c, LANES), rows(P1_SSD_DT // LANES)),
            pl.BlockSpec((bb, SUBLANES, SSD_CONV_CH), lambda b, j: (b, 0, 0)),
            pl.BlockSpec((SUBLANES, SSD_CONV_CH), const2),
            pl.BlockSpec((1, SSD_CONV_CH), const2),
            pl.BlockSpec((1, LANES), const2),
            pl.BlockSpec((1, LANES), const2),
            pl.BlockSpec((1, SSD_INNER), const2),
            pl.BlockSpec((1, SSD_INNER), const2),
            pl.BlockSpec((LANES, SSD_HEADS * c), const2),
            pl.BlockSpec((LANES, SSD_INNER), const2),
            st_spec,
        ],
        out_specs=[pl.BlockSpec((bb, c, SSD_INNER), rows(0)), st_spec],
        out_shape=[jax.ShapeDtypeStruct((bsz, t_len, SSD_INNER), BF16),
                   jax.ShapeDtypeStruct((bsz, SSD_HEADS, SSD_HEADDIM, SSD_STATE), F32)],
        scratch_shapes=[pltpu.VMEM((bb, SUBLANES, SSD_CONV_CH), F32),
                        pltpu.VMEM((bb, SSD_STATE, SSD_INNER), F32)],
        compiler_params=_cparams(("parallel", "arbitrary")),
        name="ssd",
    )(p3, p3, p3, cinit, cw, cb, dtb, alog, dd, nw, expand(c), expand(SSD_HEADDIM), s0)
    return o.reshape(bsz * t_len, SSD_INNER), s_new


def _pad_rows_to_8(x):
    return jnp.pad(x, ((0, 0), (SUBLANES - x.shape[1], 0), (0, 0)))


def _pad_lanes(x, width):
    return jnp.pad(x, ((0, 0), (0, width - x.shape[1])))


def _trunk(x, states, w):
    s_gla, s_rwkv, s_shift, s_lru, s_lru_conv, s_ssd, s_ssd_conv = states
    bsz, t_len, d = x.shape
    n = bsz * t_len
    x2 = x.reshape(n, d)

    p0 = _in_proj(x2, w["g_mix0"], w["w_in0"], P0_TN)
    o_gla, s_gla_new = _gla(p0, s_gla, w["gla_wa2"], w["gla_ba"], w["gla_gn"], bsz, t_len)
    if bsz * RWKV_HEADS == LANES and t_len % RELAYOUT_STEPS == 0:
        y_rwkv, g, s_rwkv_new = _rwkv_paired(p0, s_shift, s_rwkv, w, bsz, t_len)
    else:
        assert bsz == LANES and t_len == SUBLANES, "rwkv layouts exist for (8, 128k) and (128, 8) groups"
        y_rwkv, g, s_rwkv_new = _rwkv_seq_lanes(p0, s_shift, s_rwkv, w, bsz, t_len)
    shift_new = p0.reshape(bsz, t_len, P0_COLS)[:, -1, P0_RWKV:P0_RWKV + RWKV_COLS]
    x2 = _out_ffn(x2, o_gla, y_rwkv, g, w["w_out0"], w["g_ffn0"], w["wg0"], w["wu0"], w["wd0"], None)

    p1 = _in_proj(x2, w["g_mix1"], w["w_in1"], P1_TN)
    lru_out, h_lru = _lru(p1, _pad_rows_to_8(s_lru_conv), w["lru_cw"], w["lru_cb"], w["lru_wri"], w["lru_br"],
                          w["lru_bi"], w["lru_lam"], s_lru.reshape(bsz, 1, LRU_WIDTH), bsz, t_len)
    y_ssd, s_ssd_new = _ssd(p1, _pad_rows_to_8(s_ssd_conv), w["ssd_cw"], w["ssd_cb"], w["ssd_dtb"], w["ssd_alog"],
                            w["ssd_dd"], w["ssd_nw"], s_ssd, bsz, t_len)
    keep = CONV_W - 1
    p1_tail = p1.reshape(bsz, t_len, P1_COLS)[:, -keep:]
    lru_conv_new = p1_tail[:, :, P1_LRU_X:P1_LRU_X + LRU_WIDTH]
    ssd_conv_new = p1_tail[:, :, P1_SSD_XBC:P1_SSD_XBC + SSD_CONV_CH]
    y = _out_ffn(x2, lru_out, y_ssd, None, w["w_out1"], w["g_ffn1"], w["wg1"], w["wu1"], w["wd1"], w["g_final"])
    return y.reshape(bsz, t_len, d), (s_gla_new, s_rwkv_new, shift_new, h_lru.reshape(bsz, LRU_WIDTH),
                                      lru_conv_new, s_ssd_new, ssd_conv_new)


def kernel(x_prompt, x_sample, state_gla, state_rwkv, state_rwkv_shift, state_lru, state_lru_conv, state_ssd, state_ssd_conv, w_in0, gla_w_a2, gla_b_a, gla_g_norm, rwkv_mu, rwkv_w0, rwkv_w2, rwkv_a0, rwkv_a2, rwkv_g2, rwkv_k_k, rwkv_k_a, rwkv_r_k, rwkv_ln_w, rwkv_ln_b, w_out0, w_in1, lru_conv_w, lru_conv_b, lru_w_r, lru_b_r, lru_w_i, lru_b_i, lru_lambda, ssd_conv_w, ssd_conv_b, ssd_dt_bias, ssd_a_log, ssd_d, ssd_norm_w, w_out1, g_mix, g_ffn, w_ffn_gate, w_ffn_up, w_ffn_down, g_final):
    row = lambda p: p.reshape(1, -1).astype(F32)
    w0 = jnp.concatenate([w_in0[:, :2048], w_in0[:, 2064:3088], w_in0[:, 3088:],
                          _pad_lanes(w_in0[:, 2048:2064], LANES)], axis=1)
    zeros64 = jnp.zeros((64, RWKV_WIDTH), F32)
    rwkv_wa = jnp.concatenate([jnp.concatenate([rwkv_w2, zeros64], axis=1),
                               jnp.concatenate([zeros64, rwkv_a2], axis=1)], axis=0)
    w = {
        "w_in0": w0.astype(BF16),
        "w_in1": _pad_lanes(w_in1, P1_COLS).astype(BF16),
        "g_mix0": g_mix[0], "g_mix1": g_mix[1], "g_ffn0": g_ffn[0], "g_ffn1": g_ffn[1], "g_final": g_final,
        "gla_wa2": jnp.pad(gla_w_a2, ((0, LANES - GLA_GATE_RANK), (0, 0))).astype(BF16),
        "gla_ba": row(gla_b_a), "gla_gn": row(gla_g_norm),
        "rwkv_mu": row(rwkv_mu), "rwkv_w0": row(rwkv_w0), "rwkv_a0": row(rwkv_a0),
        "rwkv_wa": rwkv_wa.astype(BF16), "rwkv_g2": rwkv_g2.astype(BF16),
        "rwkv_tiles_heads": [_chain_tile_heads(p) for p in (rwkv_k_k, rwkv_k_a, rwkv_r_k, rwkv_ln_w, rwkv_ln_b)],
        "rwkv_tiles_paired": [_chain_tile_paired(p)
                              for p in (rwkv_k_k, rwkv_k_a, rwkv_r_k, rwkv_ln_w, rwkv_ln_b)],
        "w_out0": w_out0.astype(BF16), "w_out1": w_out1.astype(BF16),
        "wg0": w_ffn_gate[0].astype(BF16), "wu0": w_ffn_up[0].astype(BF16), "wd0": w_ffn_down[0].astype(BF16),
        "wg1": w_ffn_gate[1].astype(BF16), "wu1": w_ffn_up[1].astype(BF16), "wd1": w_ffn_down[1].astype(BF16),
        "lru_cw": jnp.pad(lru_conv_w, ((0, SUBLANES - CONV_W), (0, 0))), "lru_cb": row(lru_conv_b),
        "lru_wri": jnp.concatenate([lru_w_r, lru_w_i], axis=2).astype(BF16),
        "lru_br": row(lru_b_r), "lru_bi": row(lru_b_i), "lru_lam": row(lru_lambda),
        "ssd_cw": jnp.pad(ssd_conv_w, ((0, SUBLANES - CONV_W), (0, 0))), "ssd_cb": row(ssd_conv_b),
        "ssd_dtb": _pad_lanes(row(ssd_dt_bias), LANES), "ssd_alog": _pad_lanes(row(ssd_a_log), LANES),
        "ssd_dd": row(jnp.repeat(ssd_d, SSD_HEADDIM)), "ssd_nw": row(ssd_norm_w),
    }
    bp = x_prompt.shape[0]
    prompt_init = (
        jnp.zeros((bp, GLA_HEADS, GLA_DK, GLA_DV), F32),
        jnp.zeros((bp, RWKV_HEADS, RWKV_HEAD, RWKV_HEAD), F32),
        jnp.zeros((bp, RWKV_COLS), F32),
        jnp.zeros((bp, LRU_WIDTH), F32),
        jnp.zeros((bp, CONV_W - 1, LRU_WIDTH), F32),
        jnp.zeros((bp, SSD_HEADS, SSD_HEADDIM, SSD_STATE), F32),
        jnp.zeros((bp, CONV_W - 1, SSD_CONV_CH), F32),
    )
    sample_init = (state_gla, state_rwkv, state_rwkv_shift, state_lru, state_lru_conv, state_ssd, state_ssd_conv)
    y_prompt, p_states = _trunk(x_prompt, prompt_init, w)
    y_sample, s_states = _trunk(x_sample, sample_init, w)
    return (y_prompt, y_sample, *p_states, *s_states)
```

```python
import functools

import jax
import jax.numpy as jnp
from jax import lax
from jax.experimental import pallas as pl
from jax.experimental.pallas import tpu as pltpu

F32 = jnp.float32
BF16 = jnp.bfloat16

D_MODEL = 1024
NORM_EPS = 1e-6
GLA_HEADS = 4
GLA_DK = 128
GLA_DV = 256
GLA_GATE_RANK = 16
GLA_GATE_NORM = 16.0
GLA_CHUNK = 64
RWKV_HEAD = 64
RWKV_HEADS = 16
RWKV_WIDTH = 1024
RWKV_LORA = 256
RWKV_COLS = 3 * RWKV_WIDTH + RWKV_LORA
RWKV_GN_EPS = 64e-5
LRU_WIDTH = 1024
LRU_BLOCKS = 8
LRU_BLOCK = 128
LRU_C = 8.0
CONV_W = 4
SSD_INNER = 1024
SSD_HEADDIM = 64
SSD_HEADS = 16
SSD_GROUPS = 2
SSD_STATE = 128
SSD_CHUNK = 64
SSD_CONV_CH = SSD_INNER + 2 * SSD_GROUPS * SSD_STATE
D_FF = 2816

LANES = 128
SUBLANES = 8
VMEM_LIMIT_BYTES = 56 * 1024 * 1024

P0_COLS = 6528
P0_TN = 2176
P0_Q, P0_K, P0_V, P0_GATE, P0_RWKV, P0_RWKV_LOW, P0_GLA_LOW = 0, 512, 1024, 2048, 3072, 6144, 6400
P1_LRU_GATE, P1_LRU_X, P1_SSD_Z, P1_SSD_XBC, P1_SSD_DT = 0, 1024, 2048, 3072, 4608
P1_COLS = 4864
P1_TN = 2432


def _cparams(semantics):
    return pltpu.CompilerParams(dimension_semantics=semantics, vmem_limit_bytes=VMEM_LIMIT_BYTES)


def _dot(a, b):
    return jnp.dot(a, b, preferred_element_type=F32)


def _dot_nt(a, b):
    return lax.dot_general(a, b, (((1,), (1,)), ((), ())), preferred_element_type=F32)


def _dot_tn(a, b):
    return lax.dot_general(a, b, (((0,), (0,)), ((), ())), preferred_element_type=F32)


def _softplus(x):
    return jnp.maximum(x, 0.0) + jnp.log1p(jnp.exp(-jnp.abs(x)))


def _silu(x):
    return x * jax.nn.sigmoid(x)


def _gelu_tanh(x):
    return 0.5 * x * (1.0 + jnp.tanh(0.7978845608028654 * (x + 0.044715 * (x * x * x))))


def _rms(x, g, eps):
    return x * lax.rsqrt(jnp.mean(x * x, axis=-1, keepdims=True) + eps) * g


def _row_iota(shape):
    return lax.broadcasted_iota(jnp.int32, shape, 0)


def _cumsum_rows(x, seg):
    rows = _row_iota(x.shape) & (seg - 1)
    d = 1
    while d < seg:
        x = x + jnp.where(rows >= d, pltpu.roll(x, d, axis=0), 0.0)
        d *= 2
    return x


def _shifted(ext, k, rows):
    return pltpu.roll(ext, k, axis=0)[SUBLANES:SUBLANES + rows]


def _in_proj_kernel(x_ref, g_ref, w_ref, o_ref, h_scr):
    @pl.when(pl.program_id(1) == 0)
    def _():
        h_scr[...] = _rms(x_ref[...], g_ref[...], NORM_EPS).astype(BF16)

    o_ref[...] = _dot(h_scr[...], w_ref[...])


def _in_proj(x2d, g, w, tn):
    n, d = x2d.shape
    cols = w.shape[1]
    tm = min(n, 1024)
    return pl.pallas_call(
        _in_proj_kernel,
        grid=(n // tm, cols // tn),
        in_specs=[
            pl.BlockSpec((tm, d), lambda i, j: (i, 0)),
            pl.BlockSpec((1, d), lambda i, j: (0, 0)),
            pl.BlockSpec((d, tn), lambda i, j: (0, j)),
        ],
        out_specs=pl.BlockSpec((tm, tn), lambda i, j: (i, j)),
        out_shape=jax.ShapeDtypeStruct((n, cols), F32),
        scratch_shapes=[pltpu.VMEM((tm, d), BF16)],
        compiler_params=_cparams(("parallel", "arbitrary")),
        name="in_proj",
    )(x2d, g.reshape(1, d), w)


def _out_ffn_kernel(*refs, has_gate, final_norm):
    refs = list(refs)
    x_ref, oa_ref, ob_ref = refs[:3]
    pos = 3
    gate_ref = None
    if has_gate:
        gate_ref = refs[pos]
        pos += 1
    woa_ref, wob_ref, gffn_ref, wg_ref, wu_ref, wd_ref = refs[pos:pos + 6]
    pos += 6
    gfin_ref = None
    if final_norm:
        gfin_ref = refs[pos]
        pos += 1
    out_ref, h_scr = refs[pos:pos + 2]

    k = pl.program_id(1)

    @pl.when(k == 0)
    def _():
        ob = ob_ref[...]
        if has_gate:
            ob = (ob * gate_ref[...]).astype(BF16)
        x1 = x_ref[...] + (_dot(oa_ref[...], woa_ref[...]) + _dot(ob, wob_ref[...]))
        out_ref[...] = x1
        h_scr[...] = _rms(x1, gffn_ref[...], NORM_EPS).astype(BF16)

    h = h_scr[...]
    act = (_silu(_dot(h, wg_ref[...])) * _dot(h, wu_ref[...])).astype(BF16)
    out_ref[...] += _dot(act, wd_ref[...])

    if final_norm:
        @pl.when(k == pl.num_programs(1) - 1)
        def _():
            out_ref[...] = _rms(out_ref[...], gfin_ref[...], NORM_EPS)


def _out_ffn(x2d, oa, ob, gate, wo, g_ffn, wg, wu, wd, g_final):
    n, d = x2d.shape
    has_gate = gate is not None
    final_norm = g_final is not None
    tm = min(n, 512)
    tf = D_FF // 2
    row = lambda i, k: (i, 0)
    const = lambda i, k: (0, 0)
    args = [x2d, oa, ob]
    in_specs = [pl.BlockSpec((tm, d), row)] * 3
    if has_gate:
        args.append(gate)
        in_specs.append(pl.BlockSpec((tm, d), row))
    half = wo.shape[0] // 2
    args += [wo, wo, g_ffn.reshape(1, d), wg, wu, wd]
    in_specs += [
        pl.BlockSpec((half, d), const),
        pl.BlockSpec((half, d), lambda i, k: (1, 0)),
        pl.BlockSpec((1, d), const),
        pl.BlockSpec((d, tf), lambda i, k: (0, k)),
        pl.BlockSpec((d, tf), lambda i, k: (0, k)),
        pl.BlockSpec((tf, d), lambda i, k: (k, 0)),
    ]
    if final_norm:
        args.append(g_final.reshape(1, d))
        in_specs.append(pl.BlockSpec((1, d), const))
    return pl.pallas_call(
        functools.partial(_out_ffn_kernel, has_gate=has_gate, final_norm=final_norm),
        grid=(n // tm, D_FF // tf),
        in_specs=in_specs,
        out_specs=pl.BlockSpec((tm, d), row),
        out_shape=jax.ShapeDtypeStruct((n, d), F32),
        scratch_shapes=[pltpu.VMEM((tm, d), BF16)],
        compiler_params=_cparams(("parallel", "arbitrary")),
        name="out_ffn",
    )(*args)


def _gla_kernel(q_ref, k_ref, v_ref, og_ref, al_ref, wa2_ref, ba_ref, gn_ref, s0_ref,
                o_ref, sout_ref, st_scr, *, c, mm, bb):
    n = pl.program_id(1)

    @pl.when(n == 0)
    def _():
        for i in range(bb):
            for h in range(GLA_HEADS):
                st_scr[i * GLA_HEADS + h] = s0_ref[i, h].T

    mask = _row_iota((c, c)) >= lax.broadcasted_iota(jnp.int32, (c, c), 1)
    gn = gn_ref[...]
    for i in range(bb):
        la_all = _dot(al_ref[i].astype(BF16), wa2_ref[...]) + ba_ref[...]
        la_all = -_softplus(-la_all) * (1.0 / GLA_GATE_NORM)
        b_all = _cumsum_rows(la_all, c)
        for h in range(GLA_HEADS):
            ks = slice(h * GLA_DK, (h + 1) * GLA_DK)
            vs = slice(h * GLA_DV, (h + 1) * GLA_DV)
            b = b_all[:, ks]
            b_last = b[c - 1:c, :]
            q = q_ref[i, :, ks] * (GLA_DK ** -0.5)
            k = k_ref[i, :, ks]
            v = v_ref[i, :, vs].astype(mm)
            qd = (q * jnp.exp(b)).astype(mm)
            kd = (k * jnp.exp(-b)).astype(mm)
            kc = (k * jnp.exp(b_last - b)).astype(mm)
            scores = jnp.where(mask, _dot_nt(qd, kd), 0.0)
            st = st_scr[i * GLA_HEADS + h]
            o = _dot(scores.astype(mm), v) + _dot_nt(qd, st.astype(mm))
            st_scr[i * GLA_HEADS + h] = st * jnp.exp(b_last) + _dot_tn(v, kc)
            o = _rms(o, gn, 1e-5)
            o_ref[i, :, vs] = (o * _silu(og_ref[i, :, vs])).astype(BF16)

    @pl.when(n == pl.num_programs(1) - 1)
    def _():
        for i in range(bb):
            for h in range(GLA_HEADS):
                sout_ref[i, h] = st_scr[i * GLA_HEADS + h].T


def _batch_block(bsz, t_len):
    return min(bsz, 8)


def _gla(p0, s0, wa2, ba, gn, bsz, t_len):
    c = min(GLA_CHUNK, t_len)
    nt = t_len // c
    bb = _batch_block(bsz, t_len)
    mm = BF16 if c % 16 == 0 else F32
    p3 = p0.reshape(bsz, t_len, P0_COLS)
    rows = lambda w: (lambda b, j: (b, j, w))
    const = lambda b, j: (0, 0)
    st_spec = pl.BlockSpec((bb, GLA_HEADS, GLA_DK, GLA_DV), lambda b, j: (b, 0, 0, 0))
    o, s_new = pl.pallas_call(
        functools.partial(_gla_kernel, c=c, mm=mm, bb=bb),
        grid=(bsz // bb, nt),
        in_specs=[
            pl.BlockSpec((bb, c, 512), rows(P0_Q // 512)),
            pl.BlockSpec((bb, c, 512), rows(P0_K // 512)),
            pl.BlockSpec((bb, c, 1024), rows(P0_V // 1024)),
            pl.BlockSpec((bb, c, 1024), rows(P0_GATE // 1024)),
            pl.BlockSpec((bb, c, LANES), rows(P0_GLA_LOW // LANES)),
            pl.BlockSpec((LANES, 512), const),
            pl.BlockSpec((1, 512), const),
            pl.BlockSpec((1, GLA_DV), const),
            st_spec,
        ],
        out_specs=[pl.BlockSpec((bb, c, 1024), rows(0)), st_spec],
        out_shape=[jax.ShapeDtypeStruct((bsz, t_len, 1024), BF16),
                   jax.ShapeDtypeStruct((bsz, GLA_HEADS, GLA_DK, GLA_DV), F32)],
        scratch_shapes=[pltpu.VMEM((bb * GLA_HEADS, GLA_DV, GLA_DK), F32)],
        compiler_params=_cparams(("parallel", "arbitrary")),
        name="gla",
    )(p3, p3, p3, p3, p3, wa2, ba, gn, s0)
    return o.reshape(bsz * t_len, 1024), s_new


def _rwkv_prep_kernel(r_ref, k_ref, v_ref, l_ref, ir_ref, ik_ref, iv_ref, il_ref,
                      mu_ref, w0_ref, a0_ref, wa_ref, g2_ref,
                      ro_ref, wo_ref, ko_ref, vo_ref, ao_ref, go_ref, carry,
                      *, tm, t_len, multi_seq):
    j = pl.program_id(1)
    cols = ((0, 1024), (1024, 2048), (2048, 3072), (3072, RWKV_COLS))

    if not multi_seq:
        @pl.when(j == 0)
        def _():
            for (lo, hi), iref in zip(cols, (ir_ref, ik_ref, iv_ref, il_ref)):
                carry[:, lo:hi] = iref[...]

    def mixed(x_ref, i_ref, lo, hi):
        x = x_ref[...]
        rolled = pltpu.roll(x, 1, axis=0)
        rows = _row_iota(x.shape)
        if multi_seq:
            prev = jnp.where((rows & (t_len - 1)) == 0, i_ref[...], rolled)
        else:
            prev = jnp.where(rows == 0, carry[SUBLANES - 1:SUBLANES, lo:hi], rolled)
            carry[:, lo:hi] = x[tm - SUBLANES:tm, :]
        return x + (prev - x) * mu_ref[:, lo:hi]

    r = mixed(r_ref, ir_ref, *cols[0])
    k = mixed(k_ref, ik_ref, *cols[1])
    v = mixed(v_ref, iv_ref, *cols[2])
    lo_rank = mixed(l_ref, il_ref, *cols[3])

    wa_in = lo_rank[:, :LANES]
    lane = lax.broadcasted_iota(jnp.int32, wa_in.shape, 1)
    wa_in = jnp.where(lane < 64, jnp.tanh(wa_in), wa_in).astype(BF16)
    wa = _dot(wa_in, wa_ref[...])
    g = _dot(jax.nn.sigmoid(lo_rank[:, LANES:]).astype(BF16), g2_ref[...])
    w = -_softplus(-(w0_ref[...] + wa[:, :RWKV_WIDTH])) - 0.5
    ro_ref[...] = r
    wo_ref[...] = -jnp.exp(w)
    ko_ref[...] = k
    vo_ref[...] = v
    ao_ref[...] = jax.nn.sigmoid(a0_ref[...] + wa[:, RWKV_WIDTH:])
    go_ref[...] = g


def _rwkv_prep(p0, shift8, mu, w0, a0, wa, g2, bsz, t_len):
    n = bsz * t_len
    multi_seq = t_len < 256
    tm = min(n, 512) if multi_seq else 256
    nt = 1 if multi_seq else t_len // tm
    nb = n // tm if multi_seq else bsz
    ti = tm if multi_seq else SUBLANES
    rows = lambda w: (lambda i, j: (i * nt + j, w))
    init = lambda w: (lambda i, j: (i, w))
    const = lambda i, j: (0, 0)
    out = jax.ShapeDtypeStruct((n, RWKV_WIDTH), F32)
    return pl.pallas_call(
        functools.partial(_rwkv_prep_kernel, tm=tm, t_len=t_len, multi_seq=multi_seq),
        grid=(nb, nt),
        in_specs=[
            pl.BlockSpec((tm, RWKV_WIDTH), rows(P0_RWKV // RWKV_WIDTH)),
            pl.BlockSpec((tm, RWKV_WIDTH), rows(P0_RWKV // RWKV_WIDTH + 1)),
            pl.BlockSpec((tm, RWKV_WIDTH), rows(P0_RWKV // RWKV_WIDTH + 2)),
            pl.BlockSpec((tm, RWKV_LORA), rows(P0_RWKV_LOW // RWKV_LORA)),
            pl.BlockSpec((ti, RWKV_WIDTH), init(0)),
            pl.BlockSpec((ti, RWKV_WIDTH), init(1)),
            pl.BlockSpec((ti, RWKV_WIDTH), init(2)),
            pl.BlockSpec((ti, RWKV_LORA), init(3 * RWKV_WIDTH // RWKV_LORA)),
            pl.BlockSpec((1, RWKV_COLS), const),
            pl.BlockSpec((1, RWKV_WIDTH), const),
            pl.BlockSpec((1, RWKV_WIDTH), const),
            pl.BlockSpec((LANES, 2 * RWKV_WIDTH), const),
            pl.BlockSpec((LANES, RWKV_WIDTH), const),
        ],
        out_specs=[pl.BlockSpec((tm, RWKV_WIDTH), rows(0))] * 6,
        out_shape=[out] * 6,
        scratch_shapes=[pltpu.VMEM((SUBLANES, RWKV_COLS), F32)],
        compiler_params=_cparams(("parallel", "arbitrary")),
        name="rwkv_prep",
    )(p0, p0, p0, p0, shift8, shift8, shift8, shift8, mu, w0, a0, wa, g2)


def _rwkv_scan_kernel(r_ref, lw_ref, k_ref, a_ref, v_ref, kk_ref, ka_ref, rk_ref, lnw_ref, lnb_ref,
                      s0_ref, y_ref, s_ref, kb_scr, kp_scr, rs_scr, kn_scr, bonus_scr, grow_scr, *, tt):
    @pl.when(pl.program_id(1) == 0)
    def _():
        s_ref[...] = s0_ref[...]

    def tile_row(ref, j):
        return ref[pl.ds(j, 1), :]

    def bcast(x):
        return jnp.broadcast_to(x, (RWKV_HEAD, LANES))

    norm2 = jnp.zeros((tt, LANES), F32)
    for j in range(RWKV_HEAD):
        kj = k_ref[0, j] * tile_row(kk_ref, j)
        norm2 = norm2 + kj * kj
    norm = jnp.maximum(jnp.sqrt(norm2), 1e-12)
    bonus = jnp.zeros((tt, LANES), F32)
    s_kk = jnp.zeros((RWKV_HEAD, LANES), F32)
    for j in range(RWKV_HEAD):
        k = k_ref[0, j]
        a = a_ref[0, j]
        r = r_ref[0, j]
        kk = k * tile_row(kk_ref, j) / norm
        kp = k * (1.0 + (a - 1.0) * tile_row(ka_ref, j))
        log_g = _cumsum_rows(lw_ref[0, j], tt)
        grow = jnp.exp(log_g)
        shrink = jnp.exp(-log_g)
        kb_scr[j] = kk * a * shrink
        kp_scr[j] = kp * shrink
        rs_scr[j] = r * grow
        kn_scr[j] = pltpu.roll(kk, tt - 1, axis=0) * grow
        grow_scr[pl.ds(j, 1), :] = grow[tt - 1:tt, :]
        bonus = bonus + r * kp * tile_row(rk_ref, j)
        s_kk = s_kk + s_ref[j] * bcast(kk[0:1, :])
    bonus_scr[...] = bonus

    def row(ref, j, s):
        return bcast(ref[j, pl.ds(s, 1), :])

    def sweep(s, s_kk):
        sa = -s_kk
        v = v_ref[s]
        y = jnp.zeros_like(v)
        s_kk_next = jnp.zeros_like(v)
        for j in range(RWKV_HEAD):
            sn = s_ref[j] + sa * row(kb_scr, j, s) + v * row(kp_scr, j, s)
            s_ref[j] = sn
            y = y + sn * row(rs_scr, j, s)
            s_kk_next = s_kk_next + sn * row(kn_scr, j, s)
        return s_kk_next, y

    def emit(s, y):
        mu = jnp.mean(y, axis=0, keepdims=True)
        yc = y - mu
        var = jnp.mean(yc * yc, axis=0, keepdims=True)
        y_ref[s] = (yc * lax.rsqrt(var + RWKV_GN_EPS) * lnw_ref[...] + lnb_ref[...]
                    + bonus_scr[pl.ds(s, 1), :] * v_ref[s])

    def step(s, carry):
        s_kk, y_prev = carry
        emit(s - 1, y_prev)
        return sweep(s, s_kk)

    _, y_last = lax.fori_loop(1, tt, step, sweep(0, s_kk))
    emit(tt - 1, y_last)
    for j in range(RWKV_HEAD):
        s_ref[j] = s_ref[j] * bcast(tile_row(grow_scr, j))


def _rwkv_scan(r, lw, k, a, v, tiles, s0, tt):
    nb, _, _, nc = r.shape
    seq = pl.BlockSpec((1, RWKV_HEAD, tt, LANES), lambda g, t: (t, 0, 0, g))
    steps = pl.BlockSpec((tt, RWKV_HEAD, LANES), lambda g, t: (t, 0, g))
    shared = tiles[0].shape[0] == RWKV_HEAD
    tile = pl.BlockSpec((RWKV_HEAD, LANES), (lambda g, t: (0, 0)) if shared else (lambda g, t: (g, 0)))
    st = pl.BlockSpec((RWKV_HEAD, RWKV_HEAD, LANES), lambda g, t: (0, 0, g))
    return pl.pallas_call(
        functools.partial(_rwkv_scan_kernel, tt=tt),
        grid=(nc // LANES, nb),
        in_specs=[seq] * 4 + [steps] + [tile] * 5 + [st],
        out_specs=[steps, st],
        out_shape=[jax.ShapeDtypeStruct((nb * tt, RWKV_HEAD, nc), F32),
                   jax.ShapeDtypeStruct((RWKV_HEAD, RWKV_HEAD, nc), F32)],
        scratch_shapes=[pltpu.VMEM((RWKV_HEAD, tt, LANES), F32)] * 4
        + [pltpu.VMEM((tt, LANES), F32), pltpu.VMEM((RWKV_HEAD, LANES), F32)],
        compiler_params=_cparams(("parallel", "arbitrary")),
        name="rwkv_scan",
    )(r, lw, k, a, v, *tiles, s0)


RELAYOUT_PITCH = LANES + SUBLANES
RELAYOUT_STEPS = LANES
SCAN_STEPS = 64


def _stash_transposed(z_scr, b, val):
    pairs = RWKV_HEADS // 2
    for hp in range(pairs):
        q = b * pairs + hp
        z_scr[q * RELAYOUT_PITCH:q * RELAYOUT_PITCH + LANES, :] = val[:, hp * LANES:(hp + 1) * LANES].T


def _emit_chains(z_scr, o_ref, nseq, steps, step_major):
    half = nseq * (RWKV_HEADS // 2)
    for j in range(RWKV_HEAD):
        m = jnp.concatenate([z_scr[pl.ds(j, half, stride=RELAYOUT_PITCH), :],
                             z_scr[pl.ds(RWKV_HEAD + j, half, stride=RELAYOUT_PITCH), :]], axis=0)
        mt = m.T
        if step_major:
            o_ref[pl.ds(j, steps, stride=RWKV_HEAD), :] = mt
        else:
            for tq in range(steps // SCAN_STEPS):
                o_ref[tq, j] = mt[tq * SCAN_STEPS:(tq + 1) * SCAN_STEPS]


def _token_shift_mix(x_ref, mu_ref, carry, b, rows):
    x = x_ref[b]
    prev = jnp.where(rows == 0, carry[b, SUBLANES - 1:SUBLANES, :], pltpu.roll(x, 1, axis=0))
    carry[b] = x[x.shape[0] - SUBLANES:, :]
    return x + (prev - x) * mu_ref[...]


def _chain_plain_kernel(x_ref, init_ref, mu_ref, o_ref, carry, z_scr, *, step_major):
    @pl.when(pl.program_id(0) == 0)
    def _():
        carry[...] = init_ref[...]

    nseq, steps, width = x_ref.shape
    rows = _row_iota((steps, width))
    for b in range(nseq):
        _stash_transposed(z_scr, b, _token_shift_mix(x_ref, mu_ref, carry, b, rows))
    _emit_chains(z_scr, o_ref, nseq, steps, step_major)


def _chain_low_rank_kernel(x_ref, init_ref, mu_ref, w0_ref, a0_ref, wa_ref, g2_ref,
                           lw_ref, a_ref, g_ref, carry, z_lw, z_a):
    @pl.when(pl.program_id(0) == 0)
    def _():
        carry[...] = init_ref[...]

    nseq, steps, width = x_ref.shape
    rows = _row_iota((steps, width))
    for b in range(nseq):
        mixed = _token_shift_mix(x_ref, mu_ref, carry, b, rows)
        low = mixed[:, :LANES]
        lane = lax.broadcasted_iota(jnp.int32, low.shape, 1)
        low = jnp.where(lane < 64, jnp.tanh(low), low).astype(BF16)
        pre = _dot(low, wa_ref[...])
        decay_log = -jnp.exp(-_softplus(-(w0_ref[...] + pre[:, :RWKV_WIDTH])) - 0.5)
        _stash_transposed(z_lw, b, decay_log)
        _stash_transposed(z_a, b, jax.nn.sigmoid(a0_ref[...] + pre[:, RWKV_WIDTH:]))
        g_ref[b] = _dot(jax.nn.sigmoid(mixed[:, LANES:]).astype(BF16), g2_ref[...])
    _emit_chains(z_lw, lw_ref, nseq, steps, False)
    _emit_chains(z_a, a_ref, nseq, steps, False)


def _chain_specs(bsz, t_len, step_major):
    if step_major:
        return (pl.BlockSpec((RELAYOUT_STEPS * RWKV_HEAD, LANES), lambda t: (t, 0)),
                jax.ShapeDtypeStruct((t_len * RWKV_HEAD, LANES), F32))
    per = RELAYOUT_STEPS // SCAN_STEPS
    return (pl.BlockSpec((per, RWKV_HEAD, SCAN_STEPS, LANES), lambda t: (t, 0, 0, 0)),
            jax.ShapeDtypeStruct((t_len // SCAN_STEPS, RWKV_HEAD, SCAN_STEPS, LANES), F32))


def _relayout_scratch(bsz):
    return pltpu.VMEM((bsz * (RWKV_HEADS // 2) * RELAYOUT_PITCH, LANES), F32)


def _chain_plain(p3, shift3, mu, col, step_major=False):
    bsz, t_len, _ = p3.shape
    width = RWKV_WIDTH
    out_spec, out_shape = _chain_specs(bsz, t_len, step_major)
    return pl.pallas_call(
        functools.partial(_chain_plain_kernel, step_major=step_major),
        grid=(t_len // RELAYOUT_STEPS,),
        in_specs=[pl.BlockSpec((bsz, RELAYOUT_STEPS, width), lambda t: (0, t, P0_RWKV // width + col)),
                  pl.BlockSpec((bsz, SUBLANES, width), lambda t: (0, 0, col)),
                  pl.BlockSpec((1, width), lambda t: (0, 0))],
        out_specs=out_spec,
        out_shape=out_shape,
        scratch_shapes=[pltpu.VMEM((bsz, SUBLANES, width), F32), _relayout_scratch(bsz)],
        compiler_params=_cparams(("arbitrary",)),
        name="rwkv_plain",
    )(p3, shift3, mu)


def _chain_low_rank(p3, shift3, mu, w0, a0, wa, g2):
    bsz, t_len, _ = p3.shape
    width = RWKV_LORA
    const = lambda t: (0, 0)
    rows_spec, rows_shape = _chain_specs(bsz, t_len, False)
    natural = pl.BlockSpec((bsz, RELAYOUT_STEPS, RWKV_WIDTH), lambda t: (0, t, 0))
    return pl.pallas_call(
        _chain_low_rank_kernel,
        grid=(t_len // RELAYOUT_STEPS,),
        in_specs=[pl.BlockSpec((bsz, RELAYOUT_STEPS, width), lambda t: (0, t, P0_RWKV_LOW // width)),
                  pl.BlockSpec((bsz, SUBLANES, width), lambda t: (0, 0, 3 * RWKV_WIDTH // width)),
                  pl.BlockSpec((1, width), const),
                  pl.BlockSpec((1, RWKV_WIDTH), const),
                  pl.BlockSpec((1, RWKV_WIDTH), const),
                  pl.BlockSpec(wa.shape, const),
                  pl.BlockSpec(g2.shape, const)],
        out_specs=[rows_spec, rows_spec, natural],
        out_shape=[rows_shape, rows_shape, jax.ShapeDtypeStruct((bsz, t_len, RWKV_WIDTH), F32)],
        scratch_shapes=[pltpu.VMEM((bsz, SUBLANES, width), F32), _relayout_scratch(bsz), _relayout_scratch(bsz)],
        compiler_params=_cparams(("arbitrary",)),
        name="rwkv_low_rank",
    )(p3, shift3, mu, w0, a0, wa, g2)


def _from_chains_kernel(y_ref, g_ref, o_ref, z_scr):
    nseq = o_ref.shape[0]
    pairs = RWKV_HEADS // 2
    half = nseq * pairs
    for i in range(RWKV_HEAD):
        mt = y_ref[pl.ds(i, RELAYOUT_STEPS, stride=RWKV_HEAD), :].T
        z_scr[pl.ds(i, half, stride=RELAYOUT_PITCH), :] = mt[:half]
        z_scr[pl.ds(RWKV_HEAD + i, half, stride=RELAYOUT_PITCH), :] = mt[half:]
    for b in range(nseq):
        for hp in range(pairs):
            q = b * pairs + hp
            lanes = slice(hp * LANES, (hp + 1) * LANES)
            y = z_scr[q * RELAYOUT_PITCH:q * RELAYOUT_PITCH + LANES, :].T
            o_ref[b, :, lanes] = (y * g_ref[b, :, lanes]).astype(BF16)


def _from_chains(y2, g3):
    bsz, t_len, _ = g3.shape
    natural = pl.BlockSpec((bsz, RELAYOUT_STEPS, RWKV_WIDTH), lambda t: (0, t, 0))
    return pl.pallas_call(
        _from_chains_kernel,
        grid=(t_len // RELAYOUT_STEPS,),
        in_specs=[pl.BlockSpec((RELAYOUT_STEPS * RWKV_HEAD, LANES), lambda t: (t, 0)), natural],
        out_specs=natural,
        out_shape=jax.ShapeDtypeStruct((bsz, t_len, RWKV_WIDTH), BF16),
        scratch_shapes=[pltpu.VMEM((bsz * (RWKV_HEADS // 2) * RELAYOUT_PITCH, LANES), F32)],
        compiler_params=_cparams(("arbitrary",)),
        name="from_chains",
    )(y2, g3)


def _rwkv_paired(p0, s_shift, s_rwkv, w, bsz, t_len):
    pairs = RWKV_HEADS // 2
    p3 = p0.reshape(bsz, t_len, P0_COLS)
    shift3 = jnp.broadcast_to(s_shift[:, None, :], (bsz, SUBLANES, RWKV_COLS))
    mu = w["rwkv_mu"]
    mu_main = [mu[:, i * RWKV_WIDTH:(i + 1) * RWKV_WIDTH] for i in range(3)]
    mu_low = mu[:, 3 * RWKV_WIDTH:]
    r = _chain_plain(p3, shift3, mu_main[0], 0)
    k = _chain_plain(p3, shift3, mu_main[1], 1)
    v = _chain_plain(p3, shift3, mu_main[2], 2, step_major=True)
    lw, a, g = _chain_low_rank(p3, shift3, mu_low, w["rwkv_w0"], w["rwkv_a0"], w["rwkv_wa"], w["rwkv_g2"])
    s0 = s_rwkv.reshape(bsz, pairs, 2, RWKV_HEAD, RWKV_HEAD).transpose(4, 3, 2, 0, 1)
    y_c, s_c = _rwkv_scan(r, lw, k, a, v.reshape(t_len, RWKV_HEAD, LANES), w["rwkv_tiles_paired"],
                          s0.reshape(RWKV_HEAD, RWKV_HEAD, LANES), SCAN_STEPS)
    y = _from_chains(y_c.reshape(t_len * RWKV_HEAD, LANES), g)
    s_new = s_c.reshape(RWKV_HEAD, RWKV_HEAD, 2, bsz, pairs).transpose(3, 4, 2, 1, 0)
    return y.reshape(bsz * t_len, RWKV_WIDTH), None, s_new.reshape(bsz, RWKV_HEADS, RWKV_HEAD, RWKV_HEAD)


def _seq_lanes_kernel(x_ref, o_ref, *, step_major):
    for t in range(x_ref.shape[1]):
        tile_t = x_ref[:, t, :].T
        for hh in range(2):
            piece = tile_t[hh * RWKV_HEAD:(hh + 1) * RWKV_HEAD]
            lanes = slice(hh * LANES, (hh + 1) * LANES)
            if step_major:
                o_ref[t, :, lanes] = piece
            else:
                o_ref[0, :, t, lanes] = piece


def _seq_lanes(x2d, bsz, t_len, step_major):
    nc = RWKV_HEADS * bsz
    if step_major:
        out_spec = pl.BlockSpec((t_len, RWKV_HEAD, 2 * LANES), lambda hp: (0, 0, hp))
        out_shape = jax.ShapeDtypeStruct((t_len, RWKV_HEAD, nc), F32)
    else:
        out_spec = pl.BlockSpec((1, RWKV_HEAD, t_len, 2 * LANES), lambda hp: (0, 0, 0, hp))
        out_shape = jax.ShapeDtypeStruct((1, RWKV_HEAD, t_len, nc), F32)
    return pl.pallas_call(
        functools.partial(_seq_lanes_kernel, step_major=step_major),
        grid=(RWKV_HEADS // 2,),
        in_specs=[pl.BlockSpec((bsz, t_len, LANES), lambda hp: (0, 0, hp))],
        out_specs=out_spec,
        out_shape=out_shape,
        compiler_params=_cparams(("arbitrary",)),
        name="seq_lanes",
    )(x2d.reshape(bsz, t_len, RWKV_WIDTH))


def _seq_lanes_back_kernel(y_ref, o_ref):
    for t in range(y_ref.shape[0]):
        both = jnp.concatenate([y_ref[t, :, :LANES], y_ref[t, :, LANES:]], axis=0)
        o_ref[:, t, :] = both.T


def _seq_lanes_back(y_c, bsz):
    t_len = y_c.shape[0]
    return pl.pallas_call(
        _seq_lanes_back_kernel,
        grid=(RWKV_HEADS // 2,),
        in_specs=[pl.BlockSpec((t_len, RWKV_HEAD, 2 * LANES), lambda hp: (0, 0, hp))],
        out_specs=pl.BlockSpec((bsz, t_len, LANES), lambda hp: (0, 0, hp)),
        out_shape=jax.ShapeDtypeStruct((bsz, t_len, RWKV_WIDTH), F32),
        compiler_params=_cparams(("arbitrary",)),
        name="seq_lanes_back",
    )(y_c)


def _rwkv_seq_lanes(p0, s_shift, s_rwkv, w, bsz, t_len):
    shift8 = jnp.broadcast_to(s_shift[:, None, :], (bsz, SUBLANES, RWKV_COLS)).reshape(bsz * SUBLANES, RWKV_COLS)
    r, lw, k, v, a, g = _rwkv_prep(p0, shift8, w["rwkv_mu"], w["rwkv_w0"], w["rwkv_a0"], w["rwkv_wa"],
                                   w["rwkv_g2"], bsz, t_len)
    rows = [_seq_lanes(t, bsz, t_len, False) for t in (r, lw, k, a)]
    v_c = _seq_lanes(v, bsz, t_len, True)
    s0 = s_rwkv.transpose(3, 2, 1, 0).reshape(RWKV_HEAD, RWKV_HEAD, RWKV_HEADS * bsz)
    y_c, s_c = _rwkv_scan(*rows, v_c, w["rwkv_tiles_heads"], s0, t_len)
    y = _seq_lanes_back(y_c, bsz).reshape(bsz * t_len, RWKV_WIDTH)
    return y, g, s_c.reshape(RWKV_HEAD, RWKV_HEAD, RWKV_HEADS, bsz).transpose(3, 2, 1, 0)


def _chain_tile_heads(p):
    return jnp.broadcast_to(p.reshape(RWKV_HEADS * RWKV_HEAD, 1), (RWKV_HEADS * RWKV_HEAD, LANES))


def _chain_tile_paired(p):
    pairs = RWKV_HEADS // 2
    t = p.reshape(pairs, 2, RWKV_HEAD).transpose(2, 1, 0)[:, :, None, :]
    return jnp.broadcast_to(t, (RWKV_HEAD, 2, LANES // RWKV_HEADS, pairs)).reshape(RWKV_HEAD, LANES)


def _lru_kernel(gate_ref, x_ref, cinit_ref, cw_ref, cb_ref, wri_ref, br_ref, bi_ref, lam_ref, h0_ref,
                o_ref, hfin_ref, halo, hprev, *, tt, bb):
    j = pl.program_id(1)

    @pl.when(j == 0)
    def _():
        halo[...] = cinit_ref[...]
        for i in range(bb):
            hprev[i] = jnp.broadcast_to(h0_ref[i], (SUBLANES, LRU_WIDTH))

    rows = _row_iota((tt, LRU_BLOCK))
    for i in range(bb):
        u = x_ref[i]
        ext = jnp.concatenate([halo[i], u], axis=0)
        xc = cb_ref[...] + _shifted(ext, 3, tt) * cw_ref[0:1, :]
        xc = xc + _shifted(ext, 2, tt) * cw_ref[1:2, :]
        xc = xc + _shifted(ext, 1, tt) * cw_ref[2:3, :]
        xc = xc + u * cw_ref[3:4, :]
        halo[i] = u[tt - SUBLANES:tt, :]

        for n in range(LRU_BLOCKS):
            cs = slice(n * LRU_BLOCK, (n + 1) * LRU_BLOCK)
            xb = xc[:, cs]
            ri = _dot(xb.astype(BF16), wri_ref[n])
            rg = jax.nn.sigmoid(ri[:, :LRU_BLOCK] + br_ref[:, cs])
            ig = jax.nn.sigmoid(ri[:, LRU_BLOCK:] + bi_ref[:, cs])
            log_a = -LRU_C * rg * _softplus(-lam_ref[:, cs])
            a = jnp.exp(log_a)
            b = jnp.sqrt(-jnp.tanh(log_a) * (a * a + 1.0)) * ig * xb
            d = 1
            while d < SUBLANES:
                keep = (rows & (SUBLANES - 1)) >= d
                a_sh = jnp.where(keep, pltpu.roll(a, d, axis=0), 1.0)
                b_sh = jnp.where(keep, pltpu.roll(b, d, axis=0), 0.0)
                b = b + a * b_sh
                a = a * a_sh
                d *= 2
            h_in = hprev[i, 0:1, cs]
            groups = []
            for r0 in range(0, tt, SUBLANES):
                h_grp = a[r0:r0 + SUBLANES] * h_in + b[r0:r0 + SUBLANES]
                groups.append(h_grp)
                h_in = h_grp[SUBLANES - 1:SUBLANES, :]
            h = jnp.concatenate(groups, axis=0)
            o_ref[i, :, cs] = (h * _gelu_tanh(gate_ref[i, :, cs])).astype(BF16)
            h_last = h[tt - 1:tt, :]
            hprev[i, :, cs] = jnp.broadcast_to(h_last, (SUBLANES, LRU_BLOCK))
            hfin_ref[i, :, cs] = h_last


def _lru(p1, cinit, cw, cb, wri, br, bi, lam, h0, bsz, t_len):
    tt = min(t_len, 64)
    nt = t_len // tt
    bb = _batch_block(bsz, t_len)
    p3 = p1.reshape(bsz, t_len, P1_COLS)
    rows = lambda w: (lambda b, j: (b, j, w))
    per_seq = lambda b, j: (b, 0, 0)
    const2 = lambda b, j: (0, 0)
    o, h_fin = pl.pallas_call(
        functools.partial(_lru_kernel, tt=tt, bb=bb),
        grid=(bsz // bb, nt),
        in_specs=[
            pl.BlockSpec((bb, tt, LRU_WIDTH), rows(P1_LRU_GATE // LRU_WIDTH)),
            pl.BlockSpec((bb, tt, LRU_WIDTH), rows(P1_LRU_X // LRU_WIDTH)),
            pl.BlockSpec((bb, SUBLANES, LRU_WIDTH), per_seq),
            pl.BlockSpec((SUBLANES, LRU_WIDTH), const2),
            pl.BlockSpec((1, LRU_WIDTH), const2),
            pl.BlockSpec((LRU_BLOCKS, LRU_BLOCK, 2 * LRU_BLOCK), lambda b, j: (0, 0, 0)),
            pl.BlockSpec((1, LRU_WIDTH), const2),
            pl.BlockSpec((1, LRU_WIDTH), const2),
            pl.BlockSpec((1, LRU_WIDTH), const2),
            pl.BlockSpec((bb, 1, LRU_WIDTH), per_seq),
        ],
        out_specs=[pl.BlockSpec((bb, tt, LRU_WIDTH), rows(0)),
                   pl.BlockSpec((bb, 1, LRU_WIDTH), per_seq)],
        out_shape=[jax.ShapeDtypeStruct((bsz, t_len, LRU_WIDTH), BF16),
                   jax.ShapeDtypeStruct((bsz, 1, LRU_WIDTH), F32)],
        scratch_shapes=[pltpu.VMEM((bb, SUBLANES, LRU_WIDTH), F32), pltpu.VMEM((bb, SUBLANES, LRU_WIDTH), F32)],
        compiler_params=_cparams(("parallel", "arbitrary")),
        name="lru",
    )(p3, p3, cinit, cw, cb, wri, br, bi, lam, h0)
    return o.reshape(bsz * t_len, LRU_WIDTH), h_fin


def _expand_heads(x, e_ref, terms):
    parts = []
    rest = x
    for _ in range(terms):
        part = rest.astype(BF16).astype(F32)
        parts.append(part)
        rest = rest - part
    rows = x.shape[0]
    out = _dot(jnp.concatenate(parts, axis=0), e_ref[...])
    acc = out[:rows]
    for t in range(1, terms):
        acc = acc + out[t * rows:(t + 1) * rows]
    return acc


def _ssd_kernel(z_ref, xbc_ref, dt_ref, cinit_ref, cw_ref, cb_ref, dtb_ref, alog_ref, dd_ref, nw_ref,
                ej_ref, ep_ref, s0_ref, o_ref, s_ref, halo, st_t, *, c, mm, bb):
    j = pl.program_id(1)
    hg = SSD_HEADS // SSD_GROUPS
    gw = SSD_INNER // SSD_GROUPS
    pair = 2 * SSD_HEADDIM

    @pl.when(j == 0)
    def _():
        halo[...] = cinit_ref[...]
        for i in range(bb):
            for q in range(SSD_HEADS // 2):
                two = jnp.concatenate([s0_ref[i, 2 * q], s0_ref[i, 2 * q + 1]], axis=0)
                st_t[i, :, q * pair:(q + 1) * pair] = two.T

    wj = SSD_HEADS * c
    row_t = _row_iota((c, wj))
    lane_t = lax.broadcasted_iota(jnp.int32, (c, wj), 1) & (c - 1)
    causal = lane_t <= row_t
    diag = lane_t == row_t
    bd_rows = hg * c
    bd_mask = ((_row_iota((bd_rows, gw)) >> (c.bit_length() - 1))
               == (lax.broadcasted_iota(jnp.int32, (bd_rows, gw), 1) >> (SSD_HEADDIM.bit_length() - 1)))
    neg_a = -jnp.exp(alog_ref[...])
    gs = SSD_GROUPS * SSD_STATE
    for i in range(bb):
        u = xbc_ref[i]
        ext = jnp.concatenate([halo[i], u], axis=0)
        xc = cb_ref[...] + _shifted(ext, 3, c) * cw_ref[0:1, :]
        xc = xc + _shifted(ext, 2, c) * cw_ref[1:2, :]
        xc = xc + _shifted(ext, 1, c) * cw_ref[2:3, :]
        xc = _silu(xc + u * cw_ref[3:4, :])
        halo[i] = u[c - SUBLANES:c, :]

        xs = xc[:, :SSD_INNER]
        bm = xc[:, SSD_INNER:SSD_INNER + gs].astype(mm)
        cm = xc[:, SSD_INNER + gs:].astype(mm)
        dt = _softplus(dt_ref[i] + dtb_ref[...])
        cs = _cumsum_rows(dt * neg_a, c)
        cs_p = _expand_heads(cs, ep_ref, 3)
        dt_p = _expand_heads(dt, ep_ref, 2)
        cs_j = cs_p if c == SSD_HEADDIM else _expand_heads(cs, ej_ref, 3)
        cs_row = jnp.sum(jnp.where(diag, cs_j, 0.0), axis=0, keepdims=True)
        lmat = jnp.exp(jnp.where(causal, cs_j - cs_row, -jnp.inf))
        last_p = cs_p[c - 1:c, :]
        e_cs = jnp.exp(cs_p)
        e_last = jnp.exp(last_p)
        xdt = xs * dt_p
        xw = (xdt * jnp.exp(last_p - cs_p)).astype(mm)
        xdt = xdt.astype(mm)
        y_groups = []
        for g in range(SSD_GROUPS):
            gl = slice(g * gw, (g + 1) * gw)
            ss = slice(g * SSD_STATE, (g + 1) * SSD_STATE)
            cb_x = _dot_nt(cm[:, ss], jnp.concatenate([bm[:, ss]] * hg, axis=0))
            m_x = (cb_x * lmat[:, g * bd_rows:(g + 1) * bd_rows]).astype(mm)
            blocks = jnp.where(bd_mask, jnp.concatenate([xdt[:, gl]] * hg, axis=0), jnp.zeros((), mm))
            st = st_t[i, :, gl]
            y_groups.append(_dot(m_x, blocks) + e_cs[:, gl] * _dot(cm[:, ss], st.astype(mm)))
            st_t[i, :, gl] = st * e_last[:, gl] + _dot_tn(bm[:, ss], xw[:, gl])
        y = (jnp.concatenate(y_groups, axis=1) + dd_ref[...] * xs) * _silu(z_ref[i])
        for g in range(SSD_GROUPS):
            gl = slice(g * gw, (g + 1) * gw)
            o_ref[i, :, gl] = _rms(y[:, gl], nw_ref[:, gl], 1e-5).astype(BF16)

    @pl.when(j == pl.num_programs(1) - 1)
    def _():
        for i in range(bb):
            for q in range(SSD_HEADS // 2):
                two = st_t[i, :, q * pair:(q + 1) * pair].T
                s_ref[i, 2 * q] = two[:SSD_HEADDIM]
                s_ref[i, 2 * q + 1] = two[SSD_HEADDIM:]


def _ssd(p1, cinit, cw, cb, dtb, alog, dd, nw, s0, bsz, t_len):
    c = min(SSD_CHUNK, t_len)
    nt = t_len // c
    bb = _batch_block(bsz, t_len)
    mm = BF16 if c % 16 == 0 else F32
    p3 = p1.reshape(bsz, t_len, P1_COLS)
    rows = lambda w: (lambda b, j: (b, j, w))
    const2 = lambda b, j: (0, 0)
    st_spec = pl.BlockSpec((bb, SSD_HEADS, SSD_HEADDIM, SSD_STATE), lambda b, j: (b, 0, 0, 0))
    head = jnp.arange(LANES, dtype=jnp.int32)[:, None]
    expand = lambda width: (jnp.arange(SSD_HEADS * width, dtype=jnp.int32)[None, :] // width == head).astype(F32)
    o, s_new = pl.pallas_call(
        functools.partial(_ssd_kernel, c=c, mm=mm, bb=bb),
        grid=(bsz // bb, nt),
        in_specs=[
            pl.BlockSpec((bb, c, SSD_INNER), rows(P1_SSD_Z // SSD_INNER)),
            pl.BlockSpec((bb, c, SSD_CONV_CH), rows(P1_SSD_XBC // SSD_CONV_CH)),
            pl.BlockSpec((bb, c, LANES), rows(P1_SSD_DT // LANES)),
            pl.BlockSpec((bb, SUBLANES, SSD_CONV_CH), lambda b, j: (b, 0, 0)),
            pl.BlockSpec((SUBLANES, SSD_CONV_CH), const2),
            pl.BlockSpec((1, SSD_CONV_CH), const2),
            pl.BlockSpec((1, LANES), const2),
            pl.BlockSpec((1, LANES), const2),
            pl.BlockSpec((1, SSD_INNER), const2),
            pl.BlockSpec((1, SSD_INNER), const2),
            pl.BlockSpec((LANES, SSD_HEADS * c), const2),
            pl.BlockSpec((LANES, SSD_INNER), const2),
            st_spec,
        ],
        out_specs=[pl.BlockSpec((bb, c, SSD_INNER), rows(0)), st_spec],
        out_shape=[jax.ShapeDtypeStruct((bsz, t_len, SSD_INNER), BF16),
                   jax.ShapeDtypeStruct((bsz, SSD_HEADS, SSD_HEADDIM, SSD_STATE), F32)],
        scratch_shapes=[pltpu.VMEM((bb, SUBLANES, SSD_CONV_CH), F32),
                        pltpu.VMEM((bb, SSD_STATE, SSD_INNER), F32)],
        compiler_params=_cparams(("parallel", "arbitrary")),
        name="ssd",
    )(p3, p3, p3, cinit, cw, cb, dtb, alog, dd, nw, expand(c), expand(SSD_HEADDIM), s0)
    return o.reshape(bsz * t_len, SSD_INNER), s_new


def _pad_rows_to_8(x):
    return jnp.pad(x, ((0, 0), (SUBLANES - x.shape[1], 0), (0, 0)))


def _pad_lanes(x, width):
    return jnp.pad(x, ((0, 0), (0, width - x.shape[1])))


def _trunk(x, states, w):
    s_gla, s_rwkv, s_shift, s_lru, s_lru_conv, s_ssd, s_ssd_conv = states
    bsz, t_len, d = x.shape
    n = bsz * t_len
    x2 = x.reshape(n, d)

    p0 = _in_proj(x2, w["g_mix0"], w["w_in0"], P0_TN)
    o_gla, s_gla_new = _gla(p0, s_gla, w["gla_wa2"], w["gla_ba"], w["gla_gn"], bsz, t_len)
    if bsz * RWKV_HEADS == LANES and t_len % RELAYOUT_STEPS == 0:
        y_rwkv, g, s_rwkv_new = _rwkv_paired(p0, s_shift, s_rwkv, w, bsz, t_len)
    else:
        assert bsz == LANES and t_len == SUBLANES, "rwkv layouts exist for (8, 128k) and (128, 8) groups"
        y_rwkv, g, s_rwkv_new = _rwkv_seq_lanes(p0, s_shift, s_rwkv, w, bsz, t_len)
    shift_new = p0.reshape(bsz, t_len, P0_COLS)[:, -1, P0_RWKV:P0_RWKV + RWKV_COLS]
    x2 = _out_ffn(x2, o_gla, y_rwkv, g, w["w_out0"], w["g_ffn0"], w["wg0"], w["wu0"], w["wd0"], None)

    p1 = _in_proj(x2, w["g_mix1"], w["w_in1"], P1_TN)
    lru_out, h_lru = _lru(p1, _pad_rows_to_8(s_lru_conv), w["lru_cw"], w["lru_cb"], w["lru_wri"], w["lru_br"],
                          w["lru_bi"], w["lru_lam"], s_lru.reshape(bsz, 1, LRU_WIDTH), bsz, t_len)
    y_ssd, s_ssd_new = _ssd(p1, _pad_rows_to_8(s_ssd_conv), w["ssd_cw"], w["ssd_cb"], w["ssd_dtb"], w["ssd_alog"],
                            w["ssd_dd"], w["ssd_nw"], s_ssd, bsz, t_len)
    keep = CONV_W - 1
    p1_tail = p1.reshape(bsz, t_len, P1_COLS)[:, -keep:]
    lru_conv_new = p1_tail[:, :, P1_LRU_X:P1_LRU_X + LRU_WIDTH]
    ssd_conv_new = p1_tail[:, :, P1_SSD_XBC:P1_SSD_XBC + SSD_CONV_CH]
    y = _out_ffn(x2, lru_out, y_ssd, None, w["w_out1"], w["g_ffn1"], w["wg1"], w["wu1"], w["wd1"], w["g_final"])
    return y.reshape(bsz, t_len, d), (s_gla_new, s_rwkv_new, shift_new, h_lru.reshape(bsz, LRU_WIDTH),
                                      lru_conv_new, s_ssd_new, ssd_conv_new)


def kernel(x_prompt, x_sample, state_gla, state_rwkv, state_rwkv_shift, state_lru, state_lru_conv, state_ssd, state_ssd_conv, w_in0, gla_w_a2, gla_b_a, gla_g_norm, rwkv_mu, rwkv_w0, rwkv_w2, rwkv_a0, rwkv_a2, rwkv_g2, rwkv_k_k, rwkv_k_a, rwkv_r_k, rwkv_ln_w, rwkv_ln_b, w_out0, w_in1, lru_conv_w, lru_conv_b, lru_w_r, lru_b_r, lru_w_i, lru_b_i, lru_lambda, ssd_conv_w, ssd_conv_b, ssd_dt_bias, ssd_a_log, ssd_d, ssd_norm_w, w_out1, g_mix, g_ffn, w_ffn_gate, w_ffn_up, w_ffn_down, g_final):
    row = lambda p: p.reshape(1, -1).astype(F32)
    w0 = jnp.concatenate([w_in0[:, :2048], w_in0[:, 2064:3088], w_in0[:, 3088:],
                          _pad_lanes(w_in0[:, 2048:2064], LANES)], axis=1)
    zeros64 = jnp.zeros((64, RWKV_WIDTH), F32)
    rwkv_wa = jnp.concatenate([jnp.concatenate([rwkv_w2, zeros64], axis=1),
                               jnp.concatenate([zeros64, rwkv_a2], axis=1)], axis=0)
    w = {
        "w_in0": w0.astype(BF16),
        "w_in1": _pad_lanes(w_in1, P1_COLS).astype(BF16),
        "g_mix0": g_mix[0], "g_mix1": g_mix[1], "g_ffn0": g_ffn[0], "g_ffn1": g_ffn[1], "g_final": g_final,
        "gla_wa2": jnp.pad(gla_w_a2, ((0, LANES - GLA_GATE_RANK), (0, 0))).astype(BF16),
        "gla_ba": row(gla_b_a), "gla_gn": row(gla_g_norm),
        "rwkv_mu": row(rwkv_mu), "rwkv_w0": row(rwkv_w0), "rwkv_a0": row(rwkv_a0),
        "rwkv_wa": rwkv_wa.astype(BF16), "rwkv_g2": rwkv_g2.astype(BF16),
        "rwkv_tiles_heads": [_chain_tile_heads(p) for p in (rwkv_k_k, rwkv_k_a, rwkv_r_k, rwkv_ln_w, rwkv_ln_b)],
        "rwkv_tiles_paired": [_chain_tile_paired(p)
                              for p in (rwkv_k_k, rwkv_k_a, rwkv_r_k, rwkv_ln_w, rwkv_ln_b)],
        "w_out0": w_out0.astype(BF16), "w_out1": w_out1.astype(BF16),
        "wg0": w_ffn_gate[0].astype(BF16), "wu0": w_ffn_up[0].astype(BF16), "wd0": w_ffn_down[0].astype(BF16),
        "wg1": w_ffn_gate[1].astype(BF16), "wu1": w_ffn_up[1].astype(BF16), "wd1": w_ffn_down[1].astype(BF16),
        "lru_cw": jnp.pad(lru_conv_w, ((0, SUBLANES - CONV_W), (0, 0))), "lru_cb": row(lru_conv_b),
        "lru_wri": jnp.concatenate([lru_w_r, lru_w_i], axis=2).astype(BF16),
        "lru_br": row(lru_b_r), "lru_bi": row(lru_b_i), "lru_lam": row(lru_lambda),
        "ssd_cw": jnp.pad(ssd_conv_w, ((0, SUBLANES - CONV_W), (0, 0))), "ssd_cb": row(ssd_conv_b),
        "ssd_dtb": _pad_lanes(row(ssd_dt_bias), LANES), "ssd_alog": _pad_lanes(row(ssd_a_log), LANES),
        "ssd_dd": row(jnp.repeat(ssd_d, SSD_HEADDIM)), "ssd_nw": row(ssd_norm_w),
    }
    bp = x_prompt.shape[0]
    prompt_init = (
        jnp.zeros((bp, GLA_HEADS, GLA_DK, GLA_DV), F32),
        jnp.zeros((bp, RWKV_HEADS, RWKV_HEAD, RWKV_HEAD), F32),
        jnp.zeros((bp, RWKV_COLS), F32),
        jnp.zeros((bp, LRU_WIDTH), F32),
        jnp.zeros((bp, CONV_W - 1, LRU_WIDTH), F32),
        jnp.zeros((bp, SSD_HEADS, SSD_HEADDIM, SSD_STATE), F32),
        jnp.zeros((bp, CONV_W - 1, SSD_CONV_CH), F32),
    )
    sample_init = (state_gla, state_rwkv, state_rwkv_shift, state_lru, state_lru_conv, state_ssd, state_ssd_conv)
    y_prompt, p_states = _trunk(x_prompt, prompt_init, w)
    y_sample, s_states = _trunk(x_sample, sample_init, w)
    return (y_prompt, y_sample, *p_states, *s_states)
```
